```python
import jax, jax.numpy as jnp
from jax import lax
import numpy as np

D_MODEL = 2048
BATCH = 8
SEQ = 2048
DEPTH = 1

CHUNK = 64
N_MEM = 256
EPS = 1e-6
SB_HEAD_DIM = 128
SB_WIDTH = D_MODEL // 2
SB_HEADS = SB_WIDTH // SB_HEAD_DIM
SB_BLOCK = 128
SSM_INNER = D_MODEL // 2
SSM_HEAD_DIM = 64
SSM_HEADS = SSM_INNER // SSM_HEAD_DIM
SSM_GROUPS = 4
SSM_STATE = 128
SSM_CONV = 4
SSM_CHUNK = CHUNK
CONV_DIM = SSM_INNER + 2 * SSM_GROUPS * SSM_STATE
MIX_WIDTH = SB_WIDTH + SSM_INNER
IN_PROJ_DIM = 3 * SB_WIDTH + SSM_INNER + CONV_DIM + SSM_HEADS
MEM_HEADS = 4
MEM_HEAD_DIM = 128
MEM_WIDTH = MEM_HEADS * MEM_HEAD_DIM
D_FF = ((8 * D_MODEL // 3 + 127) // 128) * 128

kernel_name = "hybrid_sb_ssd_macaron_memory_layer"


def rms_norm(x, w):
    x32 = x.astype(jnp.float32)
    y = x32 * lax.rsqrt(jnp.mean(x32 * x32, axis=-1, keepdims=True) + EPS)
    return (y * w.astype(jnp.float32)).astype(x.dtype)


def swiglu(x, wg, wu, wd):
    return (jax.nn.silu(x @ wg) * (x @ wu)) @ wd


def stick_breaking(q, k, v):
    b, h, s, d = q.shape
    nblk = s // SB_BLOCK
    scale = d ** -0.5
    qb = q.reshape(b, h, nblk, SB_BLOCK, d).transpose(2, 0, 1, 3, 4)
    k32 = k.astype(jnp.float32)
    kpos = jnp.arange(s)

    def block(args):
        i, qi = args
        z = jnp.einsum('bhqd,bhkd->bhqk', qi.astype(jnp.float32), k32) * scale
        qpos = i * SB_BLOCK + jnp.arange(SB_BLOCK)
        mask = kpos[None, :] < qpos[:, None]
        log_keep = jnp.where(mask, jax.nn.log_sigmoid(-z), 0.0)
        tail = lax.cumsum(log_keep, axis=3, reverse=True)
        tail_excl = jnp.concatenate([tail[..., 1:], jnp.zeros_like(tail[..., :1])], axis=-1)
        log_a = jnp.where(mask, jax.nn.log_sigmoid(z) + tail_excl, -jnp.inf)
        a = jnp.exp(log_a)
        return jnp.einsum('bhqk,bhkd->bhqd', a.astype(v.dtype), v)

    out = lax.map(block, (jnp.arange(nblk), qb))
    return out.transpose(1, 2, 0, 3, 4).reshape(b, h, s, d)


def causal_dwconv(x, w, bias):
    kw = w.shape[0]
    y = lax.conv_general_dilated(x, w[:, None, :], window_strides=(1,), padding=[(kw - 1, 0)],
                                 dimension_numbers=('NWC', 'WIO', 'NWC'),
                                 feature_group_count=x.shape[-1])
    return y + bias


def segsum_exp(a):
    l = a.shape[-1]
    cs = jnp.cumsum(a, axis=-1)
    diff = cs[..., :, None] - cs[..., None, :]
    mask = jnp.tril(jnp.ones((l, l), dtype=bool))
    return jnp.exp(jnp.where(mask, diff, -jnp.inf))


def ssd(x, dt, a, bm, cm):
    b, s, h, p = x.shape
    g, n = bm.shape[2], bm.shape[3]
    r = h // g
    l = SSM_CHUNK
    c = s // l
    xd = (x * dt[..., None]).reshape(b, c, l, g, r, p)
    da = (dt * a).reshape(b, c, l, g, r).transpose(0, 1, 3, 4, 2)
    bm = bm.reshape(b, c, l, g, n)
    cm = cm.reshape(b, c, l, g, n)
    da_cs = jnp.cumsum(da, axis=-1)
    decay = segsum_exp(da)
    cb = jnp.einsum('bclgn,bcsgn->bcgls', cm, bm)
    y_diag = jnp.einsum('bcgrls,bcsgrp->bclgrp', cb[:, :, :, None] * decay, xd)
    decay_to_end = jnp.exp(da_cs[..., -1:] - da_cs)
    states = jnp.einsum('bclgn,bcgrl,bclgrp->bcgrpn', bm, decay_to_end, xd)
    chunk_decay = jnp.exp(da_cs[..., -1])

    def step(carry, inp):
        st, dec = inp
        return carry * dec[..., None, None] + st, carry

    init = jnp.zeros((b, g, r, p, n), jnp.float32)
    _, prev = lax.scan(step, init, (states.transpose(1, 0, 2, 3, 4, 5), chunk_decay.transpose(1, 0, 2, 3)))
    prev = prev.transpose(1, 0, 2, 3, 4, 5)
    y_off = jnp.einsum('bclgn,bcgrpn,bcgrl->bclgrp', cm, prev, jnp.exp(da_cs))
    return (y_diag + y_off).reshape(b, s, h, p)


def parallel_mixer(h, pre, w_in, conv_w, conv_b, dt_bias, a_log, d_skip, sb_norm, ssm_norm, w_out, post):
    b, s, _ = h.shape
    u = rms_norm(h, pre)
    proj = u @ w_in
    q, k, v, z, xbc, dt_raw = jnp.split(
        proj, np.cumsum([SB_WIDTH, SB_WIDTH, SB_WIDTH, SSM_INNER, CONV_DIM]).tolist(), axis=-1)
    heads = lambda t: t.reshape(b, s, SB_HEADS, SB_HEAD_DIM).transpose(0, 2, 1, 3)
    sb = stick_breaking(heads(q), heads(k), heads(v)).transpose(0, 2, 1, 3).reshape(b, s, SB_WIDTH)
    sb = rms_norm(sb, sb_norm)
    xbc = jax.nn.silu(causal_dwconv(xbc, conv_w, conv_b))
    xs, bm, cm = jnp.split(xbc, [SSM_INNER, SSM_INNER + SSM_GROUPS * SSM_STATE], axis=-1)
    xs = xs.reshape(b, s, SSM_HEADS, SSM_HEAD_DIM).astype(jnp.float32)
    bm = bm.reshape(b, s, SSM_GROUPS, SSM_STATE).astype(jnp.float32)
    cm = cm.reshape(b, s, SSM_GROUPS, SSM_STATE).astype(jnp.float32)
    dt = jax.nn.softplus(dt_raw.astype(jnp.float32) + dt_bias.astype(jnp.float32))
    a = -jnp.exp(a_log.astype(jnp.float32))
    y = ssd(xs, dt, a, bm, cm) + d_skip.astype(jnp.float32)[:, None] * xs
    y = y.reshape(b, s, SSM_INNER) * jax.nn.silu(z.astype(jnp.float32))
    y = rms_norm(y, ssm_norm).astype(h.dtype)
    out = jnp.concatenate([sb, y], axis=-1) @ w_out
    return rms_norm(out, post)


def memory_xattn(h, mem, pre, kv_norm, w_mq, w_mkv, w_mo, post):
    b, s, _ = h.shape
    u = rms_norm(h, pre)
    m = rms_norm(mem, kv_norm)
    q = (u @ w_mq).reshape(b, s, MEM_HEADS, MEM_HEAD_DIM)
    k, v = jnp.split((m @ w_mkv).reshape(b, -1, 2, MEM_HEADS, MEM_HEAD_DIM), 2, axis=2)
    k, v = k[:, :, 0], v[:, :, 0]
    scores = jnp.einsum('bshd,bmhd->bhsm', q, k).astype(jnp.float32) * (MEM_HEAD_DIM ** -0.5)
    p = jax.nn.softmax(scores, axis=-1)
    o = jnp.einsum('bhsm,bmhd->bshd', p.astype(v.dtype), v).reshape(b, s, MEM_WIDTH)
    return rms_norm(o @ w_mo, post)


def setup_inputs(seed: int = 0) -> dict:
    key = jax.random.key(seed)
    ks = iter(jax.random.split(key, 40))
    f32 = jnp.float32
    L = DEPTH

    def normal(shape, scale):
        return scale * jax.random.normal(next(ks), shape, f32)

    def gain(dim):
        return 1.0 + 0.05 * jax.random.normal(next(ks), (L, dim), f32)

    def ffn_params():
        return (gain(D_MODEL), normal((L, D_MODEL, D_FF), D_MODEL ** -0.5),
                normal((L, D_MODEL, D_FF), D_MODEL ** -0.5), normal((L, D_FF, D_MODEL), D_FF ** -0.5),
                gain(D_MODEL))

    x = normal((BATCH, SEQ, D_MODEL), 1.0)
    mem = normal((BATCH, N_MEM, D_MODEL), 1.0)
    f1 = ffn_params()
    mix_pre = gain(D_MODEL)
    w_in = normal((L, D_MODEL, IN_PROJ_DIM), D_MODEL ** -0.5)
    conv_w = normal((L, SSM_CONV, CONV_DIM), SSM_CONV ** -0.5)
    conv_b = normal((L, CONV_DIM), 0.01)
    dt0 = jnp.exp(jax.random.uniform(next(ks), (L, SSM_HEADS), f32, np.log(1e-3), np.log(1e-1)))
    dt_bias = dt0 + jnp.log(-jnp.expm1(-dt0))
    a_log = jnp.log(jax.random.uniform(next(ks), (L, SSM_HEADS), f32, 1.0, 16.0))
    d_skip = 1.0 + 0.05 * jax.random.normal(next(ks), (L, SSM_HEADS), f32)
    sb_norm = gain(SB_WIDTH)
    ssm_norm = gain(SSM_INNER)
    w_out = normal((L, MIX_WIDTH, D_MODEL), MIX_WIDTH ** -0.5)
    mix_post = gain(D_MODEL)
    mem_pre = gain(D_MODEL)
    mem_kv_norm = gain(D_MODEL)
    w_mq = normal((L, D_MODEL, MEM_WIDTH), D_MODEL ** -0.5)
    w_mkv = normal((L, D_MODEL, 2 * MEM_WIDTH), D_MODEL ** -0.5)
    w_mo = normal((L, MEM_WIDTH, D_MODEL), MEM_WIDTH ** -0.5)
    mem_post = gain(D_MODEL)
    f2 = ffn_params()
    return {"x": x, "mem": mem,
            "ffn1_pre": f1[0], "ffn1_wg": f1[1], "ffn1_wu": f1[2], "ffn1_wd": f1[3], "ffn1_post": f1[4],
            "mix_pre": mix_pre, "w_in": w_in, "conv_w": conv_w, "conv_b": conv_b, "dt_bias": dt_bias,
            "a_log": a_log, "d_skip": d_skip, "sb_norm": sb_norm, "ssm_norm": ssm_norm, "w_out": w_out,
            "mix_post": mix_post, "mem_pre": mem_pre, "mem_kv_norm": mem_kv_norm, "w_mq": w_mq,
            "w_mkv": w_mkv, "w_mo": w_mo, "mem_post": mem_post,
            "ffn2_pre": f2[0], "ffn2_wg": f2[1], "ffn2_wu": f2[2], "ffn2_wd": f2[3], "ffn2_post": f2[4]}


def reference(x, mem,
              ffn1_pre, ffn1_wg, ffn1_wu, ffn1_wd, ffn1_post,
              mix_pre, w_in, conv_w, conv_b, dt_bias, a_log, d_skip, sb_norm, ssm_norm, w_out, mix_post,
              mem_pre, mem_kv_norm, w_mq, w_mkv, w_mo, mem_post,
              ffn2_pre, ffn2_wg, ffn2_wu, ffn2_wd, ffn2_post):
    h = x
    for l in range(DEPTH):
        h = h + 0.5 * rms_norm(swiglu(rms_norm(h, ffn1_pre[l]), ffn1_wg[l], ffn1_wu[l], ffn1_wd[l]), ffn1_post[l])
        h = h + parallel_mixer(h, mix_pre[l], w_in[l], conv_w[l], conv_b[l], dt_bias[l], a_log[l], d_skip[l],
                               sb_norm[l], ssm_norm[l], w_out[l], mix_post[l])
        h = h + memory_xattn(h, mem, mem_pre[l], mem_kv_norm[l], w_mq[l], w_mkv[l], w_mo[l], mem_post[l])
        h = h + 0.5 * rms_norm(swiglu(rms_norm(h, ffn2_pre[l]), ffn2_wg[l], ffn2_wu[l], ffn2_wd[l]), ffn2_post[l])
    return h
```

```python
import functools

import jax
import jax.numpy as jnp
from jax import lax
from jax.experimental import pallas as pl
from jax.experimental.pallas import tpu as pltpu

F32 = jnp.float32
BF16 = jnp.bfloat16

EPS = 1e-6
LANES = 128
SUBLANES = 8
VMEM_BYTES_V7X = 64 * 1024 * 1024
MIB = 1024 * 1024

SB_HEADS = 8
SB_HEAD_DIM = 128
SB_WIDTH = SB_HEADS * SB_HEAD_DIM
SSM_HEADS = 16
SSM_HEAD_DIM = 64
SSM_INNER = SSM_HEADS * SSM_HEAD_DIM
SSM_GROUPS = 4
SSM_HEADS_PER_GROUP = SSM_HEADS // SSM_GROUPS
SSM_STATE = 128
SSM_CONV = 4
SSM_GROUP_WIDTH = SSM_HEADS_PER_GROUP * SSM_HEAD_DIM
CONV_DIM = SSM_INNER + 2 * SSM_GROUPS * SSM_STATE
MEM_HEADS = 4
MEM_HEAD_DIM = 128
MEM_WIDTH = MEM_HEADS * MEM_HEAD_DIM

COL_Q = 0
COL_K = SB_WIDTH
COL_V = 2 * SB_WIDTH
COL_Z = 3 * SB_WIDTH
COL_XBC = COL_Z + SSM_INNER
COL_DT = COL_XBC + CONV_DIM
PROJ_WIDTH = COL_DT + LANES


def _params(n_grid_axes, vmem_mib):
    return pltpu.CompilerParams(
        dimension_semantics=("arbitrary",) * n_grid_axes,
        vmem_limit_bytes=min(vmem_mib * MIB, VMEM_BYTES_V7X - 4 * MIB),
    )


def _rms(x, w):
    ms = jnp.mean(x * x, axis=-1, keepdims=True)
    return x * lax.rsqrt(ms + EPS) * w


def _silu(x):
    return x * jax.nn.sigmoid(x)


def _softplus(x):
    return jnp.maximum(x, 0.0) + jnp.log1p(jnp.exp(-jnp.abs(x)))


def _split_bf16(x):
    hi = x.astype(BF16)
    lo = (x - hi.astype(F32)).astype(BF16)
    return hi, lo


def _dot(a, b):
    return jnp.dot(a, b, preferred_element_type=F32)


def _dot_nt(a, b):
    return lax.dot_general(a, b, (((1,), (1,)), ((), ())), preferred_element_type=F32)


def _ffn_kernel(x_ref, pre_ref, wg_ref, wu_ref, wd_ref, post_ref, o_ref, u_ref, *, n_ff):
    j = pl.program_id(1)

    @pl.when(j == 0)
    def _():
        u_ref[...] = _rms(x_ref[...], pre_ref[...]).astype(BF16)
        o_ref[...] = jnp.zeros_like(o_ref)

    u = u_ref[...]
    g = _dot(u, wg_ref[...])
    up = _dot(u, wu_ref[...])
    act = (_silu(g) * up).astype(BF16)
    o_ref[...] += _dot(act, wd_ref[...])

    @pl.when(j == n_ff - 1)
    def _():
        o_ref[...] = x_ref[...] + 0.5 * _rms(o_ref[...], post_ref[...])


def _ffn(h, pre, wg, wu, wd, post, *, tm, tf):
    t, d = h.shape
    f = wg.shape[1]
    n_ff = f // tf
    vmem = (4 * tm * d * 4 + tm * d * 2 + 2 * 3 * d * tf * 2 + 4 * tm * tf * 4) // MIB + 8
    return pl.pallas_call(
        functools.partial(_ffn_kernel, n_ff=n_ff),
        grid=(t // tm, n_ff),
        in_specs=[
            pl.BlockSpec((tm, d), lambda i, j: (i, 0)),
            pl.BlockSpec((1, d), lambda i, j: (0, 0)),
            pl.BlockSpec((d, tf), lambda i, j: (0, j)),
            pl.BlockSpec((d, tf), lambda i, j: (0, j)),
            pl.BlockSpec((tf, d), lambda i, j: (j, 0)),
            pl.BlockSpec((1, d), lambda i, j: (0, 0)),
        ],
        out_specs=pl.BlockSpec((tm, d), lambda i, j: (i, 0)),
        out_shape=jax.ShapeDtypeStruct((t, d), F32),
        scratch_shapes=[pltpu.VMEM((tm, d), BF16)],
        compiler_params=_params(2, vmem),
        name="ffn",
    )(h, pre, wg, wu, wd, post)


def _in_proj_kernel(x_ref, pre_ref, w_ref, o_ref, u_ref):
    @pl.when(pl.program_id(1) == 0)
    def _():
        u_ref[...] = _rms(x_ref[...], pre_ref[...]).astype(BF16)

    o_ref[...] = _dot(u_ref[...], w_ref[...])


def _in_proj(h, pre, w, *, tm, tn):
    t, d = h.shape
    n = w.shape[1]
    vmem = (2 * tm * d * 4 + tm * d * 2 + 2 * d * tn * 2 + 3 * tm * tn * 4) // MIB + 8
    return pl.pallas_call(
        _in_proj_kernel,
        grid=(t // tm, n // tn),
        in_specs=[
            pl.BlockSpec((tm, d), lambda i, j: (i, 0)),
            pl.BlockSpec((1, d), lambda i, j: (0, 0)),
            pl.BlockSpec((d, tn), lambda i, j: (0, j)),
        ],
        out_specs=pl.BlockSpec((tm, tn), lambda i, j: (i, j)),
        out_shape=jax.ShapeDtypeStruct((t, n), F32),
        scratch_shapes=[pltpu.VMEM((tm, d), BF16)],
        compiler_params=_params(2, vmem),
        name="in_proj",
    )(h, pre, w)


def _suffix_sum_matrix():
    j = jnp.arange(LANES)[:, None]
    s = jnp.arange(LANES)[None, :]
    half = jnp.concatenate([(j > s), jnp.ones((LANES, LANES), bool)], axis=1)
    return jnp.concatenate([half, half], axis=0).astype(BF16)


def _sb_kernel(q_ref, k_ref, v_ref, w_ref, o_ref, qs_ref, ks_ref, vs_ref, acc_ref, r_ref, *, seq, tq, scale):
    qs_ref[...] = q_ref[...].astype(BF16)
    ks_ref[...] = k_ref[...].astype(BF16)
    vs_ref[...] = v_ref[...].astype(BF16)
    w = w_ref[...]
    n_sub = tq // LANES
    row = lax.broadcasted_iota(jnp.int32, (tq, tq), 0)
    col = lax.broadcasted_iota(jnp.int32, (tq, tq), 1)
    diag_mask = col < row

    def chunk(qb, kj, mask):
        k0 = pl.multiple_of(kj * tq, tq)
        kb = ks_ref[pl.ds(k0, tq), :]
        vb = vs_ref[pl.ds(k0, tq), :]
        z = _dot_nt(qb, kb) * scale
        e = jnp.log1p(jnp.exp(-jnp.abs(z)))
        mz = jnp.minimum(z, 0.0)
        log_beta = mz - e
        log_keep = (mz - z) - e
        if mask is not None:
            log_keep = jnp.where(mask, log_keep, 0.0)
        r = r_ref[...]
        a_parts = [None] * n_sub
        for c in reversed(range(n_sub)):
            sl = slice(c * LANES, (c + 1) * LANES)
            hi, lo = _split_bf16(log_keep[:, sl])
            er = _dot(jnp.concatenate([hi, lo], axis=1), w)
            a_parts[c] = jnp.exp(log_beta[:, sl] + er[:, :LANES] + r)
            r = r + er[:, LANES:]
        r_ref[...] = r
        a = jnp.concatenate(a_parts, axis=1)
        if mask is not None:
            a = jnp.where(mask, a, 0.0)
        acc_ref[...] += _dot(a.astype(BF16), vb)

    def q_body(qi, carry):
        q0 = pl.multiple_of(qi * tq, tq)
        qb = qs_ref[pl.ds(q0, tq), :]
        acc_ref[...] = jnp.zeros_like(acc_ref)
        r_ref[...] = jnp.zeros_like(r_ref)
        chunk(qb, qi, diag_mask)

        def k_body(t, c):
            chunk(qb, qi - 1 - t, None)
            return c

        lax.fori_loop(0, qi, k_body, 0)
        o_ref[pl.ds(q0, tq), :] = acc_ref[...]
        return carry

    lax.fori_loop(0, seq // tq, q_body, 0)


def _sb_attention(proj, *, tq):
    b, s, _ = proj.shape
    hd = SB_HEAD_DIM
    col0 = (COL_Q // hd, COL_K // hd, COL_V // hd)
    return pl.pallas_call(
        functools.partial(_sb_kernel, seq=s, tq=tq, scale=hd ** -0.5),
        grid=(b, SB_HEADS),
        in_specs=[
            pl.BlockSpec((None, s, hd), lambda i, h: (i, 0, col0[0] + h)),
            pl.BlockSpec((None, s, hd), lambda i, h: (i, 0, col0[1] + h)),
            pl.BlockSpec((None, s, hd), lambda i, h: (i, 0, col0[2] + h)),
            pl.BlockSpec((2 * LANES, 2 * LANES), lambda i, h: (0, 0)),
        ],
        out_specs=pl.BlockSpec((None, s, hd), lambda i, h: (i, 0, h)),
        out_shape=jax.ShapeDtypeStruct((b, s, SB_WIDTH), F32),
        scratch_shapes=[
            pltpu.VMEM((s, hd), BF16),
            pltpu.VMEM((s, hd), BF16),
            pltpu.VMEM((s, hd), BF16),
            pltpu.VMEM((tq, hd), F32),
            pltpu.VMEM((tq, LANES), F32),
        ],
        compiler_params=_params(2, 40),
        name="sb_attn",
    )(proj, proj, proj, _suffix_sum_matrix())


def _ssd_kernel(z_ref, xbc_ref, dt_ref, convw_ref, convb_ref, dtb_ref, alog_ref, dskip_ref, norm_ref,
                tri_ref, ex_ref, o_ref, xpad_ref, state_ref, *, lc):
    @pl.when(pl.program_id(1) == 0)
    def _():
        state_ref[...] = jnp.zeros_like(state_ref)
        xpad_ref[0:SUBLANES, :] = jnp.zeros((SUBLANES, CONV_DIM), F32)

    xpad_ref[SUBLANES:SUBLANES + lc, :] = xbc_ref[...]
    conv = convb_ref[...]
    for k in range(SSM_CONV):
        shift = SUBLANES - (SSM_CONV - 1) + k
        conv = conv + convw_ref[k:k + 1, :] * xpad_ref[shift:shift + lc, :]
    xpad_ref[0:SUBLANES, :] = xpad_ref[lc:lc + SUBLANES, :]
    act = _silu(conv)
    xs = act[:, :SSM_INNER]
    bm = act[:, SSM_INNER:SSM_INNER + SSM_GROUPS * SSM_STATE]
    cm = act[:, SSM_INNER + SSM_GROUPS * SSM_STATE:]

    dt = _softplus(dt_ref[...] + dtb_ref[...])
    da = dt * (-jnp.exp(alog_ref[...]))
    da_hi, da_lo = _split_bf16(da)
    cs2 = _dot(tri_ref[...], jnp.concatenate([da_hi, da_lo], axis=1))
    cs = cs2[:, :LANES] + cs2[:, LANES:]
    cs_t = cs.T

    ex = ex_ref[...]
    dt_x = _dot(jnp.concatenate(_split_bf16(dt), axis=1), ex)
    cs_x = _dot(jnp.concatenate(_split_bf16(cs), axis=1), ex)
    cs_last = cs_x[lc - 1:lc, :]
    xd = xs * dt_x
    xd_end = (xd * jnp.exp(cs_last - cs_x)).astype(BF16)
    in_decay = jnp.exp(cs_x)
    chunk_decay = jnp.exp(cs_last)

    ri = lax.broadcasted_iota(jnp.int32, (lc, lc), 0)
    ci = lax.broadcasted_iota(jnp.int32, (lc, lc), 1)
    tril = ri >= ci
    head_of_lane = lax.broadcasted_iota(jnp.int32, (1, SSM_GROUP_WIDTH), 1) // SSM_HEAD_DIM

    ys = []
    for g in range(SSM_GROUPS):
        gl = slice(g * SSM_GROUP_WIDTH, (g + 1) * SSM_GROUP_WIDTH)
        sl = slice(g * SSM_STATE, (g + 1) * SSM_STATE)
        bm_g = bm[:, sl]
        cm_g = cm[:, sl].astype(BF16)
        cb = _dot_nt(cm_g, bm_g.astype(BF16))
        xd_g = xd[:, gl]
        y_diag = jnp.zeros((lc, SSM_GROUP_WIDTH), F32)
        for r in range(SSM_HEADS_PER_GROUP):
            h = g * SSM_HEADS_PER_GROUP + r
            seg = cs[:, h:h + 1] - cs_t[h:h + 1, :]
            decay = jnp.exp(jnp.where(tril, seg, -jnp.inf))
            x_h = jnp.where(head_of_lane == r, xd_g, 0.0).astype(BF16)
            y_diag = y_diag + _dot((cb * decay).astype(BF16), x_h)
        state = state_ref[g]
        y_off = _dot(cm_g, state.astype(BF16)) * in_decay[:, gl]
        state_ref[g] = state * chunk_decay[:, gl] + _dot(bm_g.T.astype(BF16), xd_end[:, gl])
        ys.append(y_diag + y_off)
    y = jnp.concatenate(ys, axis=1) + dskip_ref[...] * xs
    y = y * _silu(z_ref[...])
    o_ref[...] = _rms(y, norm_ref[...]).astype(o_ref.dtype)


def _ssd(proj, conv_w, conv_b, dt_bias, a_log, d_skip, ssm_norm, *, lc):
    b, s, _ = proj.shape
    pad_heads = lambda v: jnp.pad(v.astype(F32), (0, LANES - SSM_HEADS)).reshape(1, LANES)
    tri = jnp.tril(jnp.ones((lc, lc), BF16))
    lane_head = jnp.arange(SSM_INNER)[None, :] // SSM_HEAD_DIM
    ex_half = (jnp.arange(LANES)[:, None] == lane_head).astype(BF16)
    ex = jnp.concatenate([ex_half, ex_half], axis=0)
    const = lambda shape: pl.BlockSpec(shape, lambda i, c: (0,) * len(shape))
    return pl.pallas_call(
        functools.partial(_ssd_kernel, lc=lc),
        grid=(b, s // lc),
        in_specs=[
            pl.BlockSpec((None, lc, SSM_INNER), lambda i, c: (i, c, COL_Z // SSM_INNER)),
            pl.BlockSpec((None, lc, CONV_DIM), lambda i, c: (i, c, COL_XBC // CONV_DIM)),
            pl.BlockSpec((None, lc, LANES), lambda i, c: (i, c, COL_DT // LANES)),
            const((SSM_CONV, CONV_DIM)),
            const((1, CONV_DIM)),
            const((1, LANES)),
            const((1, LANES)),
            const((1, SSM_INNER)),
            const((1, SSM_INNER)),
            const((lc, lc)),
            const((2 * LANES, SSM_INNER)),
        ],
        out_specs=pl.BlockSpec((None, lc, SSM_INNER), lambda i, c: (i, c, 0)),
        out_shape=jax.ShapeDtypeStruct((b, s, SSM_INNER), BF16),
        scratch_shapes=[
            pltpu.VMEM((lc + SUBLANES, CONV_DIM), F32),
            pltpu.VMEM((SSM_GROUPS, SSM_STATE, SSM_GROUP_WIDTH), F32),
        ],
        compiler_params=_params(2, 40),
        name="ssd",
    )(proj, proj, proj, conv_w.astype(F32), conv_b.reshape(1, CONV_DIM).astype(F32), pad_heads(dt_bias),
      pad_heads(a_log), jnp.repeat(d_skip.astype(F32), SSM_HEAD_DIM).reshape(1, SSM_INNER),
      ssm_norm.reshape(1, SSM_INNER).astype(F32), tri, ex)


def _mem_kv_kernel(m_ref, norm_ref, w_ref, o_ref):
    o_ref[...] = _dot(_rms(m_ref[...], norm_ref[...]).astype(BF16), w_ref[...]).astype(o_ref.dtype)


def _mem_kv(mem, kv_norm, w_mkv):
    b, n, d = mem.shape
    return pl.pallas_call(
        _mem_kv_kernel,
        grid=(b,),
        in_specs=[
            pl.BlockSpec((None, n, d), lambda i: (i, 0, 0)),
            pl.BlockSpec((1, d), lambda i: (0, 0)),
            pl.BlockSpec((d, 2 * MEM_WIDTH), lambda i: (0, 0)),
        ],
        out_specs=pl.BlockSpec((None, n, 2 * MEM_WIDTH), lambda i: (i, 0, 0)),
        out_shape=jax.ShapeDtypeStruct((b, n, 2 * MEM_WIDTH), BF16),
        compiler_params=_params(1, 32),
        name="mem_kv",
    )(mem, kv_norm, w_mkv)


def _mix_mem_kernel(sb_ref, y_ref, h_ref, kv_ref, sbn_ref, woa_ref, wob_ref, mixpost_ref, mempre_ref, wmq_ref,
                    wmo_ref, mempost_ref, o_ref, *, scale):
    sbn = _rms(sb_ref[...], sbn_ref[...]).astype(BF16)
    mix = _dot(sbn, woa_ref[...]) + _dot(y_ref[...], wob_ref[...])
    h2 = h_ref[...] + _rms(mix, mixpost_ref[...])
    q = _dot(_rms(h2, mempre_ref[...]).astype(BF16), wmq_ref[...])
    heads = []
    for hd in range(MEM_HEADS):
        sl = slice(hd * MEM_HEAD_DIM, (hd + 1) * MEM_HEAD_DIM)
        k_h = kv_ref[:, sl]
        v_h = kv_ref[:, MEM_WIDTH + hd * MEM_HEAD_DIM:MEM_WIDTH + (hd + 1) * MEM_HEAD_DIM]
        sc = _dot_nt(q[:, sl].astype(BF16), k_h) * scale
        e = jnp.exp(sc - jnp.max(sc, axis=-1, keepdims=True))
        p = e / jnp.sum(e, axis=-1, keepdims=True)
        heads.append(_dot(p.astype(BF16), v_h))
    o = jnp.concatenate(heads, axis=1).astype(BF16)
    o_ref[...] = h2 + _rms(_dot(o, wmo_ref[...]), mempost_ref[...])


def _mix_mem(sb, y, h, kv, sb_norm, wo_sb, wo_y, mix_post, mem_pre, w_mq, w_mo, mem_post, *, tm, seq):
    t, d = h.shape
    n_mem = kv.shape[1]
    steps_per_batch = seq // tm
    const = lambda shape: pl.BlockSpec(shape, lambda i: (0,) * len(shape))
    return pl.pallas_call(
        functools.partial(_mix_mem_kernel, scale=MEM_HEAD_DIM ** -0.5),
        grid=(t // tm,),
        in_specs=[
            pl.BlockSpec((tm, SB_WIDTH), lambda i: (i, 0)),
            pl.BlockSpec((tm, SSM_INNER), lambda i: (i, 0)),
            pl.BlockSpec((tm, d), lambda i: (i, 0)),
            pl.BlockSpec((None, n_mem, 2 * MEM_WIDTH), lambda i: (i // steps_per_batch, 0, 0)),
            const((1, SB_WIDTH)),
            const((SB_WIDTH, d)),
            const((SSM_INNER, d)),
            const((1, d)),
            const((1, d)),
            const((d, MEM_WIDTH)),
            const((MEM_WIDTH, d)),
            const((1, d)),
        ],
        out_specs=pl.BlockSpec((tm, d), lambda i: (i, 0)),
        out_shape=jax.ShapeDtypeStruct((t, d), F32),
        compiler_params=_params(1, 52),
        name="mix_mem",
    )(sb, y, h, kv, sb_norm, wo_sb, wo_y, mix_post, mem_pre, w_mq, w_mo, mem_post)


FFN_TM = 512
FFN_TF = 512
PROJ_TM = 1024
PROJ_TN = 896
SB_TQ = 512
SSD_CHUNK = 128
MIX_TM = 256


def _row(v):
    return v.reshape(1, -1).astype(F32)


def _pad_cols(w, n):
    return jnp.pad(w, ((0, 0), (0, n - w.shape[1])))


def _layer(h, mem, p):
    b, s, d = h.shape
    t = b * s
    f = p["ffn1_wg"].shape[1]
    f_pad = -(-f // FFN_TF) * FFN_TF

    def ffn(hh, pre, wg, wu, wd, post):
        wg = _pad_cols(wg.astype(BF16), f_pad)
        wu = _pad_cols(wu.astype(BF16), f_pad)
        wd = jnp.pad(wd.astype(BF16), ((0, f_pad - f), (0, 0)))
        return _ffn(hh, _row(pre), wg, wu, wd, _row(post), tm=min(FFN_TM, t), tf=FFN_TF)

    h1 = ffn(h.reshape(t, d), p["ffn1_pre"], p["ffn1_wg"], p["ffn1_wu"], p["ffn1_wd"], p["ffn1_post"])

    w_in = _pad_cols(p["w_in"].astype(BF16), PROJ_WIDTH)
    proj = _in_proj(h1, _row(p["mix_pre"]), w_in, tm=min(PROJ_TM, t), tn=PROJ_TN).reshape(b, s, PROJ_WIDTH)
    sb = _sb_attention(proj, tq=min(SB_TQ, s))
    y = _ssd(proj, p["conv_w"], p["conv_b"], p["dt_bias"], p["a_log"], p["d_skip"], p["ssm_norm"],
             lc=min(SSD_CHUNK, s))

    kv = _mem_kv(mem, _row(p["mem_kv_norm"]), p["w_mkv"].astype(BF16))
    w_out = p["w_out"].astype(BF16)
    h3 = _mix_mem(sb.reshape(t, SB_WIDTH), y.reshape(t, SSM_INNER), h1, kv, _row(p["sb_norm"]),
                  w_out[:SB_WIDTH], w_out[SB_WIDTH:], _row(p["mix_post"]), _row(p["mem_pre"]),
                  p["w_mq"].astype(BF16), p["w_mo"].astype(BF16), _row(p["mem_post"]), tm=min(MIX_TM, s), seq=s)

    out = ffn(h3, p["ffn2_pre"], p["ffn2_wg"], p["ffn2_wu"], p["ffn2_wd"], p["ffn2_post"])
    return out.reshape(b, s, d)


def kernel(x, mem, ffn1_pre, ffn1_wg, ffn1_wu, ffn1_wd, ffn1_post, mix_pre, w_in, conv_w, conv_b, dt_bias, a_log,
           d_skip, sb_norm, ssm_norm, w_out, mix_post, mem_pre, mem_kv_norm, w_mq, w_mkv, w_mo, mem_post, ffn2_pre,
           ffn2_wg, ffn2_wu, ffn2_wd, ffn2_post):
    params = dict(
        ffn1_pre=ffn1_pre, ffn1_wg=ffn1_wg, ffn1_wu=ffn1_wu, ffn1_wd=ffn1_wd, ffn1_post=ffn1_post,
        mix_pre=mix_pre, w_in=w_in, conv_w=conv_w, conv_b=conv_b, dt_bias=dt_bias, a_log=a_log, d_skip=d_skip,
        sb_norm=sb_norm, ssm_norm=ssm_norm, w_out=w_out, mix_post=mix_post, mem_pre=mem_pre,
        mem_kv_norm=mem_kv_norm, w_mq=w_mq, w_mkv=w_mkv, w_mo=w_mo, mem_post=mem_post,
        ffn2_pre=ffn2_pre, ffn2_wg=ffn2_wg, ffn2_wu=ffn2_wu, ffn2_wd=ffn2_wd, ffn2_post=ffn2_post)
    h = x
    for layer in range(ffn1_pre.shape[0]):
        h = _layer(h, mem, {k: v[layer] for k, v in params.items()})
    return h
```

```python
import functools

import jax
import jax.numpy as jnp
from jax import lax
from jax.experimental import pallas as pl
from jax.experimental.pallas import tpu as pltpu

F32 = jnp.float32
BF16 = jnp.bfloat16

EPS = 1e-6
LOG2E = 1.4426950408889634
LANES = 128
SUBLANES = 8
MXU_DIM = 256
VMEM_BYTES_V7X = 64 * 1024 * 1024
MIB = 1024 * 1024

SB_HEADS = 8
SB_HEAD_DIM = 128
SB_WIDTH = SB_HEADS * SB_HEAD_DIM
SSM_HEADS = 16
SSM_HEAD_DIM = 64
SSM_INNER = SSM_HEADS * SSM_HEAD_DIM
SSM_GROUPS = 4
SSM_HEADS_PER_GROUP = SSM_HEADS // SSM_GROUPS
SSM_STATE = 128
SSM_CONV = 4
SSM_GROUP_WIDTH = SSM_HEADS_PER_GROUP * SSM_HEAD_DIM
CONV_DIM = SSM_INNER + 2 * SSM_GROUPS * SSM_STATE
MEM_HEADS = 4
MEM_HEAD_DIM = 128
MEM_WIDTH = MEM_HEADS * MEM_HEAD_DIM

COL_Q = 0
COL_K = SB_WIDTH
COL_V = 2 * SB_WIDTH
COL_Z = 3 * SB_WIDTH
COL_XBC = COL_Z + SSM_INNER
COL_DT = COL_XBC + CONV_DIM
PROJ_WIDTH = COL_DT + MXU_DIM


def _params(n_grid_axes, vmem_mib):
    return pltpu.CompilerParams(
        dimension_semantics=("arbitrary",) * n_grid_axes,
        vmem_limit_bytes=min(vmem_mib * MIB, VMEM_BYTES_V7X - 4 * MIB),
    )


def _rms(x, w):
    ms = jnp.mean(x * x, axis=-1, keepdims=True)
    return x * lax.rsqrt(ms + EPS) * w


def _silu(x):
    return x * jax.nn.sigmoid(x)


def _softplus(x):
    return jnp.maximum(x, 0.0) + jnp.log1p(jnp.exp(-jnp.abs(x)))


def _split_bf16(x):
    hi = x.astype(BF16)
    lo = (x - hi.astype(F32)).astype(BF16)
    return hi, lo


def _dot(a, b):
    return jnp.dot(a, b, preferred_element_type=F32)


def _dot_nt(a, b):
    return lax.dot_general(a, b, (((1,), (1,)), ((), ())), preferred_element_type=F32)


def _ffn_kernel(x_ref, pre_ref, wg_ref, wu_ref, wd_ref, post_ref, o_ref, u_ref, *, n_ff, n_chunks, mc):
    j = pl.program_id(1)

    def step(first, last):
        for m in range(n_chunks):
            rows = slice(m * mc, (m + 1) * mc)
            if first:
                u = _rms(x_ref[rows, :], pre_ref[...]).astype(BF16)
                u_ref[rows, :] = u
            else:
                u = u_ref[rows, :]
            act = (_silu(_dot(u, wg_ref[...])) * _dot(u, wu_ref[...])).astype(BF16)
            y = _dot(act, wd_ref[...])
            if not first:
                y = o_ref[rows, :] + y
            if last:
                y = x_ref[rows, :] + 0.5 * _rms(y, post_ref[...])
            o_ref[rows, :] = y

    pl.when(j == 0)(functools.partial(step, True, False))
    pl.when(jnp.logical_and(j > 0, j < n_ff - 1))(functools.partial(step, False, False))
    pl.when(j == n_ff - 1)(functools.partial(step, False, True))


def _ffn(h, pre, wg, wu, wd, post, *, tm, tf, mc):
    t, d = h.shape
    f = wg.shape[1]
    n_ff = f // tf
    assert n_ff >= 2 and tm % mc == 0
    vmem = (4 * tm * d * 4 + tm * d * 2 + 2 * 3 * d * tf * 2 + 4 * mc * tf * 4 + 2 * mc * d * 4) // MIB + 6
    return pl.pallas_call(
        functools.partial(_ffn_kernel, n_ff=n_ff, n_chunks=tm // mc, mc=mc),
        grid=(t // tm, n_ff),
        in_specs=[
            pl.BlockSpec((tm, d), lambda i, j: (i, 0)),
            pl.BlockSpec((1, d), lambda i, j: (0, 0)),
            pl.BlockSpec((d, tf), lambda i, j: (0, j)),
            pl.BlockSpec((d, tf), lambda i, j: (0, j)),
            pl.BlockSpec((tf, d), lambda i, j: (j, 0)),
            pl.BlockSpec((1, d), lambda i, j: (0, 0)),
        ],
        out_specs=pl.BlockSpec((tm, d), lambda i, j: (i, 0)),
        out_shape=jax.ShapeDtypeStruct((t, d), F32),
        scratch_shapes=[pltpu.VMEM((tm, d), BF16)],
        compiler_params=_params(2, vmem),
        name="ffn",
    )(h, pre, wg, wu, wd, post)


def _in_proj_kernel(x_ref, pre_ref, w_ref, o_ref, u_ref):
    @pl.when(pl.program_id(1) == 0)
    def _():
        u_ref[...] = _rms(x_ref[...], pre_ref[...]).astype(BF16)

    o_ref[...] = _dot(u_ref[...], w_ref[...])


def _in_proj(h, pre, w, *, tm, tn):
    t, d = h.shape
    n = w.shape[1]
    vmem = (2 * tm * d * 4 + tm * d * 2 + 2 * d * tn * 2 + 3 * tm * tn * 4) // MIB + 8
    return pl.pallas_call(
        _in_proj_kernel,
        grid=(t // tm, n // tn),
        in_specs=[
            pl.BlockSpec((tm, d), lambda i, j: (i, 0)),
            pl.BlockSpec((1, d), lambda i, j: (0, 0)),
            pl.BlockSpec((d, tn), lambda i, j: (0, j)),
        ],
        out_specs=pl.BlockSpec((tm, tn), lambda i, j: (i, j)),
        out_shape=jax.ShapeDtypeStruct((t, n), F32),
        scratch_shapes=[pltpu.VMEM((tm, d), BF16)],
        compiler_params=_params(2, vmem),
        name="in_proj",
    )(h, pre, w)


def _suffix_sum_matrix():
    j = jnp.arange(LANES)[:, None]
    s = jnp.arange(LANES)[None, :]
    half = jnp.concatenate([(j > s), jnp.ones((LANES, LANES), bool)], axis=1)
    return jnp.concatenate([half, half], axis=0).astype(BF16)


def _sb_kernel(q_ref, k_ref, v_ref, w_ref, o_ref, qs_ref, ks_ref, vs_ref, acc_ref, r_ref, *, seq, tq, nh, scale):
    qs_ref[...] = q_ref[...].astype(BF16)
    ks_ref[...] = k_ref[...].astype(BF16)
    vs_ref[...] = v_ref[...].astype(BF16)
    w = w_ref[...]
    n_sub = tq // LANES
    heads = [slice(h * SB_HEAD_DIM, (h + 1) * SB_HEAD_DIM) for h in range(nh)]
    row = lax.broadcasted_iota(jnp.int32, (tq, tq), 0)
    col = lax.broadcasted_iota(jnp.int32, (tq, tq), 1)
    diag_mask = col < row

    def chunk(q0, kj, mask):
        k0 = pl.multiple_of(kj * tq, tq)
        d = [_dot_nt(qs_ref[pl.ds(q0, tq), hs], ks_ref[pl.ds(k0, tq), hs]) for hs in heads]
        log_beta, log_keep = [], []
        for dh in d:
            z = dh * scale
            e = jnp.log(1.0 + jnp.exp2(jnp.abs(dh) * (-scale * LOG2E)))
            lb = jnp.minimum(z, 0.0) - e
            lk = lb - z
            log_beta.append(lb)
            log_keep.append(lk if mask is None else jnp.where(mask, lk, 0.0))
        r = [r_ref[:, hs] for hs in heads]
        a_parts = [[None] * n_sub for _ in heads]
        for c in reversed(range(n_sub)):
            sl = slice(c * LANES, (c + 1) * LANES)
            for h in range(nh):
                hi, lo = _split_bf16(log_keep[h][:, sl])
                er = _dot(jnp.concatenate([hi, lo], axis=1), w)
                a_parts[h][c] = jnp.exp(log_beta[h][:, sl] + er[:, :LANES] + r[h])
                r[h] = r[h] + er[:, LANES:]
        for h, hs in enumerate(heads):
            r_ref[:, hs] = r[h]
            a = jnp.concatenate(a_parts[h], axis=1)
            if mask is not None:
                a = jnp.where(mask, a, 0.0)
            acc_ref[:, hs] += _dot(a.astype(BF16), vs_ref[pl.ds(k0, tq), hs])

    def q_body(qi, carry):
        q0 = pl.multiple_of(qi * tq, tq)
        acc_ref[...] = jnp.zeros_like(acc_ref)
        r_ref[...] = jnp.zeros_like(r_ref)
        chunk(q0, qi, diag_mask)

        def k_body(t, c):
            chunk(q0, qi - 1 - t, None)
            return c

        lax.fori_loop(0, qi, k_body, 0)
        o_ref[pl.ds(q0, tq), :] = acc_ref[...]
        return carry

    lax.fori_loop(0, seq // tq, q_body, 0)


def _sb_attention(proj, *, tq, nh):
    b, s, _ = proj.shape
    wd = nh * SB_HEAD_DIM
    col0 = (COL_Q // wd, COL_K // wd, COL_V // wd)
    return pl.pallas_call(
        functools.partial(_sb_kernel, seq=s, tq=tq, nh=nh, scale=SB_HEAD_DIM ** -0.5),
        grid=(b, SB_HEADS // nh),
        in_specs=[
            pl.BlockSpec((None, s, wd), lambda i, h: (i, 0, col0[0] + h)),
            pl.BlockSpec((None, s, wd), lambda i, h: (i, 0, col0[1] + h)),
            pl.BlockSpec((None, s, wd), lambda i, h: (i, 0, col0[2] + h)),
            pl.BlockSpec((2 * LANES, 2 * LANES), lambda i, h: (0, 0)),
        ],
        out_specs=pl.BlockSpec((None, s, wd), lambda i, h: (i, 0, h)),
        out_shape=jax.ShapeDtypeStruct((b, s, SB_WIDTH), F32),
        scratch_shapes=[
            pltpu.VMEM((s, wd), BF16),
            pltpu.VMEM((s, wd), BF16),
            pltpu.VMEM((s, wd), BF16),
            pltpu.VMEM((tq, wd), F32),
            pltpu.VMEM((tq, nh * LANES), F32),
        ],
        compiler_params=_params(2, 40),
        name="sb_attn",
    )(proj, proj, proj, _suffix_sum_matrix())


def _ssd_kernel(z_ref, xbc_ref, dt_ref, convw_ref, convb_ref, dtb_ref, alog_ref, dskip_ref, norm_ref,
                tri_ref, ex_ref, o_ref, xpad_ref, state_ref, *, lc):
    @pl.when(pl.program_id(1) == 0)
    def _():
        state_ref[...] = jnp.zeros_like(state_ref)
        xpad_ref[0:SUBLANES, :] = jnp.zeros((SUBLANES, CONV_DIM), F32)

    xpad_ref[SUBLANES:SUBLANES + lc, :] = xbc_ref[...]
    conv = convb_ref[...]
    for k in range(SSM_CONV):
        shift = SUBLANES - (SSM_CONV - 1) + k
        conv = conv + convw_ref[k:k + 1, :] * xpad_ref[shift:shift + lc, :]
    xpad_ref[0:SUBLANES, :] = xpad_ref[lc:lc + SUBLANES, :]
    act = _silu(conv)
    xs = act[:, :SSM_INNER]
    bm = act[:, SSM_INNER:SSM_INNER + SSM_GROUPS * SSM_STATE]
    cm = act[:, SSM_INNER + SSM_GROUPS * SSM_STATE:]

    dt = _softplus(dt_ref[...] + dtb_ref[...])
    da = dt * (-jnp.exp(alog_ref[...]))
    da_hi, da_lo = _split_bf16(da)
    cs2 = _dot(tri_ref[...], jnp.concatenate([da_hi, da_lo], axis=1))
    cs = cs2[:, :LANES] + cs2[:, LANES:]
    cs_t = cs.T

    ex = ex_ref[...]
    dt_x = _dot(jnp.concatenate(_split_bf16(dt), axis=1), ex)
    cs_x = _dot(jnp.concatenate(_split_bf16(cs), axis=1), ex)
    cs_last = cs_x[lc - 1:lc, :]
    xd = xs * dt_x
    xd_end = (xd * jnp.exp(cs_last - cs_x)).astype(BF16)
    in_decay = jnp.exp(cs_x)
    chunk_decay = jnp.exp(cs_last)

    ri = lax.broadcasted_iota(jnp.int32, (lc, lc), 0)
    ci = lax.broadcasted_iota(jnp.int32, (lc, lc), 1)
    tril = ri >= ci
    head_of_lane = lax.broadcasted_iota(jnp.int32, (1, SSM_GROUP_WIDTH), 1) // SSM_HEAD_DIM

    ys = []
    for g in range(SSM_GROUPS):
        gl = slice(g * SSM_GROUP_WIDTH, (g + 1) * SSM_GROUP_WIDTH)
        sl = slice(g * SSM_STATE, (g + 1) * SSM_STATE)
        bm_g = bm[:, sl]
        cm_g = cm[:, sl].astype(BF16)
        cb = _dot_nt(cm_g, bm_g.astype(BF16))
        xd_g = xd[:, gl]
        y_diag = jnp.zeros((lc, SSM_GROUP_WIDTH), F32)
        for r in range(SSM_HEADS_PER_GROUP):
            h = g * SSM_HEADS_PER_GROUP + r
            seg = cs[:, h:h + 1] - cs_t[h:h + 1, :]
            decay = jnp.exp(jnp.where(tril, seg, -jnp.inf))
            x_h = jnp.where(head_of_lane == r, xd_g, 0.0).astype(BF16)
            y_diag = y_diag + _dot((cb * decay).astype(BF16), x_h)
        state = state_ref[g]
        y_off = _dot(cm_g, state.astype(BF16)) * in_decay[:, gl]
        state_ref[g] = state * chunk_decay[:, gl] + _dot(bm_g.T.astype(BF16), xd_end[:, gl])
        ys.append(y_diag + y_off)
    y = jnp.concatenate(ys, axis=1) + dskip_ref[...] * xs
    y = y * _silu(z_ref[...])
    o_ref[...] = _rms(y, norm_ref[...]).astype(o_ref.dtype)


def _ssd(proj, conv_w, conv_b, dt_bias, a_log, d_skip, ssm_norm, *, lc):
    b, s, _ = proj.shape
    pad_heads = lambda v: jnp.pad(v.astype(F32), (0, LANES - SSM_HEADS)).reshape(1, LANES)
    tri = jnp.tril(jnp.ones((lc, lc), BF16))
    lane_head = jnp.arange(SSM_INNER)[None, :] // SSM_HEAD_DIM
    ex_half = (jnp.arange(LANES)[:, None] == lane_head).astype(BF16)
    ex = jnp.concatenate([ex_half, ex_half], axis=0)
    const = lambda shape: pl.BlockSpec(shape, lambda i, c: (0,) * len(shape))
    return pl.pallas_call(
        functools.partial(_ssd_kernel, lc=lc),
        grid=(b, s // lc),
        in_specs=[
            pl.BlockSpec((None, lc, SSM_INNER), lambda i, c: (i, c, COL_Z // SSM_INNER)),
            pl.BlockSpec((None, lc, CONV_DIM), lambda i, c: (i, c, COL_XBC // CONV_DIM)),
            pl.BlockSpec((None, lc, LANES), lambda i, c: (i, c, COL_DT // LANES)),
            const((SSM_CONV, CONV_DIM)),
            const((1, CONV_DIM)),
            const((1, LANES)),
            const((1, LANES)),
            const((1, SSM_INNER)),
            const((1, SSM_INNER)),
            const((lc, lc)),
            const((2 * LANES, SSM_INNER)),
        ],
        out_specs=pl.BlockSpec((None, lc, SSM_INNER), lambda i, c: (i, c, 0)),
        out_shape=jax.ShapeDtypeStruct((b, s, SSM_INNER), BF16),
        scratch_shapes=[
            pltpu.VMEM((lc + SUBLANES, CONV_DIM), F32),
            pltpu.VMEM((SSM_GROUPS, SSM_STATE, SSM_GROUP_WIDTH), F32),
        ],
        compiler_params=_params(2, 40),
        name="ssd",
    )(proj, proj, proj, conv_w.astype(F32), conv_b.reshape(1, CONV_DIM).astype(F32), pad_heads(dt_bias),
      pad_heads(a_log), jnp.repeat(d_skip.astype(F32), SSM_HEAD_DIM).reshape(1, SSM_INNER),
      ssm_norm.reshape(1, SSM_INNER).astype(F32), tri, ex)


def _mem_kv_kernel(m_ref, norm_ref, w_ref, o_ref):
    o_ref[...] = _dot(_rms(m_ref[...], norm_ref[...]).astype(BF16), w_ref[...]).astype(o_ref.dtype)


def _mem_kv(mem, kv_norm, w_mkv):
    b, n, d = mem.shape
    return pl.pallas_call(
        _mem_kv_kernel,
        grid=(b,),
        in_specs=[
            pl.BlockSpec((None, n, d), lambda i: (i, 0, 0)),
            pl.BlockSpec((1, d), lambda i: (0, 0)),
            pl.BlockSpec((d, 2 * MEM_WIDTH), lambda i: (0, 0)),
        ],
        out_specs=pl.BlockSpec((None, n, 2 * MEM_WIDTH), lambda i: (i, 0, 0)),
        out_shape=jax.ShapeDtypeStruct((b, n, 2 * MEM_WIDTH), BF16),
        compiler_params=_params(1, 32),
        name="mem_kv",
    )(mem, kv_norm, w_mkv)


def _mix_mem_kernel(sb_ref, y_ref, h_ref, kv_ref, sbn_ref, woa_ref, wob_ref, mixpost_ref, mempre_ref, wmq_ref,
                    wmo_ref, mempost_ref, o_ref, *, scale, n_chunks, mc):
    chunks = [slice(m * mc, (m + 1) * mc) for m in range(n_chunks)]
    sbn = [_rms(sb_ref[rows, :], sbn_ref[...]).astype(BF16) for rows in chunks]
    mix = [_dot(sbn[m], woa_ref[...]) + _dot(y_ref[rows, :], wob_ref[...]) for m, rows in enumerate(chunks)]
    h2 = [h_ref[rows, :] + _rms(mix[m], mixpost_ref[...]) for m, rows in enumerate(chunks)]
    u = [_rms(h2[m], mempre_ref[...]).astype(BF16) for m in range(n_chunks)]
    q = [_dot(u[m], wmq_ref[...]) for m in range(n_chunks)]
    att = []
    for m in range(n_chunks):
        heads = []
        for hd in range(MEM_HEADS):
            sl = slice(hd * MEM_HEAD_DIM, (hd + 1) * MEM_HEAD_DIM)
            k_h = kv_ref[:, sl]
            v_h = kv_ref[:, MEM_WIDTH + hd * MEM_HEAD_DIM:MEM_WIDTH + (hd + 1) * MEM_HEAD_DIM]
            sc = _dot_nt(q[m][:, sl].astype(BF16), k_h) * scale
            e = jnp.exp(sc - jnp.max(sc, axis=-1, keepdims=True))
            p = e / jnp.sum(e, axis=-1, keepdims=True)
            heads.append(_dot(p.astype(BF16), v_h))
        att.append(jnp.concatenate(heads, axis=1).astype(BF16))
    mo = [_dot(att[m], wmo_ref[...]) for m in range(n_chunks)]
    for m, rows in enumerate(chunks):
        o_ref[rows, :] = h2[m] + _rms(mo[m], mempost_ref[...])


def _mix_mem(sb, y, h, kv, sb_norm, wo_sb, wo_y, mix_post, mem_pre, w_mq, w_mo, mem_post, *, tm, mc, seq):
    t, d = h.shape
    n_mem = kv.shape[1]
    steps_per_batch = seq // tm
    const = lambda shape: pl.BlockSpec(shape, lambda i: (0,) * len(shape), pipeline_mode=pl.Buffered(1))
    return pl.pallas_call(
        functools.partial(_mix_mem_kernel, scale=MEM_HEAD_DIM ** -0.5, n_chunks=tm // mc, mc=mc),
        grid=(t // tm,),
        in_specs=[
            pl.BlockSpec((tm, SB_WIDTH), lambda i: (i, 0)),
            pl.BlockSpec((tm, SSM_INNER), lambda i: (i, 0)),
            pl.BlockSpec((tm, d), lambda i: (i, 0)),
            pl.BlockSpec((None, n_mem, 2 * MEM_WIDTH), lambda i: (i // steps_per_batch, 0, 0)),
            const((1, SB_WIDTH)),
            const((SB_WIDTH, d)),
            const((SSM_INNER, d)),
            const((1, d)),
            const((1, d)),
            const((d, MEM_WIDTH)),
            const((MEM_WIDTH, d)),
            const((1, d)),
        ],
        out_specs=pl.BlockSpec((tm, d), lambda i: (i, 0)),
        out_shape=jax.ShapeDtypeStruct((t, d), F32),
        compiler_params=_params(1, 52),
        name="mix_mem",
    )(sb, y, h, kv, sb_norm, wo_sb, wo_y, mix_post, mem_pre, w_mq, w_mo, mem_post)


FFN_TM = 1024
FFN_TF = 512
FFN_MC = MXU_DIM
PROJ_TM = 1024
PROJ_TN = 5 * MXU_DIM
SB_TQ = 512
SB_NH = 2
SSD_CHUNK = 128
MIX_TM = 512
MIX_MC = MXU_DIM


def _row(v):
    return v.reshape(1, -1).astype(F32)


def _cast_pad(w, rows, cols):
    return jnp.pad(w, ((0, rows - w.shape[0]), (0, cols - w.shape[1]))).astype(BF16)


def _layer(h, mem, p):
    b, s, d = h.shape
    t = b * s
    f = p["ffn1_wg"].shape[1]
    f_pad = -(-f // FFN_TF) * FFN_TF

    def ffn(hh, pre, wg, wu, wd, post):
        tm = min(FFN_TM, t)
        return _ffn(hh, _row(pre), _cast_pad(wg, d, f_pad), _cast_pad(wu, d, f_pad), _cast_pad(wd, f_pad, d),
                    _row(post), tm=tm, tf=FFN_TF, mc=min(FFN_MC, tm))

    h1 = ffn(h.reshape(t, d), p["ffn1_pre"], p["ffn1_wg"], p["ffn1_wu"], p["ffn1_wd"], p["ffn1_post"])

    w_in = _cast_pad(p["w_in"], d, PROJ_WIDTH)
    proj = _in_proj(h1, _row(p["mix_pre"]), w_in, tm=min(PROJ_TM, t), tn=PROJ_TN).reshape(b, s, PROJ_WIDTH)
    sb = _sb_attention(proj, tq=min(SB_TQ, s), nh=SB_NH)
    y = _ssd(proj, p["conv_w"], p["conv_b"], p["dt_bias"], p["a_log"], p["d_skip"], p["ssm_norm"],
             lc=min(SSD_CHUNK, s))

    kv = _mem_kv(mem, _row(p["mem_kv_norm"]), p["w_mkv"].astype(BF16))
    w_out = p["w_out"].astype(BF16)
    h3 = _mix_mem(sb.reshape(t, SB_WIDTH), y.reshape(t, SSM_INNER), h1, kv, _row(p["sb_norm"]),
                  w_out[:SB_WIDTH], w_out[SB_WIDTH:], _row(p["mix_post"]), _row(p["mem_pre"]),
                  p["w_mq"].astype(BF16), p["w_mo"].astype(BF16), _row(p["mem_post"]), tm=min(MIX_TM, s),
                  mc=min(MIX_MC, s), seq=s)

    out = ffn(h3, p["ffn2_pre"], p["ffn2_wg"], p["ffn2_wu"], p["ffn2_wd"], p["ffn2_post"])
    return out.reshape(b, s, d)


def kernel(x, mem, ffn1_pre, ffn1_wg, ffn1_wu, ffn1_wd, ffn1_post, mix_pre, w_in, conv_w, conv_b, dt_bias, a_log,
           d_skip, sb_norm, ssm_norm, w_out, mix_post, mem_pre, mem_kv_norm, w_mq, w_mkv, w_mo, mem_post, ffn2_pre,
           ffn2_wg, ffn2_wu, ffn2_wd, ffn2_post):
    params = dict(
        ffn1_pre=ffn1_pre, ffn1_wg=ffn1_wg, ffn1_wu=ffn1_wu, ffn1_wd=ffn1_wd, ffn1_post=ffn1_post,
        mix_pre=mix_pre, w_in=w_in, conv_w=conv_w, conv_b=conv_b, dt_bias=dt_bias, a_log=a_log, d_skip=d_skip,
        sb_norm=sb_norm, ssm_norm=ssm_norm, w_out=w_out, mix_post=mix_post, mem_pre=mem_pre,
        mem_kv_norm=mem_kv_norm, w_mq=w_mq, w_mkv=w_mkv, w_mo=w_mo, mem_post=mem_post,
        ffn2_pre=ffn2_pre, ffn2_wg=ffn2_wg, ffn2_wu=ffn2_wu, ffn2_wd=ffn2_wd, ffn2_post=ffn2_post)
    h = x
    for layer in range(ffn1_pre.shape[0]):
        h = _layer(h, mem, {k: v[layer] for k, v in params.items()})
    return h
```

```python
import functools

import jax
import jax.numpy as jnp
from jax import lax
from jax.experimental import pallas as pl
from jax.experimental.pallas import tpu as pltpu

F32 = jnp.float32
BF16 = jnp.bfloat16

EPS = 1e-6
LOG2E = 1.4426950408889634
LANES = 128
SUBLANES = 8
MXU_DIM = 256
VMEM_BYTES_V7X = 64 * 1024 * 1024
MIB = 1024 * 1024

SB_HEADS = 8
SB_HEAD_DIM = 128
SB_WIDTH = SB_HEADS * SB_HEAD_DIM
SSM_HEADS = 16
SSM_HEAD_DIM = 64
SSM_INNER = SSM_HEADS * SSM_HEAD_DIM
SSM_GROUPS = 4
SSM_HEADS_PER_GROUP = SSM_HEADS // SSM_GROUPS
SSM_STATE = 128
SSM_CONV = 4
SSM_GROUP_WIDTH = SSM_HEADS_PER_GROUP * SSM_HEAD_DIM
CONV_DIM = SSM_INNER + 2 * SSM_GROUPS * SSM_STATE
MEM_HEADS = 4
MEM_HEAD_DIM = 128
MEM_WIDTH = MEM_HEADS * MEM_HEAD_DIM

COL_Q = 0
COL_K = SB_WIDTH
COL_V = 2 * SB_WIDTH
COL_Z = 3 * SB_WIDTH
COL_XBC = COL_Z + SSM_INNER
PROJ_WIDTH = COL_XBC + CONV_DIM


def _params(n_grid_axes, vmem_mib):
    return pltpu.CompilerParams(
        dimension_semantics=("arbitrary",) * n_grid_axes,
        vmem_limit_bytes=min(vmem_mib * MIB, VMEM_BYTES_V7X - 4 * MIB),
    )


def _rms(x, w):
    ms = jnp.mean(x * x, axis=-1, keepdims=True)
    return x * lax.rsqrt(ms + EPS) * w


def _silu(x):
    return x * jax.nn.sigmoid(x)


def _softplus(x):
    return jnp.maximum(x, 0.0) + jnp.log1p(jnp.exp(-jnp.abs(x)))


def _split_bf16(x):
    hi = x.astype(BF16)
    lo = (x - hi.astype(F32)).astype(BF16)
    return hi, lo


def _dot(a, b):
    return jnp.dot(a, b, preferred_element_type=F32)


def _dot_nt(a, b):
    return lax.dot_general(a, b, (((1,), (1,)), ((), ())), preferred_element_type=F32)


def _ffn_kernel(x_ref, pre_ref, wg_ref, wu_ref, wd_ref, *rest, n_main, n_tail, n_chunks, mc):
    wg_tails, wu_tails, wd_tails = rest[:n_tail], rest[n_tail:2 * n_tail], rest[2 * n_tail:3 * n_tail]
    post_ref, o_ref, u_ref, wgt_ref, wut_ref, wdt_ref = rest[3 * n_tail:]
    j = pl.program_id(1)

    def step(first, last):
        if last:
            for k in range(n_tail):
                cols = slice(k * LANES, (k + 1) * LANES)
                wgt_ref[:, cols] = wg_tails[k][...]
                wut_ref[:, cols] = wu_tails[k][...]
                wdt_ref[cols, :] = wd_tails[k][...]
        wg, wu, wd = (wgt_ref, wut_ref, wdt_ref) if last else (wg_ref, wu_ref, wd_ref)
        for m in range(n_chunks):
            rows = slice(m * mc, (m + 1) * mc)
            if first:
                u = _rms(x_ref[rows, :], pre_ref[...]).astype(BF16)
                u_ref[rows, :] = u
            else:
                u = u_ref[rows, :]
            act = (_silu(_dot(u, wg[...])) * _dot(u, wu[...])).astype(BF16)
            y = _dot(act, wd[...])
            if not first:
                y = o_ref[rows, :] + y
            if last:
                y = x_ref[rows, :] + 0.5 * _rms(y, post_ref[...])
            o_ref[rows, :] = y

    pl.when(j == 0)(functools.partial(step, True, False))
    pl.when(jnp.logical_and(j > 0, j < n_main))(functools.partial(step, False, False))
    pl.when(j == n_main)(functools.partial(step, False, True))


def _ffn(h, pre, wg, wu, wd, post, *, tm, tf, mc):
    t, d = h.shape
    f = wg.shape[1]
    n_main = (f - 1) // tf
    ft = f - n_main * tf
    n_tail = ft // LANES
    assert n_main >= 1 and ft % LANES == 0 and tm % mc == 0
    vmem = (4 * tm * d * 4 + tm * d * 2 + 2 * 3 * d * tf * 2 + 2 * 3 * d * ft * 2 + 4 * mc * tf * 4
            + 2 * mc * d * 4) // MIB + 4
    main_col = lambda i, j: (0, jnp.minimum(j, n_main - 1))
    main_row = lambda i, j: (jnp.minimum(j, n_main - 1), 0)
    const = lambda shape, idx: pl.BlockSpec(shape, lambda i, j: idx, pipeline_mode=pl.Buffered(1))
    tile0 = n_main * tf // LANES
    col_tails = [const((d, LANES), (0, tile0 + k)) for k in range(n_tail)]
    row_tails = [const((LANES, d), (tile0 + k, 0)) for k in range(n_tail)]
    return pl.pallas_call(
        functools.partial(_ffn_kernel, n_main=n_main, n_tail=n_tail, n_chunks=tm // mc, mc=mc),
        grid=(t // tm, n_main + 1),
        in_specs=[
            pl.BlockSpec((tm, d), lambda i, j: (i, 0)),
            const((1, d), (0, 0)),
            pl.BlockSpec((d, tf), main_col),
            pl.BlockSpec((d, tf), main_col),
            pl.BlockSpec((tf, d), main_row),
            *col_tails,
            *col_tails,
            *row_tails,
            const((1, d), (0, 0)),
        ],
        out_specs=pl.BlockSpec((tm, d), lambda i, j: (i, 0)),
        out_shape=jax.ShapeDtypeStruct((t, d), F32),
        scratch_shapes=[
            pltpu.VMEM((tm, d), BF16),
            pltpu.VMEM((d, ft), BF16),
            pltpu.VMEM((d, ft), BF16),
            pltpu.VMEM((ft, d), BF16),
        ],
        compiler_params=_params(2, vmem),
        name="ffn",
    )(h, pre, wg, wu, wd, *([wg] * n_tail), *([wu] * n_tail), *([wd] * n_tail), post)


def _in_proj_kernel(x_ref, pre_ref, w_ref, wdt_ref, o_ref, odt_ref, u_ref):
    @pl.when(pl.program_id(1) == 0)
    def _():
        u = _rms(x_ref[...], pre_ref[...]).astype(BF16)
        u_ref[...] = u
        odt_ref[...] = _dot(u, wdt_ref[...])

    o_ref[...] = _dot(u_ref[...], w_ref[...])


def _in_proj(h, pre, w, w_dt, *, n, tm, tn):
    t, d = h.shape
    assert n % tn == 0 and n <= w.shape[1] and w_dt.shape == (d, LANES)
    vmem = (2 * tm * d * 4 + tm * d * 2 + 2 * d * tn * 2 + 3 * tm * tn * 4) // MIB + 8
    return pl.pallas_call(
        _in_proj_kernel,
        grid=(t // tm, n // tn),
        in_specs=[
            pl.BlockSpec((tm, d), lambda i, j: (i, 0)),
            pl.BlockSpec((1, d), lambda i, j: (0, 0)),
            pl.BlockSpec((d, tn), lambda i, j: (0, j)),
            pl.BlockSpec((d, LANES), lambda i, j: (0, 0)),
        ],
        out_specs=[
            pl.BlockSpec((tm, tn), lambda i, j: (i, j)),
            pl.BlockSpec((tm, LANES), lambda i, j: (i, 0)),
        ],
        out_shape=[jax.ShapeDtypeStruct((t, n), F32), jax.ShapeDtypeStruct((t, LANES), F32)],
        scratch_shapes=[pltpu.VMEM((tm, d), BF16)],
        compiler_params=_params(2, vmem),
        name="in_proj",
    )(h, pre, w, w_dt)


def _suffix_sum_matrix():
    j = jnp.arange(LANES)[:, None]
    s = jnp.arange(LANES)[None, :]
    half = jnp.concatenate([(j > s), jnp.ones((LANES, LANES), bool)], axis=1)
    return jnp.concatenate([half, half], axis=0).astype(BF16)


def _sb_kernel(q_ref, k_ref, v_ref, w_ref, o_ref, qs_ref, ks_ref, vs_ref, acc_ref, r_ref, *, seq, tq, nh, scale):
    qs_ref[...] = q_ref[...].astype(BF16)
    ks_ref[...] = k_ref[...].astype(BF16)
    vs_ref[...] = v_ref[...].astype(BF16)
    w = w_ref[...]
    n_sub = tq // LANES
    heads = [slice(h * SB_HEAD_DIM, (h + 1) * SB_HEAD_DIM) for h in range(nh)]
    row = lax.broadcasted_iota(jnp.int32, (tq, tq), 0)
    col = lax.broadcasted_iota(jnp.int32, (tq, tq), 1)
    diag_mask = col < row

    def chunk(q0, kj, mask):
        k0 = pl.multiple_of(kj * tq, tq)
        d = [_dot_nt(qs_ref[pl.ds(q0, tq), hs], ks_ref[pl.ds(k0, tq), hs]) for hs in heads]
        log_beta, log_keep = [], []
        for dh in d:
            z = dh * scale
            e = jnp.log(1.0 + jnp.exp2(jnp.abs(dh) * (-scale * LOG2E)))
            lb = jnp.minimum(z, 0.0) - e
            lk = lb - z
            log_beta.append(lb)
            log_keep.append(lk if mask is None else jnp.where(mask, lk, 0.0))
        r = [r_ref[:, hs] for hs in heads]
        a_parts = [[None] * n_sub for _ in heads]
        for c in reversed(range(n_sub)):
            sl = slice(c * LANES, (c + 1) * LANES)
            for h in range(nh):
                hi, lo = _split_bf16(log_keep[h][:, sl])
                er = _dot(jnp.concatenate([hi, lo], axis=1), w)
                a_parts[h][c] = jnp.exp(log_beta[h][:, sl] + er[:, :LANES] + r[h])
                r[h] = r[h] + er[:, LANES:]
        for h, hs in enumerate(heads):
            r_ref[:, hs] = r[h]
            a = jnp.concatenate(a_parts[h], axis=1)
            if mask is not None:
                a = jnp.where(mask, a, 0.0)
            acc_ref[:, hs] += _dot(a.astype(BF16), vs_ref[pl.ds(k0, tq), hs])

    def q_body(qi, carry):
        q0 = pl.multiple_of(qi * tq, tq)
        acc_ref[...] = jnp.zeros_like(acc_ref)
        r_ref[...] = jnp.zeros_like(r_ref)
        chunk(q0, qi, diag_mask)

        def k_body(t, c):
            chunk(q0, qi - 1 - t, None)
            return c

        lax.fori_loop(0, qi, k_body, 0)
        o_ref[pl.ds(q0, tq), :] = acc_ref[...]
        return carry

    lax.fori_loop(0, seq // tq, q_body, 0)


def _sb_attention(proj, *, tq, nh):
    b, s, _ = proj.shape
    wd = nh * SB_HEAD_DIM
    col0 = (COL_Q // wd, COL_K // wd, COL_V // wd)
    return pl.pallas_call(
        functools.partial(_sb_kernel, seq=s, tq=tq, nh=nh, scale=SB_HEAD_DIM ** -0.5),
        grid=(b, SB_HEADS // nh),
        in_specs=[
            pl.BlockSpec((None, s, wd), lambda i, h: (i, 0, col0[0] + h)),
            pl.BlockSpec((None, s, wd), lambda i, h: (i, 0, col0[1] + h)),
            pl.BlockSpec((None, s, wd), lambda i, h: (i, 0, col0[2] + h)),
            pl.BlockSpec((2 * LANES, 2 * LANES), lambda i, h: (0, 0)),
        ],
        out_specs=pl.BlockSpec((None, s, wd), lambda i, h: (i, 0, h)),
        out_shape=jax.ShapeDtypeStruct((b, s, SB_WIDTH), F32),
        scratch_shapes=[
            pltpu.VMEM((s, wd), BF16),
            pltpu.VMEM((s, wd), BF16),
            pltpu.VMEM((s, wd), BF16),
            pltpu.VMEM((tq, wd), F32),
            pltpu.VMEM((tq, nh * LANES), F32),
        ],
        compiler_params=_params(2, 40),
        name="sb_attn",
    )(proj, proj, proj, _suffix_sum_matrix())


def _ssd_kernel(z_ref, xbc_ref, dt_ref, convw_ref, convb_ref, dtb_ref, alog_ref, dskip_ref, norm_ref,
                tri_ref, ex_ref, o_ref, xpad_ref, state_ref, *, lc):
    @pl.when(pl.program_id(1) == 0)
    def _():
        state_ref[...] = jnp.zeros_like(state_ref)
        xpad_ref[:, 0:SUBLANES, :] = jnp.zeros((CONV_DIM // LANES, SUBLANES, LANES), F32)

    act = []
    for c in range(CONV_DIM // LANES):
        cols = slice(c * LANES, (c + 1) * LANES)
        xpad_ref[c, SUBLANES:SUBLANES + lc, :] = xbc_ref[:, cols]
        conv = convb_ref[:, cols]
        for k in range(SSM_CONV):
            shift = SUBLANES - (SSM_CONV - 1) + k
            conv = conv + convw_ref[k:k + 1, cols] * xpad_ref[c, shift:shift + lc, :]
        xpad_ref[c, 0:SUBLANES, :] = xpad_ref[c, lc:lc + SUBLANES, :]
        act.append(_silu(conv))
    n_x, n_b = SSM_INNER // LANES, SSM_GROUPS * SSM_STATE // LANES
    xs = jnp.concatenate(act[:n_x], axis=1)
    bm = jnp.concatenate(act[n_x:n_x + n_b], axis=1)
    cm = jnp.concatenate(act[n_x + n_b:], axis=1)

    dt = _softplus(dt_ref[...] + dtb_ref[...])
    da = dt * (-jnp.exp(alog_ref[...]))
    da_hi, da_lo = _split_bf16(da)
    cs2 = _dot(tri_ref[...], jnp.concatenate([da_hi, da_lo], axis=1))
    cs = cs2[:, :LANES] + cs2[:, LANES:]
    cs_t = cs.T

    ex = ex_ref[...]
    dt_x = _dot(jnp.concatenate(_split_bf16(dt), axis=1), ex)
    cs_x = _dot(jnp.concatenate(_split_bf16(cs), axis=1), ex)
    cs_last = cs_x[lc - 1:lc, :]
    xd = xs * dt_x
    xd_bf = xd.astype(BF16)
    xd_end = (xd * jnp.exp(cs_last - cs_x)).astype(BF16)
    in_decay = jnp.exp(cs_x)
    chunk_decay = jnp.exp(cs_last)

    ri = lax.broadcasted_iota(jnp.int32, (lc, lc), 0)
    ci = lax.broadcasted_iota(jnp.int32, (lc, lc), 1)
    tril = ri >= ci
    head_of_lane = lax.broadcasted_iota(jnp.int32, (1, SSM_GROUP_WIDTH), 1) // SSM_HEAD_DIM

    ys = []
    for g in range(SSM_GROUPS):
        gl = slice(g * SSM_GROUP_WIDTH, (g + 1) * SSM_GROUP_WIDTH)
        sl = slice(g * SSM_STATE, (g + 1) * SSM_STATE)
        bm_g = bm[:, sl]
        cm_g = cm[:, sl].astype(BF16)
        cb = _dot_nt(cm_g, bm_g.astype(BF16))
        xd_g = xd_bf[:, gl]
        y_diag = jnp.zeros((lc, SSM_GROUP_WIDTH), F32)
        for r in range(SSM_HEADS_PER_GROUP):
            h = g * SSM_HEADS_PER_GROUP + r
            seg = cs[:, h:h + 1] - cs_t[h:h + 1, :]
            decay = jnp.exp(jnp.where(tril, seg, -jnp.inf))
            x_h = jnp.where(head_of_lane == r, xd_g, jnp.zeros_like(xd_g))
            y_diag = y_diag + _dot((cb * decay).astype(BF16), x_h)
        state = state_ref[g]
        y_off = _dot(cm_g, state.astype(BF16)) * in_decay[:, gl]
        state_ref[g] = state * chunk_decay[:, gl] + _dot(bm_g.T.astype(BF16), xd_end[:, gl])
        ys.append(y_diag + y_off)
    y = jnp.concatenate(ys, axis=1) + dskip_ref[...] * xs
    y = y * _silu(z_ref[...])
    o_ref[...] = _rms(y, norm_ref[...]).astype(o_ref.dtype)


def _ssd(proj, dt_raw, conv_w, conv_b, dt_bias, a_log, d_skip, ssm_norm, *, lc):
    b, s, _ = proj.shape
    pad_heads = lambda v: jnp.pad(v.astype(F32), (0, LANES - SSM_HEADS)).reshape(1, LANES)
    tri = jnp.tril(jnp.ones((lc, lc), BF16))
    lane_head = jnp.arange(SSM_INNER)[None, :] // SSM_HEAD_DIM
    ex_half = (jnp.arange(LANES)[:, None] == lane_head).astype(BF16)
    ex = jnp.concatenate([ex_half, ex_half], axis=0)
    const = lambda shape: pl.BlockSpec(shape, lambda i, c: (0,) * len(shape))
    return pl.pallas_call(
        functools.partial(_ssd_kernel, lc=lc),
        grid=(b, s // lc),
        in_specs=[
            pl.BlockSpec((None, lc, SSM_INNER), lambda i, c: (i, c, COL_Z // SSM_INNER)),
            pl.BlockSpec((None, lc, CONV_DIM), lambda i, c: (i, c, COL_XBC // CONV_DIM)),
            pl.BlockSpec((None, lc, LANES), lambda i, c: (i, c, 0)),
            const((SSM_CONV, CONV_DIM)),
            const((1, CONV_DIM)),
            const((1, LANES)),
            const((1, LANES)),
            const((1, SSM_INNER)),
            const((1, SSM_INNER)),
            const((lc, lc)),
            const((2 * LANES, SSM_INNER)),
        ],
        out_specs=pl.BlockSpec((None, lc, SSM_INNER), lambda i, c: (i, c, 0)),
        out_shape=jax.ShapeDtypeStruct((b, s, SSM_INNER), BF16),
        scratch_shapes=[
            pltpu.VMEM((CONV_DIM // LANES, lc + SUBLANES, LANES), F32),
            pltpu.VMEM((SSM_GROUPS, SSM_STATE, SSM_GROUP_WIDTH), F32),
        ],
        compiler_params=_params(2, 40),
        name="ssd",
    )(proj, proj, dt_raw, conv_w.astype(F32), conv_b.reshape(1, CONV_DIM).astype(F32), pad_heads(dt_bias),
      pad_heads(a_log), jnp.repeat(d_skip.astype(F32), SSM_HEAD_DIM).reshape(1, SSM_INNER),
      ssm_norm.reshape(1, SSM_INNER).astype(F32), tri, ex)


def _mem_kv_kernel(m_ref, norm_ref, w_ref, o_ref):
    o_ref[...] = _dot(_rms(m_ref[...], norm_ref[...]).astype(BF16), w_ref[...]).astype(o_ref.dtype)


def _mem_kv(mem, kv_norm, w_mkv):
    b, n, d = mem.shape
    return pl.pallas_call(
        _mem_kv_kernel,
        grid=(b,),
        in_specs=[
            pl.BlockSpec((None, n, d), lambda i: (i, 0, 0)),
            pl.BlockSpec((1, d), lambda i: (0, 0)),
            pl.BlockSpec((d, 2 * MEM_WIDTH), lambda i: (0, 0)),
        ],
        out_specs=pl.BlockSpec((None, n, 2 * MEM_WIDTH), lambda i: (i, 0, 0)),
        out_shape=jax.ShapeDtypeStruct((b, n, 2 * MEM_WIDTH), BF16),
        compiler_params=_params(1, 32),
        name="mem_kv",
    )(mem, kv_norm, w_mkv)


def _mix_mem_kernel(sb_ref, y_ref, h_ref, kv_ref, sbn_ref, woa_ref, wob_ref, mixpost_ref, mempre_ref, wmq_ref,
                    wmo_ref, mempost_ref, o_ref, *, scale, n_chunks, mc):
    chunks = [slice(m * mc, (m + 1) * mc) for m in range(n_chunks)]
    sbn = [_rms(sb_ref[rows, :], sbn_ref[...]).astype(BF16) for rows in chunks]
    mix = [_dot(sbn[m], woa_ref[...]) + _dot(y_ref[rows, :], wob_ref[...]) for m, rows in enumerate(chunks)]
    h2 = [h_ref[rows, :] + _rms(mix[m], mixpost_ref[...]) for m, rows in enumerate(chunks)]
    u = [_rms(h2[m], mempre_ref[...]).astype(BF16) for m in range(n_chunks)]
    q = [_dot(u[m], wmq_ref[...]) for m in range(n_chunks)]
    att = []
    for m in range(n_chunks):
        heads = []
        for hd in range(MEM_HEADS):
            sl = slice(hd * MEM_HEAD_DIM, (hd + 1) * MEM_HEAD_DIM)
            k_h = kv_ref[:, sl]
            v_h = kv_ref[:, MEM_WIDTH + hd * MEM_HEAD_DIM:MEM_WIDTH + (hd + 1) * MEM_HEAD_DIM]
            sc = _dot_nt(q[m][:, sl].astype(BF16), k_h) * scale
            e = jnp.exp(sc - jnp.max(sc, axis=-1, keepdims=True))
            p = e / jnp.sum(e, axis=-1, keepdims=True)
            heads.append(_dot(p.astype(BF16), v_h))
        att.append(jnp.concatenate(heads, axis=1).astype(BF16))
    mo = [_dot(att[m], wmo_ref[...]) for m in range(n_chunks)]
    for m, rows in enumerate(chunks):
        o_ref[rows, :] = h2[m] + _rms(mo[m], mempost_ref[...])


def _mix_mem(sb, y, h, kv, sb_norm, wo_sb, wo_y, mix_post, mem_pre, w_mq, w_mo, mem_post, *, tm, mc, seq):
    t, d = h.shape
    n_mem = kv.shape[1]
    steps_per_batch = seq // tm
    const = lambda shape: pl.BlockSpec(shape, lambda i: (0,) * len(shape), pipeline_mode=pl.Buffered(1))
    return pl.pallas_call(
        functools.partial(_mix_mem_kernel, scale=MEM_HEAD_DIM ** -0.5, n_chunks=tm // mc, mc=mc),
        grid=(t // tm,),
        in_specs=[
            pl.BlockSpec((tm, SB_WIDTH), lambda i: (i, 0)),
            pl.BlockSpec((tm, SSM_INNER), lambda i: (i, 0)),
            pl.BlockSpec((tm, d), lambda i: (i, 0)),
            pl.BlockSpec((None, n_mem, 2 * MEM_WIDTH), lambda i: (i // steps_per_batch, 0, 0)),
            const((1, SB_WIDTH)),
            const((SB_WIDTH, d)),
            const((SSM_INNER, d)),
            const((1, d)),
            const((1, d)),
            const((d, MEM_WIDTH)),
            const((MEM_WIDTH, d)),
            const((1, d)),
        ],
        out_specs=pl.BlockSpec((tm, d), lambda i: (i, 0)),
        out_shape=jax.ShapeDtypeStruct((t, d), F32),
        compiler_params=_params(1, 52),
        name="mix_mem",
    )(sb, y, h, kv, sb_norm, wo_sb, wo_y, mix_post, mem_pre, w_mq, w_mo, mem_post)


FFN_TM = 512
FFN_TF = 512
FFN_MC = MXU_DIM
PROJ_TM = 1024
PROJ_TN = 6 * MXU_DIM
SB_TQ = 512
SB_NH = 2
SSD_CHUNK = 128
MIX_TM = 512
MIX_MC = MXU_DIM


def _row(v):
    return v.reshape(1, -1).astype(F32)


def _layer(h, mem, p):
    b, s, d = h.shape
    t = b * s

    def ffn(hh, pre, wg, wu, wd, post):
        tm = min(FFN_TM, t)
        return _ffn(hh, _row(pre), wg.astype(BF16), wu.astype(BF16), wd.astype(BF16), _row(post), tm=tm,
                    tf=FFN_TF, mc=min(FFN_MC, tm))

    h1 = ffn(h.reshape(t, d), p["ffn1_pre"], p["ffn1_wg"], p["ffn1_wu"], p["ffn1_wd"], p["ffn1_post"])

    w_in = p["w_in"]
    w_dt = jnp.pad(w_in[:, PROJ_WIDTH:], ((0, 0), (0, LANES - SSM_HEADS))).astype(BF16)
    proj, dt_raw = _in_proj(h1, _row(p["mix_pre"]), w_in.astype(BF16), w_dt, n=PROJ_WIDTH, tm=min(PROJ_TM, t),
                            tn=PROJ_TN)
    proj = proj.reshape(b, s, PROJ_WIDTH)
    sb = _sb_attention(proj, tq=min(SB_TQ, s), nh=SB_NH)
    y = _ssd(proj, dt_raw.reshape(b, s, LANES), p["conv_w"], p["conv_b"], p["dt_bias"], p["a_log"], p["d_skip"],
             p["ssm_norm"], lc=min(SSD_CHUNK, s))

    kv = _mem_kv(mem, _row(p["mem_kv_norm"]), p["w_mkv"].astype(BF16))
    w_out = p["w_out"].astype(BF16)
    h3 = _mix_mem(sb.reshape(t, SB_WIDTH), y.reshape(t, SSM_INNER), h1, kv, _row(p["sb_norm"]),
                  w_out[:SB_WIDTH], w_out[SB_WIDTH:], _row(p["mix_post"]), _row(p["mem_pre"]),
                  p["w_mq"].astype(BF16), p["w_mo"].astype(BF16), _row(p["mem_post"]), tm=min(MIX_TM, s),
                  mc=min(MIX_MC, s), seq=s)

    out = ffn(h3, p["ffn2_pre"], p["ffn2_wg"], p["ffn2_wu"], p["ffn2_wd"], p["ffn2_post"])
    return out.reshape(b, s, d)


def kernel(x, mem, ffn1_pre, ffn1_wg, ffn1_wu, ffn1_wd, ffn1_post, mix_pre, w_in, conv_w, conv_b, dt_bias, a_log,
           d_skip, sb_norm, ssm_norm, w_out, mix_post, mem_pre, mem_kv_norm, w_mq, w_mkv, w_mo, mem_post, ffn2_pre,
           ffn2_wg, ffn2_wu, ffn2_wd, ffn2_post):
    params = dict(
        ffn1_pre=ffn1_pre, ffn1_wg=ffn1_wg, ffn1_wu=ffn1_wu, ffn1_wd=ffn1_wd, ffn1_post=ffn1_post,
        mix_pre=mix_pre, w_in=w_in, conv_w=conv_w, conv_b=conv_b, dt_bias=dt_bias, a_log=a_log, d_skip=d_skip,
        sb_norm=sb_norm, ssm_norm=ssm_norm, w_out=w_out, mix_post=mix_post, mem_pre=mem_pre,
        mem_kv_norm=mem_kv_norm, w_mq=w_mq, w_mkv=w_mkv, w_mo=w_mo, mem_post=mem_post,
        ffn2_pre=ffn2_pre, ffn2_wg=ffn2_wg, ffn2_wu=ffn2_wu, ffn2_wd=ffn2_wd, ffn2_post=ffn2_post)
    h = x
    for layer in range(ffn1_pre.shape[0]):
        h = _layer(h, mem, {k: v[layer] for k, v in params.items()})
    return h
```

```python
import functools

import jax
import jax.numpy as jnp
from jax import lax
from jax.experimental import pallas as pl
from jax.experimental.pallas import tpu as pltpu

F32 = jnp.float32
BF16 = jnp.bfloat16

EPS = 1e-6
LOG2E = 1.4426950408889634
LANES = 128
SUBLANES = 8
MXU_DIM = 256
VMEM_BYTES_V7X = 64 * 1024 * 1024
MIB = 1024 * 1024

SB_HEADS = 8
SB_HEAD_DIM = 128
SB_WIDTH = SB_HEADS * SB_HEAD_DIM
SSM_HEADS = 16
SSM_HEAD_DIM = 64
SSM_INNER = SSM_HEADS * SSM_HEAD_DIM
SSM_GROUPS = 4
SSM_HEADS_PER_GROUP = SSM_HEADS // SSM_GROUPS
SSM_STATE = 128
SSM_CONV = 4
SSM_GROUP_WIDTH = SSM_HEADS_PER_GROUP * SSM_HEAD_DIM
CONV_DIM = SSM_INNER + 2 * SSM_GROUPS * SSM_STATE
MEM_HEADS = 4
MEM_HEAD_DIM = 128
MEM_WIDTH = MEM_HEADS * MEM_HEAD_DIM

COL_Q = 0
COL_K = SB_WIDTH
COL_V = 2 * SB_WIDTH
COL_Z = 3 * SB_WIDTH
COL_XBC = COL_Z + SSM_INNER
PROJ_WIDTH = COL_XBC + CONV_DIM


def _params(n_grid_axes, vmem_mib):
    return pltpu.CompilerParams(
        dimension_semantics=("arbitrary",) * n_grid_axes,
        vmem_limit_bytes=min(vmem_mib * MIB, VMEM_BYTES_V7X - 4 * MIB),
    )


def _rms(x, w):
    ms = jnp.mean(x * x, axis=-1, keepdims=True)
    return x * lax.rsqrt(ms + EPS) * w


def _silu(x):
    return x * jax.nn.sigmoid(x)


def _softplus(x):
    return jnp.maximum(x, 0.0) + jnp.log1p(jnp.exp(-jnp.abs(x)))


def _split_bf16(x):
    hi = x.astype(BF16)
    lo = (x - hi.astype(F32)).astype(BF16)
    return hi, lo


def _dot(a, b):
    return jnp.dot(a, b, preferred_element_type=F32)


def _dot_nt(a, b):
    return lax.dot_general(a, b, (((1,), (1,)), ((), ())), preferred_element_type=F32)


def _ffn_kernel(x_ref, pre_ref, wg_ref, wu_ref, wd_ref, wg_hbm, wu_hbm, wd_hbm, post_ref, o_ref, u_ref, wgt_ref,
                wut_ref, wdt_ref, sem, *, n_main, f_main, ft, n_chunks, mc):
    i = pl.program_id(0)
    j = pl.program_id(1)

    def tail_copies():
        return (pltpu.make_async_copy(wg_hbm.at[:, pl.ds(f_main, ft)], wgt_ref, sem.at[0]),
                pltpu.make_async_copy(wu_hbm.at[:, pl.ds(f_main, ft)], wut_ref, sem.at[1]),
                pltpu.make_async_copy(wd_hbm.at[pl.ds(f_main, ft), :], wdt_ref, sem.at[2]))

    @pl.when(jnp.logical_and(i == 0, j == 0))
    def _():
        for copy in tail_copies():
            copy.start()

    @pl.when(jnp.logical_and(i == 0, j == n_main))
    def _():
        for copy in tail_copies():
            copy.wait()

    def step(first, last):
        wg, wu, wd = (wgt_ref, wut_ref, wdt_ref) if last else (wg_ref, wu_ref, wd_ref)
        for m in range(n_chunks):
            rows = slice(m * mc, (m + 1) * mc)
            if first:
                u = _rms(x_ref[rows, :], pre_ref[...]).astype(BF16)
                u_ref[rows, :] = u
            else:
                u = u_ref[rows, :]
            act = (_silu(_dot(u, wg[...])) * _dot(u, wu[...])).astype(BF16)
            y = _dot(act, wd[...])
            if not first:
                y = o_ref[rows, :] + y
            if last:
                y = x_ref[rows, :] + 0.5 * _rms(y, post_ref[...])
            o_ref[rows, :] = y

    pl.when(j == 0)(functools.partial(step, True, False))
    pl.when(jnp.logical_and(j > 0, j < n_main))(functools.partial(step, False, False))
    pl.when(j == n_main)(functools.partial(step, False, True))


def _ffn(h, pre, wg, wu, wd, post, *, tm, tf, mc):
    t, d = h.shape
    f = wg.shape[1]
    n_main = (f - 1) // tf
    f_main = n_main * tf
    ft = f - f_main
    assert n_main >= 1 and ft % LANES == 0 and tm % mc == 0
    vmem = (4 * tm * d * 4 + tm * d * 2 + 2 * 3 * d * tf * 2 + 3 * d * ft * 2 + 4 * mc * tf * 4
            + 2 * mc * d * 4) // MIB + 4
    main_col = lambda i, j: (0, jnp.minimum(j, n_main - 1))
    main_row = lambda i, j: (jnp.minimum(j, n_main - 1), 0)
    const = lambda shape: pl.BlockSpec(shape, lambda i, j: (0, 0), pipeline_mode=pl.Buffered(1))
    hbm = pl.BlockSpec(memory_space=pl.ANY)
    return pl.pallas_call(
        functools.partial(_ffn_kernel, n_main=n_main, f_main=f_main, ft=ft, n_chunks=tm // mc, mc=mc),
        grid=(t // tm, n_main + 1),
        in_specs=[
            pl.BlockSpec((tm, d), lambda i, j: (i, 0)),
            const((1, d)),
            pl.BlockSpec((d, tf), main_col),
            pl.BlockSpec((d, tf), main_col),
            pl.BlockSpec((tf, d), main_row),
            hbm,
            hbm,
            hbm,
            const((1, d)),
        ],
        out_specs=pl.BlockSpec((tm, d), lambda i, j: (i, 0)),
        out_shape=jax.ShapeDtypeStruct((t, d), F32),
        scratch_shapes=[
            pltpu.VMEM((tm, d), BF16),
            pltpu.VMEM((d, ft), BF16),
            pltpu.VMEM((d, ft), BF16),
            pltpu.VMEM((ft, d), BF16),
            pltpu.SemaphoreType.DMA((3,)),
        ],
        compiler_params=_params(2, vmem),
        name="ffn",
    )(h, pre, wg, wu, wd, wg, wu, wd, post)


def _in_proj_kernel(x_ref, pre_ref, w_ref, wdt_ref, o_ref, odt_ref, u_ref):
    @pl.when(pl.program_id(1) == 0)
    def _():
        u = _rms(x_ref[...], pre_ref[...]).astype(BF16)
        u_ref[...] = u
        odt_ref[...] = _dot(u, wdt_ref[...])

    o_ref[...] = _dot(u_ref[...], w_ref[...])


def _in_proj(h, pre, w, w_dt, *, n, tm, tn):
    t, d = h.shape
    assert n % tn == 0 and n <= w.shape[1] and w_dt.shape == (d, LANES)
    vmem = (2 * tm * d * 4 + tm * d * 2 + 2 * d * tn * 2 + 3 * tm * tn * 4) // MIB + 8
    return pl.pallas_call(
        _in_proj_kernel,
        grid=(t // tm, n // tn),
        in_specs=[
            pl.BlockSpec((tm, d), lambda i, j: (i, 0)),
            pl.BlockSpec((1, d), lambda i, j: (0, 0)),
            pl.BlockSpec((d, tn), lambda i, j: (0, j)),
            pl.BlockSpec((d, LANES), lambda i, j: (0, 0)),
        ],
        out_specs=[
            pl.BlockSpec((tm, tn), lambda i, j: (i, j)),
            pl.BlockSpec((tm, LANES), lambda i, j: (i, 0)),
        ],
        out_shape=[jax.ShapeDtypeStruct((t, n), F32), jax.ShapeDtypeStruct((t, LANES), F32)],
        scratch_shapes=[pltpu.VMEM((tm, d), BF16)],
        compiler_params=_params(2, vmem),
        name="in_proj",
    )(h, pre, w, w_dt)


def _suffix_sum_matrix():
    j = jnp.arange(LANES)[:, None]
    s = jnp.arange(LANES)[None, :]
    half = jnp.concatenate([(j > s), jnp.ones((LANES, LANES), bool)], axis=1)
    return jnp.concatenate([half, half], axis=0).astype(BF16)


def _sb_kernel(q_ref, k_ref, v_ref, w_ref, o_ref, qs_ref, ks_ref, vs_ref, acc_ref, r_ref, *, seq, tq, nh, scale):
    qs_ref[...] = q_ref[...].astype(BF16)
    ks_ref[...] = k_ref[...].astype(BF16)
    vs_ref[...] = v_ref[...].astype(BF16)
    w = w_ref[...]
    n_sub = tq // LANES
    heads = [slice(h * SB_HEAD_DIM, (h + 1) * SB_HEAD_DIM) for h in range(nh)]
    row = lax.broadcasted_iota(jnp.int32, (tq, tq), 0)
    col = lax.broadcasted_iota(jnp.int32, (tq, tq), 1)
    diag_mask = col < row

    def chunk(q0, kj, mask):
        k0 = pl.multiple_of(kj * tq, tq)
        d = [_dot_nt(qs_ref[pl.ds(q0, tq), hs], ks_ref[pl.ds(k0, tq), hs]) for hs in heads]
        log_beta, log_keep = [], []
        for dh in d:
            z = dh * scale
            e = jnp.log(1.0 + jnp.exp2(jnp.abs(dh) * (-scale * LOG2E)))
            lb = jnp.minimum(z, 0.0) - e
            lk = lb - z
            log_beta.append(lb)
            log_keep.append(lk if mask is None else jnp.where(mask, lk, 0.0))
        r = [r_ref[:, hs] for hs in heads]
        a_parts = [[None] * n_sub for _ in heads]
        for c in reversed(range(n_sub)):
            sl = slice(c * LANES, (c + 1) * LANES)
            for h in range(nh):
                hi, lo = _split_bf16(log_keep[h][:, sl])
                er = _dot(jnp.concatenate([hi, lo], axis=1), w)
                a_parts[h][c] = jnp.exp(log_beta[h][:, sl] + er[:, :LANES] + r[h])
                r[h] = r[h] + er[:, LANES:]
        for h, hs in enumerate(heads):
            r_ref[:, hs] = r[h]
            a = jnp.concatenate(a_parts[h], axis=1)
            if mask is not None:
                a = jnp.where(mask, a, 0.0)
            acc_ref[:, hs] += _dot(a.astype(BF16), vs_ref[pl.ds(k0, tq), hs])

    def q_body(qi, carry):
        q0 = pl.multiple_of(qi * tq, tq)
        acc_ref[...] = jnp.zeros_like(acc_ref)
        r_ref[...] = jnp.zeros_like(r_ref)
        chunk(q0, qi, diag_mask)

        def k_body(t, c):
            chunk(q0, qi - 1 - t, None)
            return c

        lax.fori_loop(0, qi, k_body, 0)
        o_ref[pl.ds(q0, tq), :] = acc_ref[...]
        return carry

    lax.fori_loop(0, seq // tq, q_body, 0)


def _sb_attention(proj, *, tq, nh):
    b, s, _ = proj.shape
    wd = nh * SB_HEAD_DIM
    col0 = (COL_Q // wd, COL_K // wd, COL_V // wd)
    return pl.pallas_call(
        functools.partial(_sb_kernel, seq=s, tq=tq, nh=nh, scale=SB_HEAD_DIM ** -0.5),
        grid=(b, SB_HEADS // nh),
        in_specs=[
            pl.BlockSpec((None, s, wd), lambda i, h: (i, 0, col0[0] + h)),
            pl.BlockSpec((None, s, wd), lambda i, h: (i, 0, col0[1] + h)),
            pl.BlockSpec((None, s, wd), lambda i, h: (i, 0, col0[2] + h)),
            pl.BlockSpec((2 * LANES, 2 * LANES), lambda i, h: (0, 0)),
        ],
        out_specs=pl.BlockSpec((None, s, wd), lambda i, h: (i, 0, h)),
        out_shape=jax.ShapeDtypeStruct((b, s, SB_WIDTH), F32),
        scratch_shapes=[
            pltpu.VMEM((s, wd), BF16),
            pltpu.VMEM((s, wd), BF16),
            pltpu.VMEM((s, wd), BF16),
            pltpu.VMEM((tq, wd), F32),
            pltpu.VMEM((tq, nh * LANES), F32),
        ],
        compiler_params=_params(2, 40),
        name="sb_attn",
    )(proj, proj, proj, _suffix_sum_matrix())


def _ssd_kernel(z_ref, xbc_ref, dt_ref, convw_ref, convb_ref, dtb_ref, alog_ref, dskip_ref, norm_ref,
                tri_ref, ex_ref, o_ref, xpad_ref, state_ref, *, lc):
    @pl.when(pl.program_id(1) == 0)
    def _():
        state_ref[...] = jnp.zeros_like(state_ref)
        xpad_ref[:, 0:SUBLANES, :] = jnp.zeros((CONV_DIM // LANES, SUBLANES, LANES), F32)

    act = []
    for c in range(CONV_DIM // LANES):
        cols = slice(c * LANES, (c + 1) * LANES)
        xpad_ref[c, SUBLANES:SUBLANES + lc, :] = xbc_ref[:, cols]
        conv = convb_ref[:, cols]
        for k in range(SSM_CONV):
            shift = SUBLANES - (SSM_CONV - 1) + k
            conv = conv + convw_ref[k:k + 1, cols] * xpad_ref[c, shift:shift + lc, :]
        xpad_ref[c, 0:SUBLANES, :] = xpad_ref[c, lc:lc + SUBLANES, :]
        act.append(_silu(conv))
    n_x, n_b = SSM_INNER // LANES, SSM_GROUPS * SSM_STATE // LANES
    xs = jnp.concatenate(act[:n_x], axis=1)
    bm = jnp.concatenate(act[n_x:n_x + n_b], axis=1)
    cm = jnp.concatenate(act[n_x + n_b:], axis=1)

    dt = _softplus(dt_ref[...] + dtb_ref[...])
    da = dt * (-jnp.exp(alog_ref[...]))
    da_hi, da_lo = _split_bf16(da)
    cs2 = _dot(tri_ref[...], jnp.concatenate([da_hi, da_lo], axis=1))
    cs = cs2[:, :LANES] + cs2[:, LANES:]
    cs_t = cs.T

    ex = ex_ref[...]
    dt_x = _dot(jnp.concatenate(_split_bf16(dt), axis=1), ex)
    cs_x = _dot(jnp.concatenate(_split_bf16(cs), axis=1), ex)
    cs_last = cs_x[lc - 1:lc, :]
    xd = xs * dt_x
    xd_bf = xd.astype(BF16)
    xd_end = (xd * jnp.exp(cs_last - cs_x)).astype(BF16)
    in_decay = jnp.exp(cs_x)
    chunk_decay = jnp.exp(cs_last)

    ri = lax.broadcasted_iota(jnp.int32, (lc, lc), 0)
    ci = lax.broadcasted_iota(jnp.int32, (lc, lc), 1)
    tril = ri >= ci
    head_of_lane = lax.broadcasted_iota(jnp.int32, (1, SSM_GROUP_WIDTH), 1) // SSM_HEAD_DIM

    ys = []
    for g in range(SSM_GROUPS):
        gl = slice(g * SSM_GROUP_WIDTH, (g + 1) * SSM_GROUP_WIDTH)
        sl = slice(g * SSM_STATE, (g + 1) * SSM_STATE)
        bm_g = bm[:, sl]
        cm_g = cm[:, sl].astype(BF16)
        cb = _dot_nt(cm_g, bm_g.astype(BF16))
        xd_g = xd_bf[:, gl]
        y_diag = jnp.zeros((lc, SSM_GROUP_WIDTH), F32)
        for r in range(SSM_HEADS_PER_GROUP):
            h = g * SSM_HEADS_PER_GROUP + r
            seg = cs[:, h:h + 1] - cs_t[h:h + 1, :]
            decay = jnp.exp(jnp.where(tril, seg, -jnp.inf))
            x_h = jnp.where(head_of_lane == r, xd_g, jnp.zeros_like(xd_g))
            y_diag = y_diag + _dot((cb * decay).astype(BF16), x_h)
        state = state_ref[g]
        y_off = _dot(cm_g, state.astype(BF16)) * in_decay[:, gl]
        state_ref[g] = state * chunk_decay[:, gl] + _dot(bm_g.T.astype(BF16), xd_end[:, gl])
        ys.append(y_diag + y_off)
    y = jnp.concatenate(ys, axis=1) + dskip_ref[...] * xs
    y = y * _silu(z_ref[...])
    o_ref[...] = _rms(y, norm_ref[...]).astype(o_ref.dtype)


def _ssd(proj, dt_raw, conv_w, conv_b, dt_bias, a_log, d_skip, ssm_norm, *, lc):
    b, s, _ = proj.shape
    pad_heads = lambda v: jnp.pad(v.astype(F32), (0, LANES - SSM_HEADS)).reshape(1, LANES)
    tri = jnp.tril(jnp.ones((lc, lc), BF16))
    lane_head = jnp.arange(SSM_INNER)[None, :] // SSM_HEAD_DIM
    ex_half = (jnp.arange(LANES)[:, None] == lane_head).astype(BF16)
    ex = jnp.concatenate([ex_half, ex_half], axis=0)
    const = lambda shape: pl.BlockSpec(shape, lambda i, c: (0,) * len(shape))
    return pl.pallas_call(
        functools.partial(_ssd_kernel, lc=lc),
        grid=(b, s // lc),
        in_specs=[
            pl.BlockSpec((None, lc, SSM_INNER), lambda i, c: (i, c, COL_Z // SSM_INNER)),
            pl.BlockSpec((None, lc, CONV_DIM), lambda i, c: (i, c, COL_XBC // CONV_DIM)),
            pl.BlockSpec((None, lc, LANES), lambda i, c: (i, c, 0)),
            const((SSM_CONV, CONV_DIM)),
            const((1, CONV_DIM)),
            const((1, LANES)),
            const((1, LANES)),
            const((1, SSM_INNER)),
            const((1, SSM_INNER)),
            const((lc, lc)),
            const((2 * LANES, SSM_INNER)),
        ],
        out_specs=pl.BlockSpec((None, lc, SSM_INNER), lambda i, c: (i, c, 0)),
        out_shape=jax.ShapeDtypeStruct((b, s, SSM_INNER), BF16),
        scratch_shapes=[
            pltpu.VMEM((CONV_DIM // LANES, lc + SUBLANES, LANES), F32),
            pltpu.VMEM((SSM_GROUPS, SSM_STATE, SSM_GROUP_WIDTH), F32),
        ],
        compiler_params=_params(2, 40),
        name="ssd",
    )(proj, proj, dt_raw, conv_w.astype(F32), conv_b.reshape(1, CONV_DIM).astype(F32), pad_heads(dt_bias),
      pad_heads(a_log), jnp.repeat(d_skip.astype(F32), SSM_HEAD_DIM).reshape(1, SSM_INNER),
      ssm_norm.reshape(1, SSM_INNER).astype(F32), tri, ex)


def _mem_kv_kernel(m_ref, norm_ref, w_ref, o_ref):
    o_ref[...] = _dot(_rms(m_ref[...], norm_ref[...]).astype(BF16), w_ref[...]).astype(o_ref.dtype)


def _mem_kv(mem, kv_norm, w_mkv):
    b, n, d = mem.shape
    return pl.pallas_call(
        _mem_kv_kernel,
        grid=(b,),
        in_specs=[
            pl.BlockSpec((None, n, d), lambda i: (i, 0, 0)),
            pl.BlockSpec((1, d), lambda i: (0, 0)),
            pl.BlockSpec((d, 2 * MEM_WIDTH), lambda i: (0, 0)),
        ],
        out_specs=pl.BlockSpec((None, n, 2 * MEM_WIDTH), lambda i: (i, 0, 0)),
        out_shape=jax.ShapeDtypeStruct((b, n, 2 * MEM_WIDTH), BF16),
        compiler_params=_params(1, 32),
        name="mem_kv",
    )(mem, kv_norm, w_mkv)


def _mix_mem_kernel(sb_ref, y_ref, h_ref, kv_ref, sbn_ref, woa_ref, wob_ref, mixpost_ref, mempre_ref, wmq_ref,
                    wmo_ref, mempost_ref, o_ref, *, scale, n_chunks, mc):
    chunks = [slice(m * mc, (m + 1) * mc) for m in range(n_chunks)]
    sbn = [_rms(sb_ref[rows, :], sbn_ref[...]).astype(BF16) for rows in chunks]
    mix = [_dot(sbn[m], woa_ref[...]) + _dot(y_ref[rows, :], wob_ref[...]) for m, rows in enumerate(chunks)]
    h2 = [h_ref[rows, :] + _rms(mix[m], mixpost_ref[...]) for m, rows in enumerate(chunks)]
    u = [_rms(h2[m], mempre_ref[...]).astype(BF16) for m in range(n_chunks)]
    q = [_dot(u[m], wmq_ref[...]) for m in range(n_chunks)]
    att = []
    for m in range(n_chunks):
        heads = []
        for hd in range(MEM_HEADS):
            sl = slice(hd * MEM_HEAD_DIM, (hd + 1) * MEM_HEAD_DIM)
            k_h = kv_ref[:, sl]
            v_h = kv_ref[:, MEM_WIDTH + hd * MEM_HEAD_DIM:MEM_WIDTH + (hd + 1) * MEM_HEAD_DIM]
            sc = _dot_nt(q[m][:, sl].astype(BF16), k_h) * scale
            e = jnp.exp(sc - jnp.max(sc, axis=-1, keepdims=True))
            p = e / jnp.sum(e, axis=-1, keepdims=True)
            heads.append(_dot(p.astype(BF16), v_h))
        att.append(jnp.concatenate(heads, axis=1).astype(BF16))
    mo = [_dot(att[m], wmo_ref[...]) for m in range(n_chunks)]
    for m, rows in enumerate(chunks):
        o_ref[rows, :] = h2[m] + _rms(mo[m], mempost_ref[...])


def _mix_mem(sb, y, h, kv, sb_norm, wo_sb, wo_y, mix_post, mem_pre, w_mq, w_mo, mem_post, *, tm, mc, seq):
    t, d = h.shape
    n_mem = kv.shape[1]
    steps_per_batch = seq // tm
    const = lambda shape: pl.BlockSpec(shape, lambda i: (0,) * len(shape), pipeline_mode=pl.Buffered(1))
    return pl.pallas_call(
        functools.partial(_mix_mem_kernel, scale=MEM_HEAD_DIM ** -0.5, n_chunks=tm // mc, mc=mc),
        grid=(t // tm,),
        in_specs=[
            pl.BlockSpec((tm, SB_WIDTH), lambda i: (i, 0)),
            pl.BlockSpec((tm, SSM_INNER), lambda i: (i, 0)),
            pl.BlockSpec((tm, d), lambda i: (i, 0)),
            pl.BlockSpec((None, n_mem, 2 * MEM_WIDTH), lambda i: (i // steps_per_batch, 0, 0)),
            const((1, SB_WIDTH)),
            const((SB_WIDTH, d)),
            const((SSM_INNER, d)),
            const((1, d)),
            const((1, d)),
            const((d, MEM_WIDTH)),
            const((MEM_WIDTH, d)),
            const((1, d)),
        ],
        out_specs=pl.BlockSpec((tm, d), lambda i: (i, 0)),
        out_shape=jax.ShapeDtypeStruct((t, d), F32),
        compiler_params=_params(1, 52),
        name="mix_mem",
    )(sb, y, h, kv, sb_norm, wo_sb, wo_y, mix_post, mem_pre, w_mq, w_mo, mem_post)


FFN_TM = 1024
FFN_TF = 512
FFN_MC = MXU_DIM
PROJ_TM = 1024
PROJ_TN = 6 * MXU_DIM
SB_TQ = 512
SB_NH = 2
SSD_CHUNK = 128
MIX_TM = 512
MIX_MC = MXU_DIM


def _row(v):
    return v.reshape(1, -1).astype(F32)


def _layer(h, mem, p):
    b, s, d = h.shape
    t = b * s

    def ffn(hh, pre, wg, wu, wd, post):
        tm = min(FFN_TM, t)
        return _ffn(hh, _row(pre), wg.astype(BF16), wu.astype(BF16), wd.astype(BF16), _row(post), tm=tm,
                    tf=FFN_TF, mc=min(FFN_MC, tm))

    h1 = ffn(h.reshape(t, d), p["ffn1_pre"], p["ffn1_wg"], p["ffn1_wu"], p["ffn1_wd"], p["ffn1_post"])

    w_in = p["w_in"]
    w_dt = jnp.pad(w_in[:, PROJ_WIDTH:], ((0, 0), (0, LANES - SSM_HEADS))).astype(BF16)
    proj, dt_raw = _in_proj(h1, _row(p["mix_pre"]), w_in.astype(BF16), w_dt, n=PROJ_WIDTH, tm=min(PROJ_TM, t),
                            tn=PROJ_TN)
    proj = proj.reshape(b, s, PROJ_WIDTH)
    sb = _sb_attention(proj, tq=min(SB_TQ, s), nh=SB_NH)
    y = _ssd(proj, dt_raw.reshape(b, s, LANES), p["conv_w"], p["conv_b"], p["dt_bias"], p["a_log"], p["d_skip"],
             p["ssm_norm"], lc=min(SSD_CHUNK, s))

    kv = _mem_kv(mem, _row(p["mem_kv_norm"]), p["w_mkv"].astype(BF16))
    w_out = p["w_out"].astype(BF16)
    h3 = _mix_mem(sb.reshape(t, SB_WIDTH), y.reshape(t, SSM_INNER), h1, kv, _row(p["sb_norm"]),
                  w_out[:SB_WIDTH], w_out[SB_WIDTH:], _row(p["mix_post"]), _row(p["mem_pre"]),
                  p["w_mq"].astype(BF16), p["w_mo"].astype(BF16), _row(p["mem_post"]), tm=min(MIX_TM, s),
                  mc=min(MIX_MC, s), seq=s)

    out = ffn(h3, p["ffn2_pre"], p["ffn2_wg"], p["ffn2_wu"], p["ffn2_wd"], p["ffn2_post"])
    return out.reshape(b, s, d)


def kernel(x, mem, ffn1_pre, ffn1_wg, ffn1_wu, ffn1_wd, ffn1_post, mix_pre, w_in, conv_w, conv_b, dt_bias, a_log,
           d_skip, sb_norm, ssm_norm, w_out, mix_post, mem_pre, mem_kv_norm, w_mq, w_mkv, w_mo, mem_post, ffn2_pre,
           ffn2_wg, ffn2_wu, ffn2_wd, ffn2_post):
    params = dict(
        ffn1_pre=ffn1_pre, ffn1_wg=ffn1_wg, ffn1_wu=ffn1_wu, ffn1_wd=ffn1_wd, ffn1_post=ffn1_post,
        mix_pre=mix_pre, w_in=w_in, conv_w=conv_w, conv_b=conv_b, dt_bias=dt_bias, a_log=a_log, d_skip=d_skip,
        sb_norm=sb_norm, ssm_norm=ssm_norm, w_out=w_out, mix_post=mix_post, mem_pre=mem_pre,
        mem_kv_norm=mem_kv_norm, w_mq=w_mq, w_mkv=w_mkv, w_mo=w_mo, mem_post=mem_post,
        ffn2_pre=ffn2_pre, ffn2_wg=ffn2_wg, ffn2_wu=ffn2_wu, ffn2_wd=ffn2_wd, ffn2_post=ffn2_post)
    h = x
    for layer in range(ffn1_pre.shape[0]):
        h = _layer(h, mem, {k: v[layer] for k, v in params.items()})
    return h
```

```python
import functools

import jax
import jax.numpy as jnp
from jax import lax
from jax.experimental import pallas as pl
from jax.experimental.pallas import tpu as pltpu

F32 = jnp.float32
BF16 = jnp.bfloat16

EPS = 1e-6
LOG2E = 1.4426950408889634
LANES = 128
SUBLANES = 8
MXU_DIM = 256
VMEM_BYTES_V7X = 64 * 1024 * 1024
MIB = 1024 * 1024

SB_HEADS = 8
SB_HEAD_DIM = 128
SB_WIDTH = SB_HEADS * SB_HEAD_DIM
SSM_HEADS = 16
SSM_HEAD_DIM = 64
SSM_INNER = SSM_HEADS * SSM_HEAD_DIM
SSM_GROUPS = 4
SSM_HEADS_PER_GROUP = SSM_HEADS // SSM_GROUPS
SSM_STATE = 128
SSM_CONV = 4
SSM_GROUP_WIDTH = SSM_HEADS_PER_GROUP * SSM_HEAD_DIM
CONV_DIM = SSM_INNER + 2 * SSM_GROUPS * SSM_STATE
MEM_HEADS = 4
MEM_HEAD_DIM = 128
MEM_WIDTH = MEM_HEADS * MEM_HEAD_DIM

COL_Q = 0
COL_K = SB_WIDTH
COL_V = 2 * SB_WIDTH
COL_Z = 3 * SB_WIDTH
COL_XBC = COL_Z + SSM_INNER
PROJ_WIDTH = COL_XBC + CONV_DIM


def _params(n_grid_axes, vmem_mib):
    return pltpu.CompilerParams(
        dimension_semantics=("arbitrary",) * n_grid_axes,
        vmem_limit_bytes=min(vmem_mib * MIB, VMEM_BYTES_V7X - 4 * MIB),
    )


def _rms(x, w):
    ms = jnp.mean(x * x, axis=-1, keepdims=True)
    return x * lax.rsqrt(ms + EPS) * w


def _silu(x):
    return x * jax.nn.sigmoid(x)


def _softplus(x):
    return jnp.maximum(x, 0.0) + jnp.log1p(jnp.exp(-jnp.abs(x)))


def _split_bf16(x):
    hi = x.astype(BF16)
    lo = (x - hi.astype(F32)).astype(BF16)
    return hi, lo


def _dot(a, b):
    return jnp.dot(a, b, preferred_element_type=F32)


def _dot_nt(a, b):
    return lax.dot_general(a, b, (((1,), (1,)), ((), ())), preferred_element_type=F32)


def _ffn_kernel(x_ref, pre_ref, wg_ref, wu_ref, wd_ref, wg_hbm, wu_hbm, wd_hbm, post_ref, o_ref, u_ref, wgt_ref,
                wut_ref, wdt_ref, sem, *, n_main, f_main, ft, n_chunks, mc):
    i = pl.program_id(0)
    j = pl.program_id(1)

    def tail_copies():
        return (pltpu.make_async_copy(wg_hbm.at[:, pl.ds(f_main, ft)], wgt_ref, sem.at[0]),
                pltpu.make_async_copy(wu_hbm.at[:, pl.ds(f_main, ft)], wut_ref, sem.at[1]),
                pltpu.make_async_copy(wd_hbm.at[pl.ds(f_main, ft), :], wdt_ref, sem.at[2]))

    @pl.when(jnp.logical_and(i == 0, j == 0))
    def _():
        for copy in tail_copies():
            copy.start()

    @pl.when(jnp.logical_and(i == 0, j == n_main))
    def _():
        for copy in tail_copies():
            copy.wait()

    def step(first, last):
        wg, wu, wd = (wgt_ref, wut_ref, wdt_ref) if last else (wg_ref, wu_ref, wd_ref)
        for m in range(n_chunks):
            rows = slice(m * mc, (m + 1) * mc)
            if first:
                u = _rms(x_ref[rows, :], pre_ref[...]).astype(BF16)
                u_ref[rows, :] = u
            else:
                u = u_ref[rows, :]
            act = (_silu(_dot(u, wg[...])) * _dot(u, wu[...])).astype(BF16)
            y = _dot(act, wd[...])
            if not first:
                y = o_ref[rows, :] + y
            if last:
                y = x_ref[rows, :] + 0.5 * _rms(y, post_ref[...])
            o_ref[rows, :] = y

    pl.when(j == 0)(functools.partial(step, True, False))
    pl.when(jnp.logical_and(j > 0, j < n_main))(functools.partial(step, False, False))
    pl.when(j == n_main)(functools.partial(step, False, True))


def _ffn(h, pre, wg, wu, wd, post, *, tm, tf, mc):
    t, d = h.shape
    f = wg.shape[1]
    n_main = (f - 1) // tf
    f_main = n_main * tf
    ft = f - f_main
    assert n_main >= 1 and ft % LANES == 0 and tm % mc == 0
    vmem = (4 * tm * d * 4 + tm * d * 2 + 2 * 3 * d * tf * 2 + 3 * d * ft * 2 + 4 * mc * tf * 4
            + 2 * mc * d * 4) // MIB + 4
    main_col = lambda i, j: (0, jnp.minimum(j, n_main - 1))
    main_row = lambda i, j: (jnp.minimum(j, n_main - 1), 0)
    const = lambda shape: pl.BlockSpec(shape, lambda i, j: (0, 0), pipeline_mode=pl.Buffered(1))
    hbm = pl.BlockSpec(memory_space=pl.ANY)
    return pl.pallas_call(
        functools.partial(_ffn_kernel, n_main=n_main, f_main=f_main, ft=ft, n_chunks=tm // mc, mc=mc),
        grid=(t // tm, n_main + 1),
        in_specs=[
            pl.BlockSpec((tm, d), lambda i, j: (i, 0)),
            const((1, d)),
            pl.BlockSpec((d, tf), main_col),
            pl.BlockSpec((d, tf), main_col),
            pl.BlockSpec((tf, d), main_row),
            hbm,
            hbm,
            hbm,
            const((1, d)),
        ],
        out_specs=pl.BlockSpec((tm, d), lambda i, j: (i, 0)),
        out_shape=jax.ShapeDtypeStruct((t, d), F32),
        scratch_shapes=[
            pltpu.VMEM((tm, d), BF16),
            pltpu.VMEM((d, ft), BF16),
            pltpu.VMEM((d, ft), BF16),
            pltpu.VMEM((ft, d), BF16),
            pltpu.SemaphoreType.DMA((3,)),
        ],
        compiler_params=_params(2, vmem),
        name="ffn",
    )(h, pre, wg, wu, wd, wg, wu, wd, post)


def _in_proj_kernel(x_ref, pre_ref, w_ref, wdt_ref, o_ref, odt_ref, u_ref):
    @pl.when(pl.program_id(1) == 0)
    def _():
        u = _rms(x_ref[...], pre_ref[...]).astype(BF16)
        u_ref[...] = u
        odt_ref[...] = _dot(u, wdt_ref[...])

    o_ref[...] = _dot(u_ref[...], w_ref[...])


def _in_proj(h, pre, w, w_dt, *, n, tm, tn):
    t, d = h.shape
    assert n % tn == 0 and n <= w.shape[1] and w_dt.shape == (d, LANES)
    vmem = (2 * tm * d * 4 + tm * d * 2 + 2 * d * tn * 2 + 3 * tm * tn * 4) // MIB + 8
    return pl.pallas_call(
        _in_proj_kernel,
        grid=(t // tm, n // tn),
        in_specs=[
            pl.BlockSpec((tm, d), lambda i, j: (i, 0)),
            pl.BlockSpec((1, d), lambda i, j: (0, 0)),
            pl.BlockSpec((d, tn), lambda i, j: (0, j)),
            pl.BlockSpec((d, LANES), lambda i, j: (0, 0)),
        ],
        out_specs=[
            pl.BlockSpec((tm, tn), lambda i, j: (i, j)),
            pl.BlockSpec((tm, LANES), lambda i, j: (i, 0)),
        ],
        out_shape=[jax.ShapeDtypeStruct((t, n), F32), jax.ShapeDtypeStruct((t, LANES), F32)],
        scratch_shapes=[pltpu.VMEM((tm, d), BF16)],
        compiler_params=_params(2, vmem),
        name="in_proj",
    )(h, pre, w, w_dt)


def _suffix_sum_matrix():
    j = jnp.arange(LANES)[:, None]
    s = jnp.arange(LANES)[None, :]
    half = jnp.concatenate([(j > s), jnp.ones((LANES, LANES), bool)], axis=1)
    return jnp.concatenate([half, half], axis=0).astype(BF16)


def _sb_kernel(q_ref, k_ref, v_ref, w_ref, o_ref, qs_ref, ks_ref, vs_ref, acc_ref, r_ref, *, seq, tq, nh, scale):
    qs_ref[...] = q_ref[...].astype(BF16)
    ks_ref[...] = k_ref[...].astype(BF16)
    vs_ref[...] = v_ref[...].astype(BF16)
    w = w_ref[...]
    n_sub = tq // LANES
    heads = [slice(h * SB_HEAD_DIM, (h + 1) * SB_HEAD_DIM) for h in range(nh)]
    def qk(q0, k0):
        return [_dot_nt(qs_ref[q0:q0 + tq, hs], ks_ref[k0:k0 + tq, hs]) for hs in heads]

    def scores(d, c, diag):
        row0 = c * LANES if diag else 0
        mask = None
        if diag:
            mask = (lax.broadcasted_iota(jnp.int32, (tq - row0, LANES), 1)
                    < lax.broadcasted_iota(jnp.int32, (tq - row0, LANES), 0))
        per_head = []
        for dh_full in d:
            dh = dh_full[row0:, c * LANES:(c + 1) * LANES]
            z = dh * scale
            e = jnp.log(1.0 + jnp.exp2(jnp.abs(dh) * (-scale * LOG2E)))
            lb = jnp.minimum(z, 0.0) - e
            lk = lb - z
            per_head.append((lb, lk if mask is None else jnp.where(mask, lk, 0.0)))
        return row0, mask, per_head

    def accumulate(sc, a_parts, c):
        row0, mask, per_head = sc
        rows = slice(row0, tq)
        for h, hs in enumerate(heads):
            lb, lk = per_head[h]
            hi, lo = _split_bf16(lk)
            er = _dot(jnp.concatenate([hi, lo], axis=1), w)
            r = r_ref[rows, hs]
            a = jnp.exp(lb + er[:, :LANES] + r)
            r_ref[rows, hs] = r + er[:, LANES:]
            if mask is not None:
                a = jnp.where(mask, a, 0.0)
            a = a.astype(BF16)
            if row0:
                a = jnp.concatenate([jnp.zeros((row0, LANES), BF16), a], axis=0)
            a_parts[h][c] = a

    n_q = seq // tq
    chunks = [(qi * tq, kj * tq) for qi in range(n_q) for kj in range(qi, -1, -1)]
    d = qk(*chunks[0])
    sc = [scores(d, c, True) for c in range(n_sub)]
    for n, (q0, k0) in enumerate(chunks):
        if k0 == q0:
            acc_ref[...] = jnp.zeros_like(acc_ref)
            r_ref[...] = jnp.zeros_like(r_ref)
        nxt = chunks[n + 1] if n + 1 < len(chunks) else None
        if nxt is not None:
            d = qk(*nxt)
        sc_next = [None] * n_sub
        a_parts = [[None] * n_sub for _ in heads]
        for c in reversed(range(n_sub)):
            accumulate(sc[c], a_parts, c)
            if nxt is not None:
                sc_next[c] = scores(d, c, nxt[0] == nxt[1])
        for h, hs in enumerate(heads):
            acc_ref[:, hs] += _dot(jnp.concatenate(a_parts[h], axis=1), vs_ref[k0:k0 + tq, hs])
        if k0 == 0:
            o_ref[q0:q0 + tq, :] = acc_ref[...]
        sc = sc_next


def _sb_attention(proj, *, tq, nh):
    b, s, _ = proj.shape
    wd = nh * SB_HEAD_DIM
    col0 = (COL_Q // wd, COL_K // wd, COL_V // wd)
    return pl.pallas_call(
        functools.partial(_sb_kernel, seq=s, tq=tq, nh=nh, scale=SB_HEAD_DIM ** -0.5),
        grid=(b, SB_HEADS // nh),
        in_specs=[
            pl.BlockSpec((None, s, wd), lambda i, h: (i, 0, col0[0] + h)),
            pl.BlockSpec((None, s, wd), lambda i, h: (i, 0, col0[1] + h)),
            pl.BlockSpec((None, s, wd), lambda i, h: (i, 0, col0[2] + h)),
            pl.BlockSpec((2 * LANES, 2 * LANES), lambda i, h: (0, 0)),
        ],
        out_specs=pl.BlockSpec((None, s, wd), lambda i, h: (i, 0, h)),
        out_shape=jax.ShapeDtypeStruct((b, s, SB_WIDTH), F32),
        scratch_shapes=[
            pltpu.VMEM((s, wd), BF16),
            pltpu.VMEM((s, wd), BF16),
            pltpu.VMEM((s, wd), BF16),
            pltpu.VMEM((tq, wd), F32),
            pltpu.VMEM((tq, nh * LANES), F32),
        ],
        compiler_params=_params(2, 40),
        name="sb_attn",
    )(proj, proj, proj, _suffix_sum_matrix())


def _ssd_kernel(z_ref, xbc_ref, dt_ref, convw_ref, convb_ref, dtb_ref, alog_ref, dskip_ref, norm_ref,
                tri_ref, ex_ref, o_ref, xpad_ref, state_ref, *, lc, n_chunks):
    @pl.when(pl.program_id(1) == 0)
    def _():
        state_ref[...] = jnp.zeros_like(state_ref)
        xpad_ref[:, 0:SUBLANES, :] = jnp.zeros((CONV_DIM // LANES, SUBLANES, LANES), F32)

    ri = lax.broadcasted_iota(jnp.int32, (lc, lc), 0)
    ci = lax.broadcasted_iota(jnp.int32, (lc, lc), 1)
    tril = ri >= ci
    head_of_lane = lax.broadcasted_iota(jnp.int32, (1, SSM_GROUP_WIDTH), 1) // SSM_HEAD_DIM
    n_x, n_b = SSM_INNER // LANES, SSM_GROUPS * SSM_STATE // LANES

    for sub in range(n_chunks):
        rows = slice(sub * lc, (sub + 1) * lc)
        act = []
        for c in range(CONV_DIM // LANES):
            cols = slice(c * LANES, (c + 1) * LANES)
            xpad_ref[c, SUBLANES:SUBLANES + lc, :] = xbc_ref[rows, cols]
            conv = convb_ref[:, cols]
            for k in range(SSM_CONV):
                shift = SUBLANES - (SSM_CONV - 1) + k
                conv = conv + convw_ref[k:k + 1, cols] * xpad_ref[c, shift:shift + lc, :]
            xpad_ref[c, 0:SUBLANES, :] = xpad_ref[c, lc:lc + SUBLANES, :]
            act.append(_silu(conv))
        xs = jnp.concatenate(act[:n_x], axis=1)
        bm = jnp.concatenate(act[n_x:n_x + n_b], axis=1)
        cm = jnp.concatenate(act[n_x + n_b:], axis=1)

        dt = _softplus(dt_ref[rows, :] + dtb_ref[...])
        da = dt * (-jnp.exp(alog_ref[...]))
        da_hi, da_lo = _split_bf16(da)
        cs2 = _dot(tri_ref[...], jnp.concatenate([da_hi, da_lo], axis=1))
        cs = cs2[:, :LANES] + cs2[:, LANES:]
        cs_t = cs.T

        ex = ex_ref[...]
        dt_x = _dot(jnp.concatenate(_split_bf16(dt), axis=1), ex)
        cs_x = _dot(jnp.concatenate(_split_bf16(cs), axis=1), ex)
        cs_last = cs_x[lc - 1:lc, :]
        xd = xs * dt_x
        xd_bf = xd.astype(BF16)
        xd_end = (xd * jnp.exp(cs_last - cs_x)).astype(BF16)
        in_decay = jnp.exp(cs_x)
        chunk_decay = jnp.exp(cs_last)

        ys = []
        for g in range(SSM_GROUPS):
            gl = slice(g * SSM_GROUP_WIDTH, (g + 1) * SSM_GROUP_WIDTH)
            sl = slice(g * SSM_STATE, (g + 1) * SSM_STATE)
            bm_g = bm[:, sl]
            cm_g = cm[:, sl].astype(BF16)
            cb = _dot_nt(cm_g, bm_g.astype(BF16))
            xd_g = xd_bf[:, gl]
            y_diag = jnp.zeros((lc, SSM_GROUP_WIDTH), F32)
            for r in range(SSM_HEADS_PER_GROUP):
                h = g * SSM_HEADS_PER_GROUP + r
                seg = cs[:, h:h + 1] - cs_t[h:h + 1, :]
                decay = jnp.exp(jnp.where(tril, seg, -jnp.inf))
                x_h = jnp.where(head_of_lane == r, xd_g, jnp.zeros_like(xd_g))
                y_diag = y_diag + _dot((cb * decay).astype(BF16), x_h)
            state = state_ref[g]
            y_off = _dot(cm_g, state.astype(BF16)) * in_decay[:, gl]
            state_ref[g] = state * chunk_decay[:, gl] + _dot(bm_g.T.astype(BF16), xd_end[:, gl])
            ys.append(y_diag + y_off)
        y = jnp.concatenate(ys, axis=1) + dskip_ref[...] * xs
        y = y * _silu(z_ref[rows, :])
        o_ref[rows, :] = _rms(y, norm_ref[...]).astype(o_ref.dtype)


def _ssd(proj, dt_raw, conv_w, conv_b, dt_bias, a_log, d_skip, ssm_norm, *, lc, n_chunks):
    b, s, _ = proj.shape
    ts = lc * n_chunks
    pad_heads = lambda v: jnp.pad(v.astype(F32), (0, LANES - SSM_HEADS)).reshape(1, LANES)
    tri = jnp.tril(jnp.ones((lc, lc), BF16))
    lane_head = jnp.arange(SSM_INNER)[None, :] // SSM_HEAD_DIM
    ex_half = (jnp.arange(LANES)[:, None] == lane_head).astype(BF16)
    ex = jnp.concatenate([ex_half, ex_half], axis=0)
    const = lambda shape: pl.BlockSpec(shape, lambda i, c: (0,) * len(shape))
    return pl.pallas_call(
        functools.partial(_ssd_kernel, lc=lc, n_chunks=n_chunks),
        grid=(b, s // ts),
        in_specs=[
            pl.BlockSpec((None, ts, SSM_INNER), lambda i, c: (i, c, COL_Z // SSM_INNER)),
            pl.BlockSpec((None, ts, CONV_DIM), lambda i, c: (i, c, COL_XBC // CONV_DIM)),
            pl.BlockSpec((None, ts, LANES), lambda i, c: (i, c, 0)),
            const((SSM_CONV, CONV_DIM)),
            const((1, CONV_DIM)),
            const((1, LANES)),
            const((1, LANES)),
            const((1, SSM_INNER)),
            const((1, SSM_INNER)),
            const((lc, lc)),
            const((2 * LANES, SSM_INNER)),
        ],
        out_specs=pl.BlockSpec((None, ts, SSM_INNER), lambda i, c: (i, c, 0)),
        out_shape=jax.ShapeDtypeStruct((b, s, SSM_INNER), BF16),
        scratch_shapes=[
            pltpu.VMEM((CONV_DIM // LANES, lc + SUBLANES, LANES), F32),
            pltpu.VMEM((SSM_GROUPS, SSM_STATE, SSM_GROUP_WIDTH), F32),
        ],
        compiler_params=_params(2, 40),
        name="ssd",
    )(proj, proj, dt_raw, conv_w.astype(F32), conv_b.reshape(1, CONV_DIM).astype(F32), pad_heads(dt_bias),
      pad_heads(a_log), jnp.repeat(d_skip.astype(F32), SSM_HEAD_DIM).reshape(1, SSM_INNER),
      ssm_norm.reshape(1, SSM_INNER).astype(F32), tri, ex)


def _mem_kv_kernel(m_ref, norm_ref, w_ref, o_ref):
    o_ref[...] = _dot(_rms(m_ref[...], norm_ref[...]).astype(BF16), w_ref[...]).astype(o_ref.dtype)


def _mem_kv(mem, kv_norm, w_mkv):
    b, n, d = mem.shape
    return pl.pallas_call(
        _mem_kv_kernel,
        grid=(b,),
        in_specs=[
            pl.BlockSpec((None, n, d), lambda i: (i, 0, 0)),
            pl.BlockSpec((1, d), lambda i: (0, 0)),
            pl.BlockSpec((d, 2 * MEM_WIDTH), lambda i: (0, 0)),
        ],
        out_specs=pl.BlockSpec((None, n, 2 * MEM_WIDTH), lambda i: (i, 0, 0)),
        out_shape=jax.ShapeDtypeStruct((b, n, 2 * MEM_WIDTH), BF16),
        compiler_params=_params(1, 32),
        name="mem_kv",
    )(mem, kv_norm, w_mkv)


def _mix_mem_kernel(sb_ref, y_ref, h_ref, kv_ref, sbn_ref, woa_ref, wob_ref, mixpost_ref, mempre_ref, wmq_ref,
                    wmo_ref, mempost_ref, o_ref, *, scale, n_chunks, mc):
    chunks = [slice(m * mc, (m + 1) * mc) for m in range(n_chunks)]
    sbn = [_rms(sb_ref[rows, :], sbn_ref[...]).astype(BF16) for rows in chunks]
    mix = [_dot(sbn[m], woa_ref[...]) + _dot(y_ref[rows, :], wob_ref[...]) for m, rows in enumerate(chunks)]
    h2 = [h_ref[rows, :] + _rms(mix[m], mixpost_ref[...]) for m, rows in enumerate(chunks)]
    u = [_rms(h2[m], mempre_ref[...]).astype(BF16) for m in range(n_chunks)]
    q = [_dot(u[m], wmq_ref[...]) for m in range(n_chunks)]
    att = []
    for m in range(n_chunks):
        heads = []
        for hd in range(MEM_HEADS):
            sl = slice(hd * MEM_HEAD_DIM, (hd + 1) * MEM_HEAD_DIM)
            k_h = kv_ref[:, sl]
            v_h = kv_ref[:, MEM_WIDTH + hd * MEM_HEAD_DIM:MEM_WIDTH + (hd + 1) * MEM_HEAD_DIM]
            sc = _dot_nt(q[m][:, sl].astype(BF16), k_h) * scale
            e = jnp.exp(sc - jnp.max(sc, axis=-1, keepdims=True))
            p = e / jnp.sum(e, axis=-1, keepdims=True)
            heads.append(_dot(p.astype(BF16), v_h))
        att.append(jnp.concatenate(heads, axis=1).astype(BF16))
    mo = [_dot(att[m], wmo_ref[...]) for m in range(n_chunks)]
    for m, rows in enumerate(chunks):
        o_ref[rows, :] = h2[m] + _rms(mo[m], mempost_ref[...])


def _mix_mem(sb, y, h, kv, sb_norm, wo_sb, wo_y, mix_post, mem_pre, w_mq, w_mo, mem_post, *, tm, mc, seq):
    t, d = h.shape
    n_mem = kv.shape[1]
    steps_per_batch = seq // tm
    const = lambda shape: pl.BlockSpec(shape, lambda i: (0,) * len(shape), pipeline_mode=pl.Buffered(1))
    return pl.pallas_call(
        functools.partial(_mix_mem_kernel, scale=MEM_HEAD_DIM ** -0.5, n_chunks=tm // mc, mc=mc),
        grid=(t // tm,),
        in_specs=[
            pl.BlockSpec((tm, SB_WIDTH), lambda i: (i, 0)),
            pl.BlockSpec((tm, SSM_INNER), lambda i: (i, 0)),
            pl.BlockSpec((tm, d), lambda i: (i, 0)),
            pl.BlockSpec((None, n_mem, 2 * MEM_WIDTH), lambda i: (i // steps_per_batch, 0, 0)),
            const((1, SB_WIDTH)),
            const((SB_WIDTH, d)),
            const((SSM_INNER, d)),
            const((1, d)),
            const((1, d)),
            const((d, MEM_WIDTH)),
            const((MEM_WIDTH, d)),
            const((1, d)),
        ],
        out_specs=pl.BlockSpec((tm, d), lambda i: (i, 0)),
        out_shape=jax.ShapeDtypeStruct((t, d), F32),
        compiler_params=_params(1, 52),
        name="mix_mem",
    )(sb, y, h, kv, sb_norm, wo_sb, wo_y, mix_post, mem_pre, w_mq, w_mo, mem_post)


FFN_TM = 1024
FFN_TF = 512
FFN_MC = MXU_DIM
PROJ_TM = 1024
PROJ_TN = 6 * MXU_DIM
SB_TQ = 512
SB_NH = 1
SSD_CHUNK = 128
SSD_CHUNKS_PER_STEP = 2
MIX_TM = 512
MIX_MC = MXU_DIM


def _row(v):
    return v.reshape(1, -1).astype(F32)


def _layer(h, mem, p):
    b, s, d = h.shape
    t = b * s

    def ffn(hh, pre, wg, wu, wd, post):
        tm = min(FFN_TM, t)
        return _ffn(hh, _row(pre), wg.astype(BF16), wu.astype(BF16), wd.astype(BF16), _row(post), tm=tm,
                    tf=FFN_TF, mc=min(FFN_MC, tm))

    h1 = ffn(h.reshape(t, d), p["ffn1_pre"], p["ffn1_wg"], p["ffn1_wu"], p["ffn1_wd"], p["ffn1_post"])

    w_in = p["w_in"]
    w_dt = jnp.pad(w_in[:, PROJ_WIDTH:], ((0, 0), (0, LANES - SSM_HEADS))).astype(BF16)
    proj, dt_raw = _in_proj(h1, _row(p["mix_pre"]), w_in.astype(BF16), w_dt, n=PROJ_WIDTH, tm=min(PROJ_TM, t),
                            tn=PROJ_TN)
    proj = proj.reshape(b, s, PROJ_WIDTH)
    sb = _sb_attention(proj, tq=min(SB_TQ, s), nh=SB_NH)
    y = _ssd(proj, dt_raw.reshape(b, s, LANES), p["conv_w"], p["conv_b"], p["dt_bias"], p["a_log"], p["d_skip"],
             p["ssm_norm"], lc=min(SSD_CHUNK, s), n_chunks=SSD_CHUNKS_PER_STEP)

    kv = _mem_kv(mem, _row(p["mem_kv_norm"]), p["w_mkv"].astype(BF16))
    w_out = p["w_out"].astype(BF16)
    h3 = _mix_mem(sb.reshape(t, SB_WIDTH), y.reshape(t, SSM_INNER), h1, kv, _row(p["sb_norm"]),
                  w_out[:SB_WIDTH], w_out[SB_WIDTH:], _row(p["mix_post"]), _row(p["mem_pre"]),
                  p["w_mq"].astype(BF16), p["w_mo"].astype(BF16), _row(p["mem_post"]), tm=min(MIX_TM, s),
                  mc=min(MIX_MC, s), seq=s)

    out = ffn(h3, p["ffn2_pre"], p["ffn2_wg"], p["ffn2_wu"], p["ffn2_wd"], p["ffn2_post"])
    return out.reshape(b, s, d)


def kernel(x, mem, ffn1_pre, ffn1_wg, ffn1_wu, ffn1_wd, ffn1_post, mix_pre, w_in, conv_w, conv_b, dt_bias, a_log,
           d_skip, sb_norm, ssm_norm, w_out, mix_post, mem_pre, mem_kv_norm, w_mq, w_mkv, w_mo, mem_post, ffn2_pre,
           ffn2_wg, ffn2_wu, ffn2_wd, ffn2_post):
    params = dict(
        ffn1_pre=ffn1_pre, ffn1_wg=ffn1_wg, ffn1_wu=ffn1_wu, ffn1_wd=ffn1_wd, ffn1_post=ffn1_post,
        mix_pre=mix_pre, w_in=w_in, conv_w=conv_w, conv_b=conv_b, dt_bias=dt_bias, a_log=a_log, d_skip=d_skip,
        sb_norm=sb_norm, ssm_norm=ssm_norm, w_out=w_out, mix_post=mix_post, mem_pre=mem_pre,
        mem_kv_norm=mem_kv_norm, w_mq=w_mq, w_mkv=w_mkv, w_mo=w_mo, mem_post=mem_post,
        ffn2_pre=ffn2_pre, ffn2_wg=ffn2_wg, ffn2_wu=ffn2_wu, ffn2_wd=ffn2_wd, ffn2_post=ffn2_post)
    h = x
    for layer in range(ffn1_pre.shape[0]):
        h = _layer(h, mem, {k: v[layer] for k, v in params.items()})
    return h
```

```python
import functools

import jax
import jax.numpy as jnp
from jax import lax
from jax.experimental import pallas as pl
from jax.experimental.pallas import tpu as pltpu

F32 = jnp.float32
BF16 = jnp.bfloat16

EPS = 1e-6
LOG2E = 1.4426950408889634
LANES = 128
SUBLANES = 8
MXU_DIM = 256
VMEM_BYTES_V7X = 64 * 1024 * 1024
MIB = 1024 * 1024

SB_HEADS = 8
SB_HEAD_DIM = 128
SB_WIDTH = SB_HEADS * SB_HEAD_DIM
SSM_HEADS = 16
SSM_HEAD_DIM = 64
SSM_INNER = SSM_HEADS * SSM_HEAD_DIM
SSM_GROUPS = 4
SSM_HEADS_PER_GROUP = SSM_HEADS // SSM_GROUPS
SSM_STATE = 128
SSM_CONV = 4
SSM_GROUP_WIDTH = SSM_HEADS_PER_GROUP * SSM_HEAD_DIM
CONV_DIM = SSM_INNER + 2 * SSM_GROUPS * SSM_STATE
MEM_HEADS = 4
MEM_HEAD_DIM = 128
MEM_WIDTH = MEM_HEADS * MEM_HEAD_DIM

COL_Q = 0
COL_K = SB_WIDTH
COL_V = 2 * SB_WIDTH
COL_Z = 3 * SB_WIDTH
COL_XBC = COL_Z + SSM_INNER
PROJ_WIDTH = COL_XBC + CONV_DIM


def _params(n_grid_axes, vmem_mib):
    return pltpu.CompilerParams(
        dimension_semantics=("arbitrary",) * n_grid_axes,
        vmem_limit_bytes=min(vmem_mib * MIB, VMEM_BYTES_V7X - 4 * MIB),
    )


def _rms(x, w):
    ms = jnp.mean(x * x, axis=-1, keepdims=True)
    return x * lax.rsqrt(ms + EPS) * w


def _silu(x):
    return x * jax.nn.sigmoid(x)


def _softplus(x):
    return jnp.maximum(x, 0.0) + jnp.log1p(jnp.exp(-jnp.abs(x)))


def _split_bf16(x):
    hi = x.astype(BF16)
    lo = (x - hi.astype(F32)).astype(BF16)
    return hi, lo


def _dot(a, b):
    return jnp.dot(a, b, preferred_element_type=F32)


def _dot_nt(a, b):
    return lax.dot_general(a, b, (((1,), (1,)), ((), ())), preferred_element_type=F32)


def _ffn_kernel(x_ref, pre_ref, wg_ref, wu_ref, wd_ref, wg_hbm, wu_hbm, wd_hbm, post_ref, o_ref, u_ref, wgt_ref,
                wut_ref, wdt_ref, sem, *, n_main, f_main, ft, n_chunks, mc):
    i = pl.program_id(0)
    j = pl.program_id(1)

    def tail_copies():
        return (pltpu.make_async_copy(wg_hbm.at[:, pl.ds(f_main, ft)], wgt_ref, sem.at[0]),
                pltpu.make_async_copy(wu_hbm.at[:, pl.ds(f_main, ft)], wut_ref, sem.at[1]),
                pltpu.make_async_copy(wd_hbm.at[pl.ds(f_main, ft), :], wdt_ref, sem.at[2]))

    @pl.when(jnp.logical_and(i == 0, j == 0))
    def _():
        for copy in tail_copies():
            copy.start()

    @pl.when(jnp.logical_and(i == 0, j == n_main))
    def _():
        for copy in tail_copies():
            copy.wait()

    def step(first, last):
        wg, wu, wd = (wgt_ref, wut_ref, wdt_ref) if last else (wg_ref, wu_ref, wd_ref)
        for m in range(n_chunks):
            rows = slice(m * mc, (m + 1) * mc)
            if first:
                u = _rms(x_ref[rows, :], pre_ref[...]).astype(BF16)
                u_ref[rows, :] = u
            else:
                u = u_ref[rows, :]
            act = (_silu(_dot(u, wg[...])) * _dot(u, wu[...])).astype(BF16)
            y = _dot(act, wd[...])
            if not first:
                y = o_ref[rows, :] + y
            if last:
                y = x_ref[rows, :] + 0.5 * _rms(y, post_ref[...])
            o_ref[rows, :] = y

    pl.when(j == 0)(functools.partial(step, True, False))
    pl.when(jnp.logical_and(j > 0, j < n_main))(functools.partial(step, False, False))
    pl.when(j == n_main)(functools.partial(step, False, True))


def _ffn(h, pre, wg, wu, wd, post, *, tm, tf, mc):
    t, d = h.shape
    f = wg.shape[1]
    n_main = (f - 1) // tf
    f_main = n_main * tf
    ft = f - f_main
    assert n_main >= 1 and ft % LANES == 0 and tm % mc == 0
    vmem = (4 * tm * d * 4 + tm * d * 2 + 2 * 3 * d * tf * 2 + 3 * d * ft * 2 + 4 * mc * tf * 4
            + 2 * mc * d * 4) // MIB + 4
    main_col = lambda i, j: (0, jnp.minimum(j, n_main - 1))
    main_row = lambda i, j: (jnp.minimum(j, n_main - 1), 0)
    const = lambda shape: pl.BlockSpec(shape, lambda i, j: (0, 0), pipeline_mode=pl.Buffered(1))
    hbm = pl.BlockSpec(memory_space=pl.ANY)
    return pl.pallas_call(
        functools.partial(_ffn_kernel, n_main=n_main, f_main=f_main, ft=ft, n_chunks=tm // mc, mc=mc),
        grid=(t // tm, n_main + 1),
        in_specs=[
            pl.BlockSpec((tm, d), lambda i, j: (i, 0)),
            const((1, d)),
            pl.BlockSpec((d, tf), main_col),
            pl.BlockSpec((d, tf), main_col),
            pl.BlockSpec((tf, d), main_row),
            hbm,
            hbm,
            hbm,
            const((1, d)),
        ],
        out_specs=pl.BlockSpec((tm, d), lambda i, j: (i, 0)),
        out_shape=jax.ShapeDtypeStruct((t, d), F32),
        scratch_shapes=[
            pltpu.VMEM((tm, d), BF16),
            pltpu.VMEM((d, ft), BF16),
            pltpu.VMEM((d, ft), BF16),
            pltpu.VMEM((ft, d), BF16),
            pltpu.SemaphoreType.DMA((3,)),
        ],
        compiler_params=_params(2, vmem),
        name="ffn",
    )(h, pre, wg, wu, wd, wg, wu, wd, post)


def _in_proj_kernel(x_ref, pre_ref, w_ref, wdt_ref, o_ref, odt_ref, u_ref, *, n_chunks, mc):
    j = pl.program_id(1)

    @pl.when(j == 0)
    def _():
        for m in range(n_chunks):
            rows = slice(m * mc, (m + 1) * mc)
            u = _rms(x_ref[rows, :], pre_ref[...]).astype(BF16)
            u_ref[rows, :] = u
            o_ref[rows, :] = _dot(u, w_ref[...])
            odt_ref[rows, :] = _dot(u, wdt_ref[...])

    @pl.when(j > 0)
    def _():
        o_ref[...] = _dot(u_ref[...], w_ref[...])


def _in_proj(h, pre, w, w_dt, *, n, tm, tn, mc):
    t, d = h.shape
    assert n % tn == 0 and n <= w.shape[1] and w_dt.shape == (d, LANES) and tm % mc == 0
    vmem = (2 * tm * d * 4 + tm * d * 2 + 2 * d * tn * 2 + 3 * tm * tn * 4) // MIB + 8
    return pl.pallas_call(
        functools.partial(_in_proj_kernel, n_chunks=tm // mc, mc=mc),
        grid=(t // tm, n // tn),
        in_specs=[
            pl.BlockSpec((tm, d), lambda i, j: (i, 0)),
            pl.BlockSpec((1, d), lambda i, j: (0, 0)),
            pl.BlockSpec((d, tn), lambda i, j: (0, j)),
            pl.BlockSpec((d, LANES), lambda i, j: (0, 0)),
        ],
        out_specs=[
            pl.BlockSpec((tm, tn), lambda i, j: (i, j)),
            pl.BlockSpec((tm, LANES), lambda i, j: (i, 0)),
        ],
        out_shape=[jax.ShapeDtypeStruct((t, n), F32), jax.ShapeDtypeStruct((t, LANES), F32)],
        scratch_shapes=[pltpu.VMEM((tm, d), BF16)],
        compiler_params=_params(2, vmem),
        name="in_proj",
    )(h, pre, w, w_dt)


def _suffix_sum_matrix():
    j = jnp.arange(LANES)[:, None]
    s = jnp.arange(LANES)[None, :]
    half = jnp.concatenate([(j > s), jnp.ones((LANES, LANES), bool)], axis=1)
    return jnp.concatenate([half, half], axis=0).astype(BF16)


def _sb_kernel(q_ref, k_ref, v_ref, w_ref, *rest, n_cast, seq, tq, nh, scale):
    cast_in, (o_ref, *cast_out) = rest[:n_cast], rest[n_cast:2 * n_cast + 1]
    qs_ref, ks_ref, vs_ref, acc_ref, r_ref = rest[2 * n_cast + 1:]
    for src, dst in zip(cast_in, cast_out):
        dst[...] = src[...].astype(BF16)
    qs_ref[...] = q_ref[...].astype(BF16)
    ks_ref[...] = k_ref[...].astype(BF16)
    vs_ref[...] = v_ref[...].astype(BF16)
    w = w_ref[...]
    n_sub = tq // LANES
    heads = [slice(h * SB_HEAD_DIM, (h + 1) * SB_HEAD_DIM) for h in range(nh)]
    def qk(q0, k0):
        return [_dot_nt(qs_ref[q0:q0 + tq, hs], ks_ref[k0:k0 + tq, hs]) for hs in heads]

    def scores(d, c, diag):
        row0 = c * LANES if diag else 0
        mask = None
        if diag:
            mask = (lax.broadcasted_iota(jnp.int32, (tq - row0, LANES), 1)
                    < lax.broadcasted_iota(jnp.int32, (tq - row0, LANES), 0))
        per_head = []
        for dh_full in d:
            dh = dh_full[row0:, c * LANES:(c + 1) * LANES]
            z = dh * scale
            e = jnp.log(1.0 + jnp.exp2(jnp.abs(dh) * (-scale * LOG2E)))
            lb = jnp.minimum(z, 0.0) - e
            lk = lb - z
            per_head.append((lb, lk if mask is None else jnp.where(mask, lk, 0.0)))
        return row0, mask, per_head

    def accumulate(sc, a_parts, c):
        row0, mask, per_head = sc
        rows = slice(row0, tq)
        for h, hs in enumerate(heads):
            lb, lk = per_head[h]
            hi, lo = _split_bf16(lk)
            er = _dot(jnp.concatenate([hi, lo], axis=1), w)
            r = r_ref[rows, hs]
            a = jnp.exp(lb + er[:, :LANES] + r)
            r_ref[rows, hs] = r + er[:, LANES:]
            if mask is not None:
                a = jnp.where(mask, a, 0.0)
            a = a.astype(BF16)
            if row0:
                a = jnp.concatenate([jnp.zeros((row0, LANES), BF16), a], axis=0)
            a_parts[h][c] = a

    n_q = seq // tq
    chunks = [(qi * tq, kj * tq) for qi in range(n_q) for kj in range(qi, -1, -1)]
    d = qk(*chunks[0])
    sc = [scores(d, c, True) for c in range(n_sub)]
    for n, (q0, k0) in enumerate(chunks):
        if k0 == q0:
            acc_ref[...] = jnp.zeros_like(acc_ref)
            r_ref[...] = jnp.zeros_like(r_ref)
        nxt = chunks[n + 1] if n + 1 < len(chunks) else None
        if nxt is not None:
            d = qk(*nxt)
        sc_next = [None] * n_sub
        a_parts = [[None] * n_sub for _ in heads]
        for c in reversed(range(n_sub)):
            accumulate(sc[c], a_parts, c)
            if nxt is not None:
                sc_next[c] = scores(d, c, nxt[0] == nxt[1])
        for h, hs in enumerate(heads):
            acc_ref[:, hs] += _dot(jnp.concatenate(a_parts[h], axis=1), vs_ref[k0:k0 + tq, hs])
        if k0 == 0:
            o_ref[q0:q0 + tq, :] = acc_ref[...]
        sc = sc_next


def _sb_attention(proj, to_cast, *, tq, nh):
    b, s, _ = proj.shape
    wd = nh * SB_HEAD_DIM
    col0 = (COL_Q // wd, COL_K // wd, COL_V // wd)
    n_heads_steps = SB_HEADS // nh
    n_steps = b * n_heads_steps
    cast_specs = []
    for a in to_cast:
        rows = a.shape[0]
        pack = 2 * SUBLANES
        br = -(-rows // (n_steps * pack)) * pack
        while rows % br:
            br += pack
        last = rows // br - 1
        cast_specs.append(pl.BlockSpec(
            (br, a.shape[1]), lambda i, h, last=last: (jnp.minimum(i * n_heads_steps + h, last), 0)))
    outs = pl.pallas_call(
        functools.partial(_sb_kernel, n_cast=len(to_cast), seq=s, tq=tq, nh=nh, scale=SB_HEAD_DIM ** -0.5),
        grid=(b, n_heads_steps),
        in_specs=[
            pl.BlockSpec((None, s, wd), lambda i, h: (i, 0, col0[0] + h)),
            pl.BlockSpec((None, s, wd), lambda i, h: (i, 0, col0[1] + h)),
            pl.BlockSpec((None, s, wd), lambda i, h: (i, 0, col0[2] + h)),
            pl.BlockSpec((2 * LANES, 2 * LANES), lambda i, h: (0, 0)),
            *cast_specs,
        ],
        out_specs=[pl.BlockSpec((None, s, wd), lambda i, h: (i, 0, h)), *cast_specs],
        out_shape=[jax.ShapeDtypeStruct((b, s, SB_WIDTH), F32),
                   *[jax.ShapeDtypeStruct(a.shape, BF16) for a in to_cast]],
        scratch_shapes=[
            pltpu.VMEM((s, wd), BF16),
            pltpu.VMEM((s, wd), BF16),
            pltpu.VMEM((s, wd), BF16),
            pltpu.VMEM((tq, wd), F32),
            pltpu.VMEM((tq, nh * LANES), F32),
        ],
        compiler_params=_params(2, 40),
        name="sb_attn",
    )(proj, proj, proj, _suffix_sum_matrix(), *to_cast)
    return outs[0], outs[1:]


def _ssd_kernel(z_ref, xbc_ref, dt_ref, convw_ref, convb_ref, dtb_ref, alog_ref, dskip_ref, norm_ref,
                tri_ref, ex_ref, o_ref, xpad_ref, state_ref, *, lc, n_chunks):
    @pl.when(pl.program_id(1) == 0)
    def _():
        state_ref[...] = jnp.zeros_like(state_ref)
        xpad_ref[:, 0:SUBLANES, :] = jnp.zeros((CONV_DIM // LANES, SUBLANES, LANES), F32)

    ri = lax.broadcasted_iota(jnp.int32, (lc, lc), 0)
    ci = lax.broadcasted_iota(jnp.int32, (lc, lc), 1)
    tril = ri >= ci
    head_of_lane = lax.broadcasted_iota(jnp.int32, (1, SSM_GROUP_WIDTH), 1) // SSM_HEAD_DIM
    n_x, n_b = SSM_INNER // LANES, SSM_GROUPS * SSM_STATE // LANES

    for sub in range(n_chunks):
        rows = slice(sub * lc, (sub + 1) * lc)
        act = []
        for c in range(CONV_DIM // LANES):
            cols = slice(c * LANES, (c + 1) * LANES)
            xpad_ref[c, SUBLANES:SUBLANES + lc, :] = xbc_ref[rows, cols]
            conv = convb_ref[:, cols]
            for k in range(SSM_CONV):
                shift = SUBLANES - (SSM_CONV - 1) + k
                conv = conv + convw_ref[k:k + 1, cols] * xpad_ref[c, shift:shift + lc, :]
            xpad_ref[c, 0:SUBLANES, :] = xpad_ref[c, lc:lc + SUBLANES, :]
            act.append(_silu(conv))
        xs = jnp.concatenate(act[:n_x], axis=1)
        bm = jnp.concatenate(act[n_x:n_x + n_b], axis=1)
        cm = jnp.concatenate(act[n_x + n_b:], axis=1)

        dt = _softplus(dt_ref[rows, :] + dtb_ref[...])
        da = dt * (-jnp.exp(alog_ref[...]))
        da_hi, da_lo = _split_bf16(da)
        cs2 = _dot(tri_ref[...], jnp.concatenate([da_hi, da_lo], axis=1))
        cs = cs2[:, :LANES] + cs2[:, LANES:]
        cs_t = cs.T

        ex = ex_ref[...]
        dt_x = _dot(jnp.concatenate(_split_bf16(dt), axis=1), ex)
        cs_x = _dot(jnp.concatenate(_split_bf16(cs), axis=1), ex)
        cs_last = cs_x[lc - 1:lc, :]
        xd = xs * dt_x
        xd_bf = xd.astype(BF16)
        xd_end = (xd * jnp.exp(cs_last - cs_x)).astype(BF16)
        in_decay = jnp.exp(cs_x)
        chunk_decay = jnp.exp(cs_last)

        ys = []
        for g in range(SSM_GROUPS):
            gl = slice(g * SSM_GROUP_WIDTH, (g + 1) * SSM_GROUP_WIDTH)
            sl = slice(g * SSM_STATE, (g + 1) * SSM_STATE)
            bm_g = bm[:, sl]
            cm_g = cm[:, sl].astype(BF16)
            cb = _dot_nt(cm_g, bm_g.astype(BF16))
            xd_g = xd_bf[:, gl]
            y_diag = jnp.zeros((lc, SSM_GROUP_WIDTH), F32)
            for r in range(SSM_HEADS_PER_GROUP):
                h = g * SSM_HEADS_PER_GROUP + r
                seg = cs[:, h:h + 1] - cs_t[h:h + 1, :]
                decay = jnp.exp(jnp.where(tril, seg, -jnp.inf))
                x_h = jnp.where(head_of_lane == r, xd_g, jnp.zeros_like(xd_g))
                y_diag = y_diag + _dot((cb * decay).astype(BF16), x_h)
            state = state_ref[g]
            y_off = _dot(cm_g, state.astype(BF16)) * in_decay[:, gl]
            state_ref[g] = state * chunk_decay[:, gl] + _dot(bm_g.T.astype(BF16), xd_end[:, gl])
            ys.append(y_diag + y_off)
        y = jnp.concatenate(ys, axis=1) + dskip_ref[...] * xs
        y = y * _silu(z_ref[rows, :])
        o_ref[rows, :] = _rms(y, norm_ref[...]).astype(o_ref.dtype)


def _ssd(proj, dt_raw, conv_w, conv_b, dt_bias, a_log, d_skip, ssm_norm, *, lc, n_chunks):
    b, s, _ = proj.shape
    ts = lc * n_chunks
    pad_heads = lambda v: jnp.pad(v.astype(F32), (0, LANES - SSM_HEADS)).reshape(1, LANES)
    tri = jnp.tril(jnp.ones((lc, lc), BF16))
    lane_head = jnp.arange(SSM_INNER)[None, :] // SSM_HEAD_DIM
    ex_half = (jnp.arange(LANES)[:, None] == lane_head).astype(BF16)
    ex = jnp.concatenate([ex_half, ex_half], axis=0)
    const = lambda shape: pl.BlockSpec(shape, lambda i, c: (0,) * len(shape))
    return pl.pallas_call(
        functools.partial(_ssd_kernel, lc=lc, n_chunks=n_chunks),
        grid=(b, s // ts),
        in_specs=[
            pl.BlockSpec((None, ts, SSM_INNER), lambda i, c: (i, c, COL_Z // SSM_INNER)),
            pl.BlockSpec((None, ts, CONV_DIM), lambda i, c: (i, c, COL_XBC // CONV_DIM)),
            pl.BlockSpec((None, ts, LANES), lambda i, c: (i, c, 0)),
            const((SSM_CONV, CONV_DIM)),
            const((1, CONV_DIM)),
            const((1, LANES)),
            const((1, LANES)),
            const((1, SSM_INNER)),
            const((1, SSM_INNER)),
            const((lc, lc)),
            const((2 * LANES, SSM_INNER)),
        ],
        out_specs=pl.BlockSpec((None, ts, SSM_INNER), lambda i, c: (i, c, 0)),
        out_shape=jax.ShapeDtypeStruct((b, s, SSM_INNER), BF16),
        scratch_shapes=[
            pltpu.VMEM((CONV_DIM // LANES, lc + SUBLANES, LANES), F32),
            pltpu.VMEM((SSM_GROUPS, SSM_STATE, SSM_GROUP_WIDTH), F32),
        ],
        compiler_params=_params(2, 40),
        name="ssd",
    )(proj, proj, dt_raw, conv_w.astype(F32), conv_b.reshape(1, CONV_DIM).astype(F32), pad_heads(dt_bias),
      pad_heads(a_log), jnp.repeat(d_skip.astype(F32), SSM_HEAD_DIM).reshape(1, SSM_INNER),
      ssm_norm.reshape(1, SSM_INNER).astype(F32), tri, ex)


def _mem_kv_kernel(m_ref, norm_ref, w_ref, o_ref):
    o_ref[...] = _dot(_rms(m_ref[...], norm_ref[...]).astype(BF16), w_ref[...]).astype(o_ref.dtype)


def _mem_kv(mem, kv_norm, w_mkv):
    b, n, d = mem.shape
    return pl.pallas_call(
        _mem_kv_kernel,
        grid=(b,),
        in_specs=[
            pl.BlockSpec((None, n, d), lambda i: (i, 0, 0)),
            pl.BlockSpec((1, d), lambda i: (0, 0)),
            pl.BlockSpec((d, 2 * MEM_WIDTH), lambda i: (0, 0)),
        ],
        out_specs=pl.BlockSpec((None, n, 2 * MEM_WIDTH), lambda i: (i, 0, 0)),
        out_shape=jax.ShapeDtypeStruct((b, n, 2 * MEM_WIDTH), BF16),
        compiler_params=_params(1, 32),
        name="mem_kv",
    )(mem, kv_norm, w_mkv)


def _mix_mem_kernel(sb_ref, y_ref, h_ref, kv_ref, sbn_ref, woa_ref, wob_ref, mixpost_ref, mempre_ref, wmq_ref,
                    wmo_ref, mempost_ref, o_ref, *, scale, n_chunks, mc):
    chunks = [slice(m * mc, (m + 1) * mc) for m in range(n_chunks)]
    sbn = [_rms(sb_ref[rows, :], sbn_ref[...]).astype(BF16) for rows in chunks]
    mix = [_dot(sbn[m], woa_ref[...]) + _dot(y_ref[rows, :], wob_ref[...]) for m, rows in enumerate(chunks)]
    h2 = [h_ref[rows, :] + _rms(mix[m], mixpost_ref[...]) for m, rows in enumerate(chunks)]
    u = [_rms(h2[m], mempre_ref[...]).astype(BF16) for m in range(n_chunks)]
    q = [_dot(u[m], wmq_ref[...]) for m in range(n_chunks)]
    att = []
    for m in range(n_chunks):
        heads = []
        for hd in range(MEM_HEADS):
            sl = slice(hd * MEM_HEAD_DIM, (hd + 1) * MEM_HEAD_DIM)
            k_h = kv_ref[:, sl]
            v_h = kv_ref[:, MEM_WIDTH + hd * MEM_HEAD_DIM:MEM_WIDTH + (hd + 1) * MEM_HEAD_DIM]
            sc = _dot_nt(q[m][:, sl].astype(BF16), k_h) * scale
            e = jnp.exp(sc - jnp.max(sc, axis=-1, keepdims=True))
            p = e / jnp.sum(e, axis=-1, keepdims=True)
            heads.append(_dot(p.astype(BF16), v_h))
        att.append(jnp.concatenate(heads, axis=1).astype(BF16))
    mo = [_dot(att[m], wmo_ref[...]) for m in range(n_chunks)]
    for m, rows in enumerate(chunks):
        o_ref[rows, :] = h2[m] + _rms(mo[m], mempost_ref[...])


def _mix_mem(sb, y, h, kv, sb_norm, w_out, mix_post, mem_pre, w_mq, w_mo, mem_post, *, tm, mc, seq):
    t, d = h.shape
    n_mem = kv.shape[1]
    steps_per_batch = seq // tm
    assert SB_WIDTH == SSM_INNER and w_out.shape == (SB_WIDTH + SSM_INNER, d)
    const = lambda shape, idx=None: pl.BlockSpec(shape, lambda i: idx or (0,) * len(shape),
                                                 pipeline_mode=pl.Buffered(1))
    return pl.pallas_call(
        functools.partial(_mix_mem_kernel, scale=MEM_HEAD_DIM ** -0.5, n_chunks=tm // mc, mc=mc),
        grid=(t // tm,),
        in_specs=[
            pl.BlockSpec((tm, SB_WIDTH), lambda i: (i, 0)),
            pl.BlockSpec((tm, SSM_INNER), lambda i: (i, 0)),
            pl.BlockSpec((tm, d), lambda i: (i, 0)),
            pl.BlockSpec((None, n_mem, 2 * MEM_WIDTH), lambda i: (i // steps_per_batch, 0, 0)),
            const((1, SB_WIDTH)),
            const((SB_WIDTH, d), (0, 0)),
            const((SSM_INNER, d), (1, 0)),
            const((1, d)),
            const((1, d)),
            const((d, MEM_WIDTH)),
            const((MEM_WIDTH, d)),
            const((1, d)),
        ],
        out_specs=pl.BlockSpec((tm, d), lambda i: (i, 0)),
        out_shape=jax.ShapeDtypeStruct((t, d), F32),
        compiler_params=_params(1, 52),
        name="mix_mem",
    )(sb, y, h, kv, sb_norm, w_out, w_out, mix_post, mem_pre, w_mq, w_mo, mem_post)


FFN_TM = 1024
FFN_TF = 512
FFN_MC = MXU_DIM
PROJ_TM = 1024
PROJ_TN = 6 * MXU_DIM
PROJ_MC = MXU_DIM
SB_TQ = 512
SB_NH = 1
SSD_CHUNK = 128
SSD_CHUNKS_PER_STEP = 2
MIX_TM = 512
MIX_MC = MXU_DIM


def _row(v):
    return v.reshape(1, -1).astype(F32)


def _layer(h, mem, p):
    b, s, d = h.shape
    t = b * s

    def ffn(hh, pre, wg, wu, wd, post):
        tm = min(FFN_TM, t)
        return _ffn(hh, _row(pre), wg, wu, wd, _row(post), tm=tm, tf=FFN_TF, mc=min(FFN_MC, tm))

    h1 = ffn(h.reshape(t, d), p["ffn1_pre"], p["ffn1_wg"].astype(BF16), p["ffn1_wu"].astype(BF16),
             p["ffn1_wd"].astype(BF16), p["ffn1_post"])

    w_in = p["w_in"]
    w_dt = jnp.pad(w_in[:, PROJ_WIDTH:], ((0, 0), (0, LANES - SSM_HEADS))).astype(BF16)
    proj, dt_raw = _in_proj(h1, _row(p["mix_pre"]), w_in.astype(BF16), w_dt, n=PROJ_WIDTH, tm=min(PROJ_TM, t),
                            tn=PROJ_TN, mc=min(PROJ_MC, t))
    proj = proj.reshape(b, s, PROJ_WIDTH)
    later = (p["ffn2_wg"], p["ffn2_wu"], p["ffn2_wd"], p["w_out"], p["w_mq"], p["w_mo"])
    sb, (wg2, wu2, wd2, w_out, w_mq, w_mo) = _sb_attention(proj, later, tq=min(SB_TQ, s), nh=SB_NH)
    y = _ssd(proj, dt_raw.reshape(b, s, LANES), p["conv_w"], p["conv_b"], p["dt_bias"], p["a_log"], p["d_skip"],
             p["ssm_norm"], lc=min(SSD_CHUNK, s), n_chunks=SSD_CHUNKS_PER_STEP)

    kv = _mem_kv(mem, _row(p["mem_kv_norm"]), p["w_mkv"].astype(BF16))
    h3 = _mix_mem(sb.reshape(t, SB_WIDTH), y.reshape(t, SSM_INNER), h1, kv, _row(p["sb_norm"]), w_out,
                  _row(p["mix_post"]), _row(p["mem_pre"]), w_mq, w_mo, _row(p["mem_post"]), tm=min(MIX_TM, s),
                  mc=min(MIX_MC, s), seq=s)

    out = ffn(h3, p["ffn2_pre"], wg2, wu2, wd2, p["ffn2_post"])
    return out.reshape(b, s, d)


def kernel(x, mem, ffn1_pre, ffn1_wg, ffn1_wu, ffn1_wd, ffn1_post, mix_pre, w_in, conv_w, conv_b, dt_bias, a_log,
           d_skip, sb_norm, ssm_norm, w_out, mix_post, mem_pre, mem_kv_norm, w_mq, w_mkv, w_mo, mem_post, ffn2_pre,
           ffn2_wg, ffn2_wu, ffn2_wd, ffn2_post):
    params = dict(
        ffn1_pre=ffn1_pre, ffn1_wg=ffn1_wg, ffn1_wu=ffn1_wu, ffn1_wd=ffn1_wd, ffn1_post=ffn1_post,
        mix_pre=mix_pre, w_in=w_in, conv_w=conv_w, conv_b=conv_b, dt_bias=dt_bias, a_log=a_log, d_skip=d_skip,
        sb_norm=sb_norm, ssm_norm=ssm_norm, w_out=w_out, mix_post=mix_post, mem_pre=mem_pre,
        mem_kv_norm=mem_kv_norm, w_mq=w_mq, w_mkv=w_mkv, w_mo=w_mo, mem_post=mem_post,
        ffn2_pre=ffn2_pre, ffn2_wg=ffn2_wg, ffn2_wu=ffn2_wu, ffn2_wd=ffn2_wd, ffn2_post=ffn2_post)
    h = x
    for layer in range(ffn1_pre.shape[0]):
        h = _layer(h, mem, {k: v[layer] for k, v in params.items()})
    return h
```

```python
import functools

import jax
import jax.numpy as jnp
from jax import lax
from jax.experimental import pallas as pl
from jax.experimental.pallas import tpu as pltpu

F32 = jnp.float32
BF16 = jnp.bfloat16

EPS = 1e-6
LOG2E = 1.4426950408889634
LANES = 128
SUBLANES = 8
MXU_DIM = 256
VMEM_BYTES_V7X = 64 * 1024 * 1024
MIB = 1024 * 1024

SB_HEADS = 8
SB_HEAD_DIM = 128
SB_WIDTH = SB_HEADS * SB_HEAD_DIM
SSM_HEADS = 16
SSM_HEAD_DIM = 64
SSM_INNER = SSM_HEADS * SSM_HEAD_DIM
SSM_GROUPS = 4
SSM_HEADS_PER_GROUP = SSM_HEADS // SSM_GROUPS
SSM_STATE = 128
SSM_CONV = 4
SSM_GROUP_WIDTH = SSM_HEADS_PER_GROUP * SSM_HEAD_DIM
CONV_DIM = SSM_INNER + 2 * SSM_GROUPS * SSM_STATE
MEM_HEADS = 4
MEM_HEAD_DIM = 128
MEM_WIDTH = MEM_HEADS * MEM_HEAD_DIM

COL_Q = 0
COL_K = SB_WIDTH
COL_V = 2 * SB_WIDTH
COL_Z = 3 * SB_WIDTH
COL_XBC = COL_Z + SSM_INNER
PROJ_WIDTH = COL_XBC + CONV_DIM


def _params(n_grid_axes, vmem_mib):
    return pltpu.CompilerParams(
        dimension_semantics=("arbitrary",) * n_grid_axes,
        vmem_limit_bytes=min(vmem_mib * MIB, VMEM_BYTES_V7X - 4 * MIB),
    )


def _rms(x, w):
    ms = jnp.mean(x * x, axis=-1, keepdims=True)
    return x * lax.rsqrt(ms + EPS) * w


def _silu(x):
    return x * jax.nn.sigmoid(x)


def _softplus(x):
    return jnp.maximum(x, 0.0) + jnp.log1p(jnp.exp(-jnp.abs(x)))


def _split_bf16(x):
    hi = x.astype(BF16)
    lo = (x - hi.astype(F32)).astype(BF16)
    return hi, lo


def _dot(a, b):
    return jnp.dot(a, b, preferred_element_type=F32)


def _dot_nt(a, b):
    return lax.dot_general(a, b, (((1,), (1,)), ((), ())), preferred_element_type=F32)


def _row_block_specs(arrays, n_steps, step_of):
    specs = []
    for a in arrays:
        rows = a.shape[0]
        pack = 2 * SUBLANES
        br = -(-rows // (n_steps * pack)) * pack
        while rows % br:
            br += pack
        last = rows // br - 1
        specs.append(pl.BlockSpec(
            (br, a.shape[1]), lambda *ids, last=last: (jnp.minimum(step_of(*ids), last), 0)))
    return specs


def _ffn_kernel(x_ref, pre_ref, wg_ref, wu_ref, wd_ref, wg_hbm, wu_hbm, wd_hbm, post_ref, o_ref, u_ref, wgt_ref,
                wut_ref, wdt_ref, sem, *, n_main, f_main, ft, n_chunks, mc):
    i = pl.program_id(0)
    j = pl.program_id(1)

    def tail_copies():
        return (pltpu.make_async_copy(wg_hbm.at[:, pl.ds(f_main, ft)], wgt_ref, sem.at[0]),
                pltpu.make_async_copy(wu_hbm.at[:, pl.ds(f_main, ft)], wut_ref, sem.at[1]),
                pltpu.make_async_copy(wd_hbm.at[pl.ds(f_main, ft), :], wdt_ref, sem.at[2]))

    @pl.when(jnp.logical_and(i == 0, j == 0))
    def _():
        for copy in tail_copies():
            copy.start()

    @pl.when(jnp.logical_and(i == 0, j == n_main))
    def _():
        for copy in tail_copies():
            copy.wait()

    def step(first, last):
        wg, wu, wd = (wgt_ref, wut_ref, wdt_ref) if last else (wg_ref, wu_ref, wd_ref)
        for m in range(n_chunks):
            rows = slice(m * mc, (m + 1) * mc)
            if first:
                u = _rms(x_ref[rows, :], pre_ref[...]).astype(BF16)
                u_ref[rows, :] = u
            else:
                u = u_ref[rows, :]
            act = (_silu(_dot(u, wg[...])) * _dot(u, wu[...])).astype(BF16)
            y = _dot(act, wd[...])
            if not first:
                y = o_ref[rows, :] + y
            if last:
                y = x_ref[rows, :] + 0.5 * _rms(y, post_ref[...])
            o_ref[rows, :] = y

    pl.when(j == 0)(functools.partial(step, True, False))
    pl.when(jnp.logical_and(j > 0, j < n_main))(functools.partial(step, False, False))
    pl.when(j == n_main)(functools.partial(step, False, True))


def _ffn(h, pre, wg, wu, wd, post, *, tm, tf, mc):
    t, d = h.shape
    f = wg.shape[1]
    n_main = (f - 1) // tf
    f_main = n_main * tf
    ft = f - f_main
    assert n_main >= 1 and ft % LANES == 0 and tm % mc == 0
    vmem = (4 * tm * d * 4 + tm * d * 2 + 2 * 3 * d * tf * 2 + 3 * d * ft * 2 + 4 * mc * tf * 4
            + 2 * mc * d * 4) // MIB + 4
    main_col = lambda i, j: (0, jnp.minimum(j, n_main - 1))
    main_row = lambda i, j: (jnp.minimum(j, n_main - 1), 0)
    const = lambda shape: pl.BlockSpec(shape, lambda i, j: (0, 0), pipeline_mode=pl.Buffered(1))
    hbm = pl.BlockSpec(memory_space=pl.ANY)
    return pl.pallas_call(
        functools.partial(_ffn_kernel, n_main=n_main, f_main=f_main, ft=ft, n_chunks=tm // mc, mc=mc),
        grid=(t // tm, n_main + 1),
        in_specs=[
            pl.BlockSpec((tm, d), lambda i, j: (i, 0)),
            const((1, d)),
            pl.BlockSpec((d, tf), main_col),
            pl.BlockSpec((d, tf), main_col),
            pl.BlockSpec((tf, d), main_row),
            hbm,
            hbm,
            hbm,
            const((1, d)),
        ],
        out_specs=pl.BlockSpec((tm, d), lambda i, j: (i, 0)),
        out_shape=jax.ShapeDtypeStruct((t, d), F32),
        scratch_shapes=[
            pltpu.VMEM((tm, d), BF16),
            pltpu.VMEM((d, ft), BF16),
            pltpu.VMEM((d, ft), BF16),
            pltpu.VMEM((ft, d), BF16),
            pltpu.SemaphoreType.DMA((3,)),
        ],
        compiler_params=_params(2, vmem),
        name="ffn",
    )(h, pre, wg, wu, wd, wg, wu, wd, post)


def _in_proj_kernel(x_ref, pre_ref, w_ref, wdt_ref, o_ref, odt_ref, u_ref, *, n_chunks, mc):
    j = pl.program_id(1)

    @pl.when(j == 0)
    def _():
        for m in range(n_chunks):
            rows = slice(m * mc, (m + 1) * mc)
            u = _rms(x_ref[rows, :], pre_ref[...]).astype(BF16)
            u_ref[rows, :] = u
            o_ref[rows, :] = _dot(u, w_ref[...])
            odt_ref[rows, :] = _dot(u, wdt_ref[...])

    @pl.when(j > 0)
    def _():
        o_ref[...] = _dot(u_ref[...], w_ref[...])


def _in_proj(h, pre, w, w_dt, *, n, tm, tn, mc):
    t, d = h.shape
    assert n % tn == 0 and n <= w.shape[1] and w_dt.shape == (d, LANES) and tm % mc == 0
    vmem = (2 * tm * d * 4 + tm * d * 2 + 2 * d * tn * 2 + 3 * tm * tn * 4) // MIB + 8
    return pl.pallas_call(
        functools.partial(_in_proj_kernel, n_chunks=tm // mc, mc=mc),
        grid=(t // tm, n // tn),
        in_specs=[
            pl.BlockSpec((tm, d), lambda i, j: (i, 0)),
            pl.BlockSpec((1, d), lambda i, j: (0, 0)),
            pl.BlockSpec((d, tn), lambda i, j: (0, j)),
            pl.BlockSpec((d, LANES), lambda i, j: (0, 0)),
        ],
        out_specs=[
            pl.BlockSpec((tm, tn), lambda i, j: (i, j)),
            pl.BlockSpec((tm, LANES), lambda i, j: (i, 0)),
        ],
        out_shape=[jax.ShapeDtypeStruct((t, n), F32), jax.ShapeDtypeStruct((t, LANES), F32)],
        scratch_shapes=[pltpu.VMEM((tm, d), BF16)],
        compiler_params=_params(2, vmem),
        name="in_proj",
    )(h, pre, w, w_dt)


def _suffix_sum_matrix():
    j = jnp.arange(LANES)[:, None]
    s = jnp.arange(LANES)[None, :]
    half = jnp.concatenate([(j > s), jnp.ones((LANES, LANES), bool)], axis=1)
    return jnp.concatenate([half, half], axis=0).astype(BF16)


def _sb_kernel(q_ref, k_ref, v_ref, w_ref, *rest, n_cast, seq, tq, nh, scale):
    cast_in, (o_ref, *cast_out) = rest[:n_cast], rest[n_cast:2 * n_cast + 1]
    qs_ref, ks_ref, vs_ref, acc_ref, r_ref = rest[2 * n_cast + 1:]
    for src, dst in zip(cast_in, cast_out):
        dst[...] = src[...].astype(BF16)
    qs_ref[...] = q_ref[...].astype(BF16)
    ks_ref[...] = k_ref[...].astype(BF16)
    vs_ref[...] = v_ref[...].astype(BF16)
    w = w_ref[...]
    n_sub = tq // LANES
    heads = [slice(h * SB_HEAD_DIM, (h + 1) * SB_HEAD_DIM) for h in range(nh)]
    def qk(q0, k0):
        return [_dot_nt(qs_ref[q0:q0 + tq, hs], ks_ref[k0:k0 + tq, hs]) for hs in heads]

    def scores(d, c, diag):
        row0 = c * LANES if diag else 0
        mask = None
        if diag:
            mask = (lax.broadcasted_iota(jnp.int32, (tq - row0, LANES), 1)
                    < lax.broadcasted_iota(jnp.int32, (tq - row0, LANES), 0))
        per_head = []
        for dh_full in d:
            dh = dh_full[row0:, c * LANES:(c + 1) * LANES]
            z = dh * scale
            e = jnp.log(1.0 + jnp.exp2(jnp.abs(dh) * (-scale * LOG2E)))
            lb = jnp.minimum(z, 0.0) - e
            lk = lb - z
            per_head.append((lb, lk if mask is None else jnp.where(mask, lk, 0.0)))
        return row0, mask, per_head

    def accumulate(sc, a_parts, c):
        row0, mask, per_head = sc
        rows = slice(row0, tq)
        for h, hs in enumerate(heads):
            lb, lk = per_head[h]
            hi, lo = _split_bf16(lk)
            er = _dot(jnp.concatenate([hi, lo], axis=1), w)
            r = r_ref[rows, hs]
            a = jnp.exp(lb + er[:, :LANES] + r)
            r_ref[rows, hs] = r + er[:, LANES:]
            if mask is not None:
                a = jnp.where(mask, a, 0.0)
            a = a.astype(BF16)
            if row0:
                a = jnp.concatenate([jnp.zeros((row0, LANES), BF16), a], axis=0)
            a_parts[h][c] = a

    n_q = seq // tq
    chunks = [(qi * tq, kj * tq) for qi in range(n_q) for kj in range(qi, -1, -1)]
    d = qk(*chunks[0])
    sc = [scores(d, c, True) for c in range(n_sub)]
    for n, (q0, k0) in enumerate(chunks):
        if k0 == q0:
            acc_ref[...] = jnp.zeros_like(acc_ref)
            r_ref[...] = jnp.zeros_like(r_ref)
        nxt = chunks[n + 1] if n + 1 < len(chunks) else None
        if nxt is not None:
            d = qk(*nxt)
        sc_next = [None] * n_sub
        a_parts = [[None] * n_sub for _ in heads]
        for c in reversed(range(n_sub)):
            accumulate(sc[c], a_parts, c)
            if nxt is not None:
                sc_next[c] = scores(d, c, nxt[0] == nxt[1])
        for h, hs in enumerate(heads):
            acc_ref[:, hs] += _dot(jnp.concatenate(a_parts[h], axis=1), vs_ref[k0:k0 + tq, hs])
        if k0 == 0:
            o_ref[q0:q0 + tq, :] = acc_ref[...]
        sc = sc_next


def _sb_attention(proj, to_cast, *, tq, nh):
    b, s, _ = proj.shape
    wd = nh * SB_HEAD_DIM
    col0 = (COL_Q // wd, COL_K // wd, COL_V // wd)
    n_heads_steps = SB_HEADS // nh
    cast_specs = _row_block_specs(to_cast, b * n_heads_steps, lambda i, h: i * n_heads_steps + h)
    outs = pl.pallas_call(
        functools.partial(_sb_kernel, n_cast=len(to_cast), seq=s, tq=tq, nh=nh, scale=SB_HEAD_DIM ** -0.5),
        grid=(b, n_heads_steps),
        in_specs=[
            pl.BlockSpec((None, s, wd), lambda i, h: (i, 0, col0[0] + h)),
            pl.BlockSpec((None, s, wd), lambda i, h: (i, 0, col0[1] + h)),
            pl.BlockSpec((None, s, wd), lambda i, h: (i, 0, col0[2] + h)),
            pl.BlockSpec((2 * LANES, 2 * LANES), lambda i, h: (0, 0)),
            *cast_specs,
        ],
        out_specs=[pl.BlockSpec((None, s, wd), lambda i, h: (i, 0, h)), *cast_specs],
        out_shape=[jax.ShapeDtypeStruct((b, s, SB_WIDTH), F32),
                   *[jax.ShapeDtypeStruct(a.shape, BF16) for a in to_cast]],
        scratch_shapes=[
            pltpu.VMEM((s, wd), BF16),
            pltpu.VMEM((s, wd), BF16),
            pltpu.VMEM((s, wd), BF16),
            pltpu.VMEM((tq, wd), F32),
            pltpu.VMEM((tq, nh * LANES), F32),
        ],
        compiler_params=_params(2, 40),
        name="sb_attn",
    )(proj, proj, proj, _suffix_sum_matrix(), *to_cast)
    return outs[0], outs[1:]


def _ssd_kernel(z_ref, xbc_ref, dt_ref, convw_ref, convb_ref, dtb_ref, alog_ref, dskip_ref, norm_ref,
                tri_ref, ex_ref, o_ref, xpad_ref, state_ref, *, lc, n_chunks):
    @pl.when(pl.program_id(1) == 0)
    def _():
        state_ref[...] = jnp.zeros_like(state_ref)
        xpad_ref[:, 0:SUBLANES, :] = jnp.zeros((CONV_DIM // LANES, SUBLANES, LANES), F32)

    ri = lax.broadcasted_iota(jnp.int32, (lc, lc), 0)
    ci = lax.broadcasted_iota(jnp.int32, (lc, lc), 1)
    tril = ri >= ci
    head_of_lane = lax.broadcasted_iota(jnp.int32, (1, SSM_GROUP_WIDTH), 1) // SSM_HEAD_DIM
    n_x, n_b = SSM_INNER // LANES, SSM_GROUPS * SSM_STATE // LANES

    for sub in range(n_chunks):
        rows = slice(sub * lc, (sub + 1) * lc)
        act = []
        for c in range(CONV_DIM // LANES):
            cols = slice(c * LANES, (c + 1) * LANES)
            xpad_ref[c, SUBLANES:SUBLANES + lc, :] = xbc_ref[rows, cols]
            conv = convb_ref[:, cols]
            for k in range(SSM_CONV):
                shift = SUBLANES - (SSM_CONV - 1) + k
                conv = conv + convw_ref[k:k + 1, cols] * xpad_ref[c, shift:shift + lc, :]
            xpad_ref[c, 0:SUBLANES, :] = xpad_ref[c, lc:lc + SUBLANES, :]
            act.append(_silu(conv))
        xs = jnp.concatenate(act[:n_x], axis=1)
        bm = jnp.concatenate(act[n_x:n_x + n_b], axis=1)
        cm = jnp.concatenate(act[n_x + n_b:], axis=1)

        dt = _softplus(dt_ref[rows, :] + dtb_ref[...])
        da = dt * (-jnp.exp(alog_ref[...]))
        da_hi, da_lo = _split_bf16(da)
        cs2 = _dot(tri_ref[...], jnp.concatenate([da_hi, da_lo], axis=1))
        cs = cs2[:, :LANES] + cs2[:, LANES:]
        cs_t = cs.T

        ex = ex_ref[...]
        dt_x = _dot(jnp.concatenate(_split_bf16(dt), axis=1), ex)
        cs_x = _dot(jnp.concatenate(_split_bf16(cs), axis=1), ex)
        cs_last = cs_x[lc - 1:lc, :]
        xd = xs * dt_x
        xd_bf = xd.astype(BF16)
        xd_end = (xd * jnp.exp(cs_last - cs_x)).astype(BF16)
        in_decay = jnp.exp(cs_x)
        chunk_decay = jnp.exp(cs_last)

        ys = []
        for g in range(SSM_GROUPS):
            gl = slice(g * SSM_GROUP_WIDTH, (g + 1) * SSM_GROUP_WIDTH)
            sl = slice(g * SSM_STATE, (g + 1) * SSM_STATE)
            bm_g = bm[:, sl]
            cm_g = cm[:, sl].astype(BF16)
            cb = _dot_nt(cm_g, bm_g.astype(BF16))
            xd_g = xd_bf[:, gl]
            y_diag = jnp.zeros((lc, SSM_GROUP_WIDTH), F32)
            for r in range(SSM_HEADS_PER_GROUP):
                h = g * SSM_HEADS_PER_GROUP + r
                seg = cs[:, h:h + 1] - cs_t[h:h + 1, :]
                decay = jnp.exp(jnp.where(tril, seg, -jnp.inf))
                x_h = jnp.where(head_of_lane == r, xd_g, jnp.zeros_like(xd_g))
                y_diag = y_diag + _dot((cb * decay).astype(BF16), x_h)
            state = state_ref[g]
            y_off = _dot(cm_g, state.astype(BF16)) * in_decay[:, gl]
            state_ref[g] = state * chunk_decay[:, gl] + _dot(bm_g.T.astype(BF16), xd_end[:, gl])
            ys.append(y_diag + y_off)
        y = jnp.concatenate(ys, axis=1) + dskip_ref[...] * xs
        y = y * _silu(z_ref[rows, :])
        o_ref[rows, :] = _rms(y, norm_ref[...]).astype(o_ref.dtype)


def _ssd(proj, dt_raw, conv_w, conv_b, dt_bias, a_log, d_skip, ssm_norm, *, lc, n_chunks):
    b, s, _ = proj.shape
    ts = lc * n_chunks
    pad_heads = lambda v: jnp.pad(v.astype(F32), (0, LANES - SSM_HEADS)).reshape(1, LANES)
    tri = jnp.tril(jnp.ones((lc, lc), BF16))
    lane_head = jnp.arange(SSM_INNER)[None, :] // SSM_HEAD_DIM
    ex_half = (jnp.arange(LANES)[:, None] == lane_head).astype(BF16)
    ex = jnp.concatenate([ex_half, ex_half], axis=0)
    const = lambda shape: pl.BlockSpec(shape, lambda i, c: (0,) * len(shape))
    return pl.pallas_call(
        functools.partial(_ssd_kernel, lc=lc, n_chunks=n_chunks),
        grid=(b, s // ts),
        in_specs=[
            pl.BlockSpec((None, ts, SSM_INNER), lambda i, c: (i, c, COL_Z // SSM_INNER)),
            pl.BlockSpec((None, ts, CONV_DIM), lambda i, c: (i, c, COL_XBC // CONV_DIM)),
            pl.BlockSpec((None, ts, LANES), lambda i, c: (i, c, 0)),
            const((SSM_CONV, CONV_DIM)),
            const((1, CONV_DIM)),
            const((1, LANES)),
            const((1, LANES)),
            const((1, SSM_INNER)),
            const((1, SSM_INNER)),
            const((lc, lc)),
            const((2 * LANES, SSM_INNER)),
        ],
        out_specs=pl.BlockSpec((None, ts, SSM_INNER), lambda i, c: (i, c, 0)),
        out_shape=jax.ShapeDtypeStruct((b, s, SSM_INNER), BF16),
        scratch_shapes=[
            pltpu.VMEM((CONV_DIM // LANES, lc + SUBLANES, LANES), F32),
            pltpu.VMEM((SSM_GROUPS, SSM_STATE, SSM_GROUP_WIDTH), F32),
        ],
        compiler_params=_params(2, 40),
        name="ssd",
    )(proj, proj, dt_raw, conv_w.astype(F32), conv_b.reshape(1, CONV_DIM).astype(F32), pad_heads(dt_bias),
      pad_heads(a_log), jnp.repeat(d_skip.astype(F32), SSM_HEAD_DIM).reshape(1, SSM_INNER),
      ssm_norm.reshape(1, SSM_INNER).astype(F32), tri, ex)


def _mem_kv_kernel(m_ref, norm_ref, w_ref, *rest, n_cast):
    cast_in, (o_ref, *cast_out) = rest[:n_cast], rest[n_cast:]
    for src, dst in zip(cast_in, cast_out):
        dst[...] = src[...].astype(BF16)
    o_ref[...] = _dot(_rms(m_ref[...], norm_ref[...]).astype(BF16), w_ref[...]).astype(o_ref.dtype)


def _mem_kv(mem, kv_norm, w_mkv, to_cast):
    b, n, d = mem.shape
    cast_specs = _row_block_specs(to_cast, b, lambda i: i)
    outs = pl.pallas_call(
        functools.partial(_mem_kv_kernel, n_cast=len(to_cast)),
        grid=(b,),
        in_specs=[
            pl.BlockSpec((None, n, d), lambda i: (i, 0, 0)),
            pl.BlockSpec((1, d), lambda i: (0, 0)),
            pl.BlockSpec((d, 2 * MEM_WIDTH), lambda i: (0, 0)),
            *cast_specs,
        ],
        out_specs=[pl.BlockSpec((None, n, 2 * MEM_WIDTH), lambda i: (i, 0, 0)), *cast_specs],
        out_shape=[jax.ShapeDtypeStruct((b, n, 2 * MEM_WIDTH), BF16),
                   *[jax.ShapeDtypeStruct(a.shape, BF16) for a in to_cast]],
        compiler_params=_params(1, 40),
        name="mem_kv",
    )(mem, kv_norm, w_mkv, *to_cast)
    return outs[0], outs[1:]


def _mix_mem_kernel(sb_ref, y_ref, h_ref, kv_ref, sbn_ref, woa_ref, wob_ref, mixpost_ref, mempre_ref, wmq_ref,
                    wmo_ref, mempost_ref, o_ref, *, scale, n_chunks, mc):
    def stage_norm_sb(st, rows):
        st["sbn"] = _rms(sb_ref[rows, :], sbn_ref[...]).astype(BF16)

    def stage_mix(st, rows):
        st["mix"] = _dot(st["sbn"], woa_ref[...]) + _dot(y_ref[rows, :], wob_ref[...])

    def stage_norms(st, rows):
        st["h2"] = h_ref[rows, :] + _rms(st["mix"], mixpost_ref[...])
        st["u"] = _rms(st["h2"], mempre_ref[...]).astype(BF16)

    def stage_q(st, rows):
        st["q"] = _dot(st["u"], wmq_ref[...])

    def stage_attention(st, rows):
        heads = []
        for hd in range(MEM_HEADS):
            sl = slice(hd * MEM_HEAD_DIM, (hd + 1) * MEM_HEAD_DIM)
            k_h = kv_ref[:, sl]
            v_h = kv_ref[:, MEM_WIDTH + hd * MEM_HEAD_DIM:MEM_WIDTH + (hd + 1) * MEM_HEAD_DIM]
            sc = _dot_nt(st["q"][:, sl].astype(BF16), k_h) * scale
            e = jnp.exp(sc - jnp.max(sc, axis=-1, keepdims=True))
            p = e / jnp.sum(e, axis=-1, keepdims=True)
            heads.append(_dot(p.astype(BF16), v_h))
        st["att"] = jnp.concatenate(heads, axis=1).astype(BF16)

    def stage_out(st, rows):
        st["mo"] = _dot(st["att"], wmo_ref[...])

    def stage_final(st, rows):
        o_ref[rows, :] = st["h2"] + _rms(st["mo"], mempost_ref[...])

    stages = (stage_norm_sb, stage_mix, stage_norms, stage_q, stage_attention, stage_out, stage_final)
    states = [{} for _ in range(n_chunks)]
    for tick in range(len(stages) + n_chunks - 1):
        for m in range(n_chunks):
            if 0 <= tick - m < len(stages):
                stages[tick - m](states[m], slice(m * mc, (m + 1) * mc))


def _mix_mem(sb, y, h, kv, sb_norm, w_out, mix_post, mem_pre, w_mq, w_mo, mem_post, *, tm, mc, seq):
    t, d = h.shape
    n_mem = kv.shape[1]
    steps_per_batch = seq // tm
    assert SB_WIDTH == SSM_INNER and w_out.shape == (SB_WIDTH + SSM_INNER, d)
    const = lambda shape, idx=None: pl.BlockSpec(shape, lambda i: idx or (0,) * len(shape),
                                                 pipeline_mode=pl.Buffered(1))
    return pl.pallas_call(
        functools.partial(_mix_mem_kernel, scale=MEM_HEAD_DIM ** -0.5, n_chunks=tm // mc, mc=mc),
        grid=(t // tm,),
        in_specs=[
            pl.BlockSpec((tm, SB_WIDTH), lambda i: (i, 0)),
            pl.BlockSpec((tm, SSM_INNER), lambda i: (i, 0)),
            pl.BlockSpec((tm, d), lambda i: (i, 0)),
            pl.BlockSpec((None, n_mem, 2 * MEM_WIDTH), lambda i: (i // steps_per_batch, 0, 0)),
            const((1, SB_WIDTH)),
            const((SB_WIDTH, d), (0, 0)),
            const((SSM_INNER, d), (1, 0)),
            const((1, d)),
            const((1, d)),
            const((d, MEM_WIDTH)),
            const((MEM_WIDTH, d)),
            const((1, d)),
        ],
        out_specs=pl.BlockSpec((tm, d), lambda i: (i, 0)),
        out_shape=jax.ShapeDtypeStruct((t, d), F32),
        compiler_params=_params(1, 52),
        name="mix_mem",
    )(sb, y, h, kv, sb_norm, w_out, w_out, mix_post, mem_pre, w_mq, w_mo, mem_post)


FFN_TM = 1024
FFN_TF = 512
FFN_MC = 2 * MXU_DIM
PROJ_TM = 1024
PROJ_TN = 6 * MXU_DIM
PROJ_MC = MXU_DIM
SB_TQ = 512
SB_NH = 1
SSD_CHUNK = 128
SSD_CHUNKS_PER_STEP = 2
MIX_TM = 512
MIX_MC = MXU_DIM


def _row(v):
    return v.reshape(1, -1).astype(F32)


def _layer(h, mem, p):
    b, s, d = h.shape
    t = b * s

    def ffn(hh, pre, wg, wu, wd, post):
        tm = min(FFN_TM, t)
        return _ffn(hh, _row(pre), wg, wu, wd, _row(post), tm=tm, tf=FFN_TF, mc=min(FFN_MC, tm))

    h1 = ffn(h.reshape(t, d), p["ffn1_pre"], p["ffn1_wg"].astype(BF16), p["ffn1_wu"].astype(BF16),
             p["ffn1_wd"].astype(BF16), p["ffn1_post"])

    kv, (w_in,) = _mem_kv(mem, _row(p["mem_kv_norm"]), p["w_mkv"].astype(BF16), (p["w_in"],))
    w_dt = jnp.pad(p["w_in"][:, PROJ_WIDTH:], ((0, 0), (0, LANES - SSM_HEADS))).astype(BF16)
    proj, dt_raw = _in_proj(h1, _row(p["mix_pre"]), w_in, w_dt, n=PROJ_WIDTH, tm=min(PROJ_TM, t),
                            tn=PROJ_TN, mc=min(PROJ_MC, t))
    proj = proj.reshape(b, s, PROJ_WIDTH)
    later = (p["ffn2_wg"], p["ffn2_wu"], p["ffn2_wd"], p["w_out"], p["w_mq"], p["w_mo"])
    sb, (wg2, wu2, wd2, w_out, w_mq, w_mo) = _sb_attention(proj, later, tq=min(SB_TQ, s), nh=SB_NH)
    y = _ssd(proj, dt_raw.reshape(b, s, LANES), p["conv_w"], p["conv_b"], p["dt_bias"], p["a_log"], p["d_skip"],
             p["ssm_norm"], lc=min(SSD_CHUNK, s), n_chunks=SSD_CHUNKS_PER_STEP)

    h3 = _mix_mem(sb.reshape(t, SB_WIDTH), y.reshape(t, SSM_INNER), h1, kv, _row(p["sb_norm"]), w_out,
                  _row(p["mix_post"]), _row(p["mem_pre"]), w_mq, w_mo, _row(p["mem_post"]), tm=min(MIX_TM, s),
                  mc=min(MIX_MC, s), seq=s)

    out = ffn(h3, p["ffn2_pre"], wg2, wu2, wd2, p["ffn2_post"])
    return out.reshape(b, s, d)


def kernel(x, mem, ffn1_pre, ffn1_wg, ffn1_wu, ffn1_wd, ffn1_post, mix_pre, w_in, conv_w, conv_b, dt_bias, a_log,
           d_skip, sb_norm, ssm_norm, w_out, mix_post, mem_pre, mem_kv_norm, w_mq, w_mkv, w_mo, mem_post, ffn2_pre,
           ffn2_wg, ffn2_wu, ffn2_wd, ffn2_post):
    params = dict(
        ffn1_pre=ffn1_pre, ffn1_wg=ffn1_wg, ffn1_wu=ffn1_wu, ffn1_wd=ffn1_wd, ffn1_post=ffn1_post,
        mix_pre=mix_pre, w_in=w_in, conv_w=conv_w, conv_b=conv_b, dt_bias=dt_bias, a_log=a_log, d_skip=d_skip,
        sb_norm=sb_norm, ssm_norm=ssm_norm, w_out=w_out, mix_post=mix_post, mem_pre=mem_pre,
        mem_kv_norm=mem_kv_norm, w_mq=w_mq, w_mkv=w_mkv, w_mo=w_mo, mem_post=mem_post,
        ffn2_pre=ffn2_pre, ffn2_wg=ffn2_wg, ffn2_wu=ffn2_wu, ffn2_wd=ffn2_wd, ffn2_post=ffn2_post)
    h = x
    for layer in range(ffn1_pre.shape[0]):
        h = _layer(h, mem, {k: v[layer] for k, v in params.items()})
    return h
```

```python
import functools

import jax
import jax.numpy as jnp
from jax import lax
from jax.experimental import pallas as pl
from jax.experimental.pallas import tpu as pltpu

F32 = jnp.float32
BF16 = jnp.bfloat16

EPS = 1e-6
LOG2E = 1.4426950408889634
LANES = 128
SUBLANES = 8
MXU_DIM = 256
VMEM_BYTES_V7X = 64 * 1024 * 1024
MIB = 1024 * 1024

SB_HEADS = 8
SB_HEAD_DIM = 128
SB_WIDTH = SB_HEADS * SB_HEAD_DIM
SSM_HEADS = 16
SSM_HEAD_DIM = 64
SSM_INNER = SSM_HEADS * SSM_HEAD_DIM
SSM_GROUPS = 4
SSM_HEADS_PER_GROUP = SSM_HEADS // SSM_GROUPS
SSM_STATE = 128
SSM_CONV = 4
SSM_GROUP_WIDTH = SSM_HEADS_PER_GROUP * SSM_HEAD_DIM
CONV_DIM = SSM_INNER + 2 * SSM_GROUPS * SSM_STATE
MEM_HEADS = 4
MEM_HEAD_DIM = 128
MEM_WIDTH = MEM_HEADS * MEM_HEAD_DIM

COL_Q = 0
COL_K = SB_WIDTH
COL_V = 2 * SB_WIDTH
COL_Z = 3 * SB_WIDTH
COL_XBC = COL_Z + SSM_INNER
PROJ_WIDTH = COL_XBC + CONV_DIM


def _params(n_grid_axes, vmem_mib):
    return pltpu.CompilerParams(
        dimension_semantics=("arbitrary",) * n_grid_axes,
        vmem_limit_bytes=min(vmem_mib * MIB, VMEM_BYTES_V7X - 4 * MIB),
    )


def _rms(x, w):
    ms = jnp.mean(x * x, axis=-1, keepdims=True)
    return x * lax.rsqrt(ms + EPS) * w


def _silu(x):
    return x * jax.nn.sigmoid(x)


def _softplus(x):
    return jnp.maximum(x, 0.0) + jnp.log1p(jnp.exp(-jnp.abs(x)))


def _split_bf16(x):
    hi = x.astype(BF16)
    lo = (x - hi.astype(F32)).astype(BF16)
    return hi, lo


def _dot(a, b):
    return jnp.dot(a, b, preferred_element_type=F32)


def _dot_nt(a, b):
    return lax.dot_general(a, b, (((1,), (1,)), ((), ())), preferred_element_type=F32)


def _row_block_specs(arrays, n_steps, step_of):
    specs = []
    for a in arrays:
        rows = a.shape[0]
        pack = 2 * SUBLANES
        br = -(-rows // (n_steps * pack)) * pack
        while rows % br:
            br += pack
        last = rows // br - 1
        specs.append(pl.BlockSpec(
            (br, a.shape[1]), lambda *ids, last=last: (jnp.minimum(step_of(*ids), last), 0)))
    return specs


def _ffn_kernel(x_ref, pre_ref, wg_ref, wu_ref, wd_ref, wg_hbm, wu_hbm, wd_hbm, post_ref, o_ref, u_ref, wgt_ref,
                wut_ref, wdt_ref, sem, *, n_main, f_main, ft, n_chunks, mc):
    i = pl.program_id(0)
    j = pl.program_id(1)

    def tail_copies():
        return (pltpu.make_async_copy(wg_hbm.at[:, pl.ds(f_main, ft)], wgt_ref, sem.at[0]),
                pltpu.make_async_copy(wu_hbm.at[:, pl.ds(f_main, ft)], wut_ref, sem.at[1]),
                pltpu.make_async_copy(wd_hbm.at[pl.ds(f_main, ft), :], wdt_ref, sem.at[2]))

    @pl.when(jnp.logical_and(i == 0, j == 0))
    def _():
        for copy in tail_copies():
            copy.start()

    @pl.when(jnp.logical_and(i == 0, j == n_main))
    def _():
        for copy in tail_copies():
            copy.wait()

    def step(first, last):
        wg, wu, wd = (wgt_ref, wut_ref, wdt_ref) if last else (wg_ref, wu_ref, wd_ref)
        for m in range(n_chunks):
            rows = slice(m * mc, (m + 1) * mc)
            if first:
                u = _rms(x_ref[rows, :], pre_ref[...]).astype(BF16)
                u_ref[rows, :] = u
            else:
                u = u_ref[rows, :]
            act = (_silu(_dot(u, wg[...])) * _dot(u, wu[...])).astype(BF16)
            y = _dot(act, wd[...])
            if not first:
                y = o_ref[rows, :] + y
            if last:
                y = x_ref[rows, :] + 0.5 * _rms(y, post_ref[...])
            o_ref[rows, :] = y

    pl.when(j == 0)(functools.partial(step, True, False))
    pl.when(jnp.logical_and(j > 0, j < n_main))(functools.partial(step, False, False))
    pl.when(j == n_main)(functools.partial(step, False, True))


def _ffn(h, pre, wg, wu, wd, post, *, tm, tf, mc):
    t, d = h.shape
    f = wg.shape[1]
    n_main = (f - 1) // tf
    f_main = n_main * tf
    ft = f - f_main
    assert n_main >= 1 and ft % LANES == 0 and tm % mc == 0
    vmem = (4 * tm * d * 4 + tm * d * 2 + 2 * 3 * d * tf * 2 + 3 * d * ft * 2 + 4 * mc * tf * 4
            + 2 * mc * d * 4) // MIB + 4
    main_col = lambda i, j: (0, jnp.minimum(j, n_main - 1))
    main_row = lambda i, j: (jnp.minimum(j, n_main - 1), 0)
    const = lambda shape: pl.BlockSpec(shape, lambda i, j: (0, 0), pipeline_mode=pl.Buffered(1))
    hbm = pl.BlockSpec(memory_space=pl.ANY)
    return pl.pallas_call(
        functools.partial(_ffn_kernel, n_main=n_main, f_main=f_main, ft=ft, n_chunks=tm // mc, mc=mc),
        grid=(t // tm, n_main + 1),
        in_specs=[
            pl.BlockSpec((tm, d), lambda i, j: (i, 0)),
            const((1, d)),
            pl.BlockSpec((d, tf), main_col),
            pl.BlockSpec((d, tf), main_col),
            pl.BlockSpec((tf, d), main_row),
            hbm,
            hbm,
            hbm,
            const((1, d)),
        ],
        out_specs=pl.BlockSpec((tm, d), lambda i, j: (i, 0)),
        out_shape=jax.ShapeDtypeStruct((t, d), F32),
        scratch_shapes=[
            pltpu.VMEM((tm, d), BF16),
            pltpu.VMEM((d, ft), BF16),
            pltpu.VMEM((d, ft), BF16),
            pltpu.VMEM((ft, d), BF16),
            pltpu.SemaphoreType.DMA((3,)),
        ],
        compiler_params=_params(2, vmem),
        name="ffn",
    )(h, pre, wg, wu, wd, wg, wu, wd, post)


def _in_proj_kernel(x_ref, pre_ref, w_ref, wx_ref, o_ref, ox_ref, u_ref, wxp_ref, *, n_extra, n_chunks, mc):
    j = pl.program_id(1)

    @pl.when(j == 0)
    def _():
        wxp_ref[...] = jnp.zeros_like(wxp_ref)
        wxp_ref[0:n_extra, :] = wx_ref[...]
        for m in range(n_chunks):
            rows = slice(m * mc, (m + 1) * mc)
            u = _rms(x_ref[rows, :], pre_ref[...]).astype(BF16)
            u_ref[rows, :] = u
            o_ref[rows, :] = _dot_nt(u, w_ref[...])
            ox_ref[rows, :] = _dot_nt(u, wxp_ref[...])

    @pl.when(j > 0)
    def _():
        o_ref[...] = _dot_nt(u_ref[...], w_ref[...])


def _in_proj(h, pre, w_t, *, n, tm, tn, mc):
    t, d = h.shape
    n_extra = w_t.shape[0] - n
    assert n % tn == 0 and 0 < n_extra <= LANES and n % n_extra == 0 and n_extra % (2 * SUBLANES) == 0
    assert tm % mc == 0
    vmem = (2 * tm * d * 4 + tm * d * 2 + 2 * d * tn * 2 + 3 * tm * tn * 4) // MIB + 8
    return pl.pallas_call(
        functools.partial(_in_proj_kernel, n_extra=n_extra, n_chunks=tm // mc, mc=mc),
        grid=(t // tm, n // tn),
        in_specs=[
            pl.BlockSpec((tm, d), lambda i, j: (i, 0)),
            pl.BlockSpec((1, d), lambda i, j: (0, 0)),
            pl.BlockSpec((tn, d), lambda i, j: (j, 0)),
            pl.BlockSpec((n_extra, d), lambda i, j: (n // n_extra, 0)),
        ],
        out_specs=[
            pl.BlockSpec((tm, tn), lambda i, j: (i, j)),
            pl.BlockSpec((tm, LANES), lambda i, j: (i, 0)),
        ],
        out_shape=[jax.ShapeDtypeStruct((t, n), F32), jax.ShapeDtypeStruct((t, LANES), F32)],
        scratch_shapes=[pltpu.VMEM((tm, d), BF16), pltpu.VMEM((LANES, d), BF16)],
        compiler_params=_params(2, vmem),
        name="in_proj",
    )(h, pre, w_t, w_t)


def _suffix_sum_matrix():
    j = jnp.arange(LANES)[:, None]
    s = jnp.arange(LANES)[None, :]
    half = jnp.concatenate([(j > s), jnp.ones((LANES, LANES), bool)], axis=1)
    return jnp.concatenate([half, half], axis=0).astype(BF16)


def _sb_kernel(q_ref, k_ref, v_ref, w_ref, *rest, n_cast, seq, tq, nh, scale):
    cast_in, (o_ref, *cast_out) = rest[:n_cast], rest[n_cast:2 * n_cast + 1]
    qs_ref, ks_ref, vs_ref, acc_ref, r_ref = rest[2 * n_cast + 1:]
    for src, dst in zip(cast_in, cast_out):
        dst[...] = src[...].astype(BF16)
    qs_ref[...] = q_ref[...].astype(BF16)
    ks_ref[...] = k_ref[...].astype(BF16)
    vs_ref[...] = v_ref[...].astype(BF16)
    w = w_ref[...]
    n_sub = tq // LANES
    heads = [slice(h * SB_HEAD_DIM, (h + 1) * SB_HEAD_DIM) for h in range(nh)]
    def qk(q0, k0):
        return [_dot_nt(qs_ref[q0:q0 + tq, hs], ks_ref[k0:k0 + tq, hs]) for hs in heads]

    def scores(d, c, diag):
        row0 = c * LANES if diag else 0
        mask = None
        if diag:
            mask = (lax.broadcasted_iota(jnp.int32, (tq - row0, LANES), 1)
                    < lax.broadcasted_iota(jnp.int32, (tq - row0, LANES), 0))
        per_head = []
        for dh_full in d:
            dh = dh_full[row0:, c * LANES:(c + 1) * LANES]
            z = dh * scale
            e = jnp.log(1.0 + jnp.exp2(jnp.abs(dh) * (-scale * LOG2E)))
            lb = jnp.minimum(z, 0.0) - e
            lk = lb - z
            per_head.append((lb, lk if mask is None else jnp.where(mask, lk, 0.0)))
        return row0, mask, per_head

    def accumulate(sc, a_parts, c):
        row0, mask, per_head = sc
        rows = slice(row0, tq)
        for h, hs in enumerate(heads):
            lb, lk = per_head[h]
            hi, lo = _split_bf16(lk)
            er = _dot(jnp.concatenate([hi, lo], axis=1), w)
            r = r_ref[rows, hs]
            a = jnp.exp(lb + er[:, :LANES] + r)
            r_ref[rows, hs] = r + er[:, LANES:]
            if mask is not None:
                a = jnp.where(mask, a, 0.0)
            a = a.astype(BF16)
            if row0:
                a = jnp.concatenate([jnp.zeros((row0, LANES), BF16), a], axis=0)
            a_parts[h][c] = a

    n_q = seq // tq
    chunks = [(qi * tq, kj * tq) for qi in range(n_q) for kj in range(qi, -1, -1)]
    d = qk(*chunks[0])
    sc = [scores(d, c, True) for c in range(n_sub)]
    for n, (q0, k0) in enumerate(chunks):
        if k0 == q0:
            acc_ref[...] = jnp.zeros_like(acc_ref)
            r_ref[...] = jnp.zeros_like(r_ref)
        nxt = chunks[n + 1] if n + 1 < len(chunks) else None
        if nxt is not None:
            d = qk(*nxt)
        sc_next = [None] * n_sub
        a_parts = [[None] * n_sub for _ in heads]
        for c in reversed(range(n_sub)):
            accumulate(sc[c], a_parts, c)
            if nxt is not None:
                sc_next[c] = scores(d, c, nxt[0] == nxt[1])
        for h, hs in enumerate(heads):
            acc_ref[:, hs] += _dot(jnp.concatenate(a_parts[h], axis=1), vs_ref[k0:k0 + tq, hs])
        if k0 == 0:
            o_ref[q0:q0 + tq, :] = acc_ref[...]
        sc = sc_next


def _sb_attention(proj, to_cast, *, tq, nh):
    b, s, _ = proj.shape
    wd = nh * SB_HEAD_DIM
    col0 = (COL_Q // wd, COL_K // wd, COL_V // wd)
    n_heads_steps = SB_HEADS // nh
    cast_specs = _row_block_specs(to_cast, b * n_heads_steps, lambda i, h: i * n_heads_steps + h)
    outs = pl.pallas_call(
        functools.partial(_sb_kernel, n_cast=len(to_cast), seq=s, tq=tq, nh=nh, scale=SB_HEAD_DIM ** -0.5),
        grid=(b, n_heads_steps),
        in_specs=[
            pl.BlockSpec((None, s, wd), lambda i, h: (i, 0, col0[0] + h)),
            pl.BlockSpec((None, s, wd), lambda i, h: (i, 0, col0[1] + h)),
            pl.BlockSpec((None, s, wd), lambda i, h: (i, 0, col0[2] + h)),
            pl.BlockSpec((2 * LANES, 2 * LANES), lambda i, h: (0, 0)),
            *cast_specs,
        ],
        out_specs=[pl.BlockSpec((None, s, wd), lambda i, h: (i, 0, h)), *cast_specs],
        out_shape=[jax.ShapeDtypeStruct((b, s, SB_WIDTH), F32),
                   *[jax.ShapeDtypeStruct(a.shape, BF16) for a in to_cast]],
        scratch_shapes=[
            pltpu.VMEM((s, wd), BF16),
            pltpu.VMEM((s, wd), BF16),
            pltpu.VMEM((s, wd), BF16),
            pltpu.VMEM((tq, wd), F32),
            pltpu.VMEM((tq, nh * LANES), F32),
        ],
        compiler_params=_params(2, 40),
        name="sb_attn",
    )(proj, proj, proj, _suffix_sum_matrix(), *to_cast)
    return outs[0], outs[1:]


def _ssd_kernel(z_ref, xbc_ref, dt_ref, convw_ref, convb_ref, dtb_ref, alog_ref, dskip_ref, norm_ref,
                tri_ref, ex_ref, o_ref, xpad_ref, state_ref, *, lc, n_chunks):
    @pl.when(pl.program_id(1) == 0)
    def _():
        state_ref[...] = jnp.zeros_like(state_ref)
        xpad_ref[:, 0:SUBLANES, :] = jnp.zeros((CONV_DIM // LANES, SUBLANES, LANES), F32)

    ri = lax.broadcasted_iota(jnp.int32, (lc, lc), 0)
    ci = lax.broadcasted_iota(jnp.int32, (lc, lc), 1)
    tril = ri >= ci
    head_of_lane = lax.broadcasted_iota(jnp.int32, (1, SSM_GROUP_WIDTH), 1) // SSM_HEAD_DIM
    n_x, n_b = SSM_INNER // LANES, SSM_GROUPS * SSM_STATE // LANES

    for sub in range(n_chunks):
        rows = slice(sub * lc, (sub + 1) * lc)
        act = []
        for c in range(CONV_DIM // LANES):
            cols = slice(c * LANES, (c + 1) * LANES)
            xpad_ref[c, SUBLANES:SUBLANES + lc, :] = xbc_ref[rows, cols]
            conv = convb_ref[:, cols]
            for k in range(SSM_CONV):
                shift = SUBLANES - (SSM_CONV - 1) + k
                conv = conv + convw_ref[k:k + 1, cols] * xpad_ref[c, shift:shift + lc, :]
            xpad_ref[c, 0:SUBLANES, :] = xpad_ref[c, lc:lc + SUBLANES, :]
            act.append(_silu(conv))
        xs = jnp.concatenate(act[:n_x], axis=1)
        bm = jnp.concatenate(act[n_x:n_x + n_b], axis=1)
        cm = jnp.concatenate(act[n_x + n_b:], axis=1)

        dt = _softplus(dt_ref[rows, :] + dtb_ref[...])
        da = dt * (-jnp.exp(alog_ref[...]))
        da_hi, da_lo = _split_bf16(da)
        cs2 = _dot(tri_ref[...], jnp.concatenate([da_hi, da_lo], axis=1))
        cs = cs2[:, :LANES] + cs2[:, LANES:]
        cs_t = cs.T

        ex = ex_ref[...]
        dt_x = _dot(jnp.concatenate(_split_bf16(dt), axis=1), ex)
        cs_x = _dot(jnp.concatenate(_split_bf16(cs), axis=1), ex)
        cs_last = cs_x[lc - 1:lc, :]
        xd = xs * dt_x
        xd_bf = xd.astype(BF16)
        xd_end = (xd * jnp.exp(cs_last - cs_x)).astype(BF16)
        in_decay = jnp.exp(cs_x)
        chunk_decay = jnp.exp(cs_last)

        ys = []
        for g in range(SSM_GROUPS):
            gl = slice(g * SSM_GROUP_WIDTH, (g + 1) * SSM_GROUP_WIDTH)
            sl = slice(g * SSM_STATE, (g + 1) * SSM_STATE)
            bm_g = bm[:, sl]
            cm_g = cm[:, sl].astype(BF16)
            cb = _dot_nt(cm_g, bm_g.astype(BF16))
            xd_g = xd_bf[:, gl]
            y_diag = jnp.zeros((lc, SSM_GROUP_WIDTH), F32)
            for r in range(SSM_HEADS_PER_GROUP):
                h = g * SSM_HEADS_PER_GROUP + r
                seg = cs[:, h:h + 1] - cs_t[h:h + 1, :]
                decay = jnp.exp(jnp.where(tril, seg, -jnp.inf))
                x_h = jnp.where(head_of_lane == r, xd_g, jnp.zeros_like(xd_g))
                y_diag = y_diag + _dot((cb * decay).astype(BF16), x_h)
            state = state_ref[g]
            y_off = _dot(cm_g, state.astype(BF16)) * in_decay[:, gl]
            state_ref[g] = state * chunk_decay[:, gl] + _dot(bm_g.T.astype(BF16), xd_end[:, gl])
            ys.append(y_diag + y_off)
        y = jnp.concatenate(ys, axis=1) + dskip_ref[...] * xs
        y = y * _silu(z_ref[rows, :])
        o_ref[rows, :] = _rms(y, norm_ref[...]).astype(o_ref.dtype)


def _ssd(proj, dt_raw, conv_w, conv_b, dt_bias, a_log, d_skip, ssm_norm, *, lc, n_chunks):
    b, s, _ = proj.shape
    ts = lc * n_chunks
    pad_heads = lambda v: jnp.pad(v.astype(F32), (0, LANES - SSM_HEADS)).reshape(1, LANES)
    tri = jnp.tril(jnp.ones((lc, lc), BF16))
    lane_head = jnp.arange(SSM_INNER)[None, :] // SSM_HEAD_DIM
    ex_half = (jnp.arange(LANES)[:, None] == lane_head).astype(BF16)
    ex = jnp.concatenate([ex_half, ex_half], axis=0)
    const = lambda shape: pl.BlockSpec(shape, lambda i, c: (0,) * len(shape))
    return pl.pallas_call(
        functools.partial(_ssd_kernel, lc=lc, n_chunks=n_chunks),
        grid=(b, s // ts),
        in_specs=[
            pl.BlockSpec((None, ts, SSM_INNER), lambda i, c: (i, c, COL_Z // SSM_INNER)),
            pl.BlockSpec((None, ts, CONV_DIM), lambda i, c: (i, c, COL_XBC // CONV_DIM)),
            pl.BlockSpec((None, ts, LANES), lambda i, c: (i, c, 0)),
            const((SSM_CONV, CONV_DIM)),
            const((1, CONV_DIM)),
            const((1, LANES)),
            const((1, LANES)),
            const((1, SSM_INNER)),
            const((1, SSM_INNER)),
            const((lc, lc)),
            const((2 * LANES, SSM_INNER)),
        ],
        out_specs=pl.BlockSpec((None, ts, SSM_INNER), lambda i, c: (i, c, 0)),
        out_shape=jax.ShapeDtypeStruct((b, s, SSM_INNER), BF16),
        scratch_shapes=[
            pltpu.VMEM((CONV_DIM // LANES, lc + SUBLANES, LANES), F32),
            pltpu.VMEM((SSM_GROUPS, SSM_STATE, SSM_GROUP_WIDTH), F32),
        ],
        compiler_params=_params(2, 40),
        name="ssd",
    )(proj, proj, dt_raw, conv_w.astype(F32), conv_b.reshape(1, CONV_DIM).astype(F32), pad_heads(dt_bias),
      pad_heads(a_log), jnp.repeat(d_skip.astype(F32), SSM_HEAD_DIM).reshape(1, SSM_INNER),
      ssm_norm.reshape(1, SSM_INNER).astype(F32), tri, ex)


def _mem_kv_kernel(m_ref, norm_ref, w_ref, *rest, n_cast):
    cast_in, (o_ref, *cast_out) = rest[:n_cast], rest[n_cast:]
    for src, dst in zip(cast_in, cast_out):
        dst[...] = src[...].astype(BF16)
    o_ref[...] = _dot(_rms(m_ref[...], norm_ref[...]).astype(BF16), w_ref[...]).astype(o_ref.dtype)


def _mem_kv(mem, kv_norm, w_mkv, to_cast):
    b, n, d = mem.shape
    cast_specs = _row_block_specs(to_cast, b, lambda i: i)
    outs = pl.pallas_call(
        functools.partial(_mem_kv_kernel, n_cast=len(to_cast)),
        grid=(b,),
        in_specs=[
            pl.BlockSpec((None, n, d), lambda i: (i, 0, 0)),
            pl.BlockSpec((1, d), lambda i: (0, 0)),
            pl.BlockSpec((d, 2 * MEM_WIDTH), lambda i: (0, 0)),
            *cast_specs,
        ],
        out_specs=[pl.BlockSpec((None, n, 2 * MEM_WIDTH), lambda i: (i, 0, 0)), *cast_specs],
        out_shape=[jax.ShapeDtypeStruct((b, n, 2 * MEM_WIDTH), BF16),
                   *[jax.ShapeDtypeStruct(a.shape, BF16) for a in to_cast]],
        compiler_params=_params(1, 40),
        name="mem_kv",
    )(mem, kv_norm, w_mkv, *to_cast)
    return outs[0], outs[1:]


def _mix_mem_kernel(sb_ref, y_ref, h_ref, kv_ref, sbn_ref, woa_ref, wob_ref, mixpost_ref, mempre_ref, wmq_ref,
                    wmo_ref, mempost_ref, o_ref, *, scale, n_chunks, mc):
    def stage_norm_sb(st, rows):
        st["sbn"] = _rms(sb_ref[rows, :], sbn_ref[...]).astype(BF16)

    def stage_mix(st, rows):
        st["mix"] = _dot(st["sbn"], woa_ref[...]) + _dot(y_ref[rows, :], wob_ref[...])

    def stage_norms(st, rows):
        st["h2"] = h_ref[rows, :] + _rms(st["mix"], mixpost_ref[...])
        st["u"] = _rms(st["h2"], mempre_ref[...]).astype(BF16)

    def stage_q(st, rows):
        st["q"] = _dot(st["u"], wmq_ref[...])

    def stage_attention(st, rows):
        heads = []
        for hd in range(MEM_HEADS):
            sl = slice(hd * MEM_HEAD_DIM, (hd + 1) * MEM_HEAD_DIM)
            k_h = kv_ref[:, sl]
            v_h = kv_ref[:, MEM_WIDTH + hd * MEM_HEAD_DIM:MEM_WIDTH + (hd + 1) * MEM_HEAD_DIM]
            sc = _dot_nt(st["q"][:, sl].astype(BF16), k_h) * scale
            e = jnp.exp(sc - jnp.max(sc, axis=-1, keepdims=True))
            p = e / jnp.sum(e, axis=-1, keepdims=True)
            heads.append(_dot(p.astype(BF16), v_h))
        st["att"] = jnp.concatenate(heads, axis=1).astype(BF16)

    def stage_out(st, rows):
        st["mo"] = _dot(st["att"], wmo_ref[...])

    def stage_final(st, rows):
        o_ref[rows, :] = st["h2"] + _rms(st["mo"], mempost_ref[...])

    stages = (stage_norm_sb, stage_mix, stage_norms, stage_q, stage_attention, stage_out, stage_final)
    states = [{} for _ in range(n_chunks)]
    for tick in range(len(stages) + n_chunks - 1):
        for m in range(n_chunks):
            if 0 <= tick - m < len(stages):
                stages[tick - m](states[m], slice(m * mc, (m + 1) * mc))


def _mix_mem(sb, y, h, kv, sb_norm, w_out, mix_post, mem_pre, w_mq, w_mo, mem_post, *, tm, mc, seq):
    t, d = h.shape
    n_mem = kv.shape[1]
    steps_per_batch = seq // tm
    assert SB_WIDTH == SSM_INNER and w_out.shape == (SB_WIDTH + SSM_INNER, d)
    const = lambda shape, idx=None: pl.BlockSpec(shape, lambda i: idx or (0,) * len(shape),
                                                 pipeline_mode=pl.Buffered(1))
    return pl.pallas_call(
        functools.partial(_mix_mem_kernel, scale=MEM_HEAD_DIM ** -0.5, n_chunks=tm // mc, mc=mc),
        grid=(t // tm,),
        in_specs=[
            pl.BlockSpec((tm, SB_WIDTH), lambda i: (i, 0)),
            pl.BlockSpec((tm, SSM_INNER), lambda i: (i, 0)),
            pl.BlockSpec((tm, d), lambda i: (i, 0)),
            pl.BlockSpec((None, n_mem, 2 * MEM_WIDTH), lambda i: (i // steps_per_batch, 0, 0)),
            const((1, SB_WIDTH)),
            const((SB_WIDTH, d), (0, 0)),
            const((SSM_INNER, d), (1, 0)),
            const((1, d)),
            const((1, d)),
            const((d, MEM_WIDTH)),
            const((MEM_WIDTH, d)),
            const((1, d)),
        ],
        out_specs=pl.BlockSpec((tm, d), lambda i: (i, 0)),
        out_shape=jax.ShapeDtypeStruct((t, d), F32),
        compiler_params=_params(1, 52),
        name="mix_mem",
    )(sb, y, h, kv, sb_norm, w_out, w_out, mix_post, mem_pre, w_mq, w_mo, mem_post)


FFN_TM = 1024
FFN_TF = 512
FFN_MC = 2 * MXU_DIM
PROJ_TM = 1024
PROJ_TN = 6 * MXU_DIM
PROJ_MC = MXU_DIM
SB_TQ = 512
SB_NH = 1
SSD_CHUNK = 128
SSD_CHUNKS_PER_STEP = 2
MIX_TM = 512
MIX_MC = MXU_DIM


def _row(v):
    return v.reshape(1, -1).astype(F32)


def _layer(h, mem, p):
    b, s, d = h.shape
    t = b * s

    def ffn(hh, pre, wg, wu, wd, post):
        tm = min(FFN_TM, t)
        return _ffn(hh, _row(pre), wg, wu, wd, _row(post), tm=tm, tf=FFN_TF, mc=min(FFN_MC, tm))

    h1 = ffn(h.reshape(t, d), p["ffn1_pre"], p["ffn1_wg"].astype(BF16), p["ffn1_wu"].astype(BF16),
             p["ffn1_wd"].astype(BF16), p["ffn1_post"])

    kv, (w_in_t,) = _mem_kv(mem, _row(p["mem_kv_norm"]), p["w_mkv"].astype(BF16), (p["w_in"].T,))
    proj, dt_raw = _in_proj(h1, _row(p["mix_pre"]), w_in_t, n=PROJ_WIDTH, tm=min(PROJ_TM, t), tn=PROJ_TN,
                            mc=min(PROJ_MC, t))
    proj = proj.reshape(b, s, PROJ_WIDTH)
    later = (p["ffn2_wg"], p["ffn2_wu"], p["ffn2_wd"], p["w_out"], p["w_mq"], p["w_mo"])
    sb, (wg2, wu2, wd2, w_out, w_mq, w_mo) = _sb_attention(proj, later, tq=min(SB_TQ, s), nh=SB_NH)
    y = _ssd(proj, dt_raw.reshape(b, s, LANES), p["conv_w"], p["conv_b"], p["dt_bias"], p["a_log"], p["d_skip"],
             p["ssm_norm"], lc=min(SSD_CHUNK, s), n_chunks=SSD_CHUNKS_PER_STEP)

    h3 = _mix_mem(sb.reshape(t, SB_WIDTH), y.reshape(t, SSM_INNER), h1, kv, _row(p["sb_norm"]), w_out,
                  _row(p["mix_post"]), _row(p["mem_pre"]), w_mq, w_mo, _row(p["mem_post"]), tm=min(MIX_TM, s),
                  mc=min(MIX_MC, s), seq=s)

    out = ffn(h3, p["ffn2_pre"], wg2, wu2, wd2, p["ffn2_post"])
    return out.reshape(b, s, d)


def kernel(x, mem, ffn1_pre, ffn1_wg, ffn1_wu, ffn1_wd, ffn1_post, mix_pre, w_in, conv_w, conv_b, dt_bias, a_log,
           d_skip, sb_norm, ssm_norm, w_out, mix_post, mem_pre, mem_kv_norm, w_mq, w_mkv, w_mo, mem_post, ffn2_pre,
           ffn2_wg, ffn2_wu, ffn2_wd, ffn2_post):
    params = dict(
        ffn1_pre=ffn1_pre, ffn1_wg=ffn1_wg, ffn1_wu=ffn1_wu, ffn1_wd=ffn1_wd, ffn1_post=ffn1_post,
        mix_pre=mix_pre, w_in=w_in, conv_w=conv_w, conv_b=conv_b, dt_bias=dt_bias, a_log=a_log, d_skip=d_skip,
        sb_norm=sb_norm, ssm_norm=ssm_norm, w_out=w_out, mix_post=mix_post, mem_pre=mem_pre,
        mem_kv_norm=mem_kv_norm, w_mq=w_mq, w_mkv=w_mkv, w_mo=w_mo, mem_post=mem_post,
        ffn2_pre=ffn2_pre, ffn2_wg=ffn2_wg, ffn2_wu=ffn2_wu, ffn2_wd=ffn2_wd, ffn2_post=ffn2_post)
    h = x
    for layer in range(ffn1_pre.shape[0]):
        h = _layer(h, mem, {k: v[layer] for k, v in params.items()})
    return h
```

```python
import functools

import jax
import jax.numpy as jnp
from jax import lax
from jax.experimental import pallas as pl
from jax.experimental.pallas import tpu as pltpu

F32 = jnp.float32
BF16 = jnp.bfloat16

EPS = 1e-6
LOG2E = 1.4426950408889634
LANES = 128
SUBLANES = 8
MXU_DIM = 256
VMEM_BYTES_V7X = 64 * 1024 * 1024
MIB = 1024 * 1024

SB_HEADS = 8
SB_HEAD_DIM = 128
SB_WIDTH = SB_HEADS * SB_HEAD_DIM
SSM_HEADS = 16
SSM_HEAD_DIM = 64
SSM_INNER = SSM_HEADS * SSM_HEAD_DIM
SSM_GROUPS = 4
SSM_HEADS_PER_GROUP = SSM_HEADS // SSM_GROUPS
SSM_STATE = 128
SSM_CONV = 4
SSM_GROUP_WIDTH = SSM_HEADS_PER_GROUP * SSM_HEAD_DIM
CONV_DIM = SSM_INNER + 2 * SSM_GROUPS * SSM_STATE
MEM_HEADS = 4
MEM_HEAD_DIM = 128
MEM_WIDTH = MEM_HEADS * MEM_HEAD_DIM

COL_Q = 0
COL_K = SB_WIDTH
COL_V = 2 * SB_WIDTH
COL_Z = 3 * SB_WIDTH
COL_XBC = COL_Z + SSM_INNER
PROJ_WIDTH = COL_XBC + CONV_DIM


def _params(n_grid_axes, vmem_mib):
    return pltpu.CompilerParams(
        dimension_semantics=("arbitrary",) * n_grid_axes,
        vmem_limit_bytes=min(vmem_mib * MIB, VMEM_BYTES_V7X - 4 * MIB),
    )


def _rms(x, w):
    ms = jnp.mean(x * x, axis=-1, keepdims=True)
    return x * lax.rsqrt(ms + EPS) * w


def _silu(x):
    return x * jax.nn.sigmoid(x)


def _softplus(x):
    return jnp.maximum(x, 0.0) + jnp.log1p(jnp.exp(-jnp.abs(x)))


def _split_bf16(x):
    hi = x.astype(BF16)
    lo = (x - hi.astype(F32)).astype(BF16)
    return hi, lo


def _dot(a, b):
    return jnp.dot(a, b, preferred_element_type=F32)


def _dot_nt(a, b):
    return lax.dot_general(a, b, (((1,), (1,)), ((), ())), preferred_element_type=F32)


def _row_block_specs(arrays, n_steps, step_of):
    specs = []
    for a in arrays:
        rows = a.shape[0]
        pack = 2 * SUBLANES
        br = -(-rows // (n_steps * pack)) * pack
        while rows % br:
            br += pack
        last = rows // br - 1
        specs.append(pl.BlockSpec(
            (br, a.shape[1]), lambda *ids, last=last: (jnp.minimum(step_of(*ids), last), 0)))
    return specs


def _ffn_kernel(x_ref, pre_ref, wg_ref, wu_ref, wd_ref, wg_hbm, wu_hbm, wd_hbm, post_ref, o_ref, u_ref, wgt_ref,
                wut_ref, wdt_ref, sem, *, n_main, f_main, ft, n_chunks, mc):
    i = pl.program_id(0)
    j = pl.program_id(1)

    def tail_copies():
        return (pltpu.make_async_copy(wg_hbm.at[:, pl.ds(f_main, ft)], wgt_ref, sem.at[0]),
                pltpu.make_async_copy(wu_hbm.at[:, pl.ds(f_main, ft)], wut_ref, sem.at[1]),
                pltpu.make_async_copy(wd_hbm.at[pl.ds(f_main, ft), :], wdt_ref, sem.at[2]))

    @pl.when(jnp.logical_and(i == 0, j == 0))
    def _():
        for copy in tail_copies():
            copy.start()

    @pl.when(jnp.logical_and(i == 0, j == n_main))
    def _():
        for copy in tail_copies():
            copy.wait()

    def step(first, last):
        wg, wu, wd = (wgt_ref, wut_ref, wdt_ref) if last else (wg_ref, wu_ref, wd_ref)
        chunk = mc if first or last else n_chunks * mc
        for m in range(n_chunks * mc // chunk):
            rows = slice(m * chunk, (m + 1) * chunk)
            if first:
                u = _rms(x_ref[rows, :], pre_ref[...]).astype(BF16)
                u_ref[rows, :] = u
            else:
                u = u_ref[rows, :]
            act = (_silu(_dot(u, wg[...])) * _dot(u, wu[...])).astype(BF16)
            y = _dot(act, wd[...])
            if not first:
                y = o_ref[rows, :] + y
            if last:
                y = x_ref[rows, :] + 0.5 * _rms(y, post_ref[...])
            o_ref[rows, :] = y

    pl.when(j == 0)(functools.partial(step, True, False))
    pl.when(jnp.logical_and(j > 0, j < n_main))(functools.partial(step, False, False))
    pl.when(j == n_main)(functools.partial(step, False, True))


def _ffn(h, pre, wg, wu, wd, post, *, tm, tf, mc):
    t, d = h.shape
    f = wg.shape[1]
    n_main = (f - 1) // tf
    f_main = n_main * tf
    ft = f - f_main
    assert n_main >= 1 and ft % LANES == 0 and tm % mc == 0
    vmem = (4 * tm * d * 4 + tm * d * 2 + 2 * 3 * d * tf * 2 + 3 * d * ft * 2 + 4 * mc * tf * 4
            + 2 * mc * d * 4) // MIB + 4
    main_col = lambda i, j: (0, jnp.minimum(j, n_main - 1))
    main_row = lambda i, j: (jnp.minimum(j, n_main - 1), 0)
    const = lambda shape: pl.BlockSpec(shape, lambda i, j: (0, 0), pipeline_mode=pl.Buffered(1))
    hbm = pl.BlockSpec(memory_space=pl.ANY)
    return pl.pallas_call(
        functools.partial(_ffn_kernel, n_main=n_main, f_main=f_main, ft=ft, n_chunks=tm // mc, mc=mc),
        grid=(t // tm, n_main + 1),
        in_specs=[
            pl.BlockSpec((tm, d), lambda i, j: (i, 0)),
            const((1, d)),
            pl.BlockSpec((d, tf), main_col),
            pl.BlockSpec((d, tf), main_col),
            pl.BlockSpec((tf, d), main_row),
            hbm,
            hbm,
            hbm,
            const((1, d)),
        ],
        out_specs=pl.BlockSpec((tm, d), lambda i, j: (i, 0)),
        out_shape=jax.ShapeDtypeStruct((t, d), F32),
        scratch_shapes=[
            pltpu.VMEM((tm, d), BF16),
            pltpu.VMEM((d, ft), BF16),
            pltpu.VMEM((d, ft), BF16),
            pltpu.VMEM((ft, d), BF16),
            pltpu.SemaphoreType.DMA((3,)),
        ],
        compiler_params=_params(2, vmem),
        name="ffn",
    )(h, pre, wg, wu, wd, wg, wu, wd, post)


def _in_proj_kernel(x_ref, pre_ref, w_ref, wx_ref, o_ref, ox_ref, u_ref, wxp_ref, *, n_extra, n_chunks, mc):
    j = pl.program_id(1)

    @pl.when(j == 0)
    def _():
        wxp_ref[...] = jnp.zeros_like(wxp_ref)
        wxp_ref[0:n_extra, :] = wx_ref[...]
        for m in range(n_chunks):
            rows = slice(m * mc, (m + 1) * mc)
            u = _rms(x_ref[rows, :], pre_ref[...]).astype(BF16)
            u_ref[rows, :] = u
            o_ref[rows, :] = _dot_nt(u, w_ref[...])
            ox_ref[rows, :] = _dot_nt(u, wxp_ref[...])

    @pl.when(j > 0)
    def _():
        o_ref[...] = _dot_nt(u_ref[...], w_ref[...])


def _in_proj(h, pre, w_t, *, n, tm, tn, mc):
    t, d = h.shape
    n_extra = w_t.shape[0] - n
    assert n % tn == 0 and 0 < n_extra <= LANES and n % n_extra == 0 and n_extra % (2 * SUBLANES) == 0
    assert tm % mc == 0
    vmem = (2 * tm * d * 4 + tm * d * 2 + 2 * d * tn * 2 + 3 * tm * tn * 4) // MIB + 8
    return pl.pallas_call(
        functools.partial(_in_proj_kernel, n_extra=n_extra, n_chunks=tm // mc, mc=mc),
        grid=(t // tm, n // tn),
        in_specs=[
            pl.BlockSpec((tm, d), lambda i, j: (i, 0)),
            pl.BlockSpec((1, d), lambda i, j: (0, 0)),
            pl.BlockSpec((tn, d), lambda i, j: (j, 0)),
            pl.BlockSpec((n_extra, d), lambda i, j: (n // n_extra, 0)),
        ],
        out_specs=[
            pl.BlockSpec((tm, tn), lambda i, j: (i, j)),
            pl.BlockSpec((tm, LANES), lambda i, j: (i, 0)),
        ],
        out_shape=[jax.ShapeDtypeStruct((t, n), F32), jax.ShapeDtypeStruct((t, LANES), F32)],
        scratch_shapes=[pltpu.VMEM((tm, d), BF16), pltpu.VMEM((LANES, d), BF16)],
        compiler_params=_params(2, vmem),
        name="in_proj",
    )(h, pre, w_t, w_t)


def _suffix_sum_matrix():
    j = jnp.arange(LANES)[:, None]
    s = jnp.arange(LANES)[None, :]
    half = jnp.concatenate([(j > s), jnp.ones((LANES, LANES), bool)], axis=1)
    return jnp.concatenate([half, half], axis=0).astype(BF16)


def _sb_kernel(q_ref, k_ref, v_ref, w_ref, *rest, n_cast, seq, tq, nh, scale):
    cast_in, (o_ref, *cast_out) = rest[:n_cast], rest[n_cast:2 * n_cast + 1]
    qs_ref, ks_ref, vs_ref, acc_ref, r_ref = rest[2 * n_cast + 1:]
    for src, dst in zip(cast_in, cast_out):
        dst[...] = src[...].astype(BF16)
    qs_ref[...] = q_ref[...].astype(BF16)
    ks_ref[...] = k_ref[...].astype(BF16)
    vs_ref[...] = v_ref[...].astype(BF16)
    w = w_ref[...]
    n_sub = tq // LANES
    heads = [slice(h * SB_HEAD_DIM, (h + 1) * SB_HEAD_DIM) for h in range(nh)]
    def qk(q0, k0):
        return [_dot_nt(qs_ref[q0:q0 + tq, hs], ks_ref[k0:k0 + tq, hs]) for hs in heads]

    def scores(d, c, diag):
        row0 = c * LANES if diag else 0
        mask = None
        if diag:
            mask = (lax.broadcasted_iota(jnp.int32, (tq - row0, LANES), 1)
                    < lax.broadcasted_iota(jnp.int32, (tq - row0, LANES), 0))
        per_head = []
        for dh_full in d:
            dh = dh_full[row0:, c * LANES:(c + 1) * LANES]
            z = dh * scale
            e = jnp.log(1.0 + jnp.exp2(jnp.abs(dh) * (-scale * LOG2E)))
            lb = jnp.minimum(z, 0.0) - e
            lk = lb - z
            per_head.append((lb, lk if mask is None else jnp.where(mask, lk, 0.0)))
        return row0, mask, per_head

    def accumulate(sc, a_parts, c):
        row0, mask, per_head = sc
        rows = slice(row0, tq)
        for h, hs in enumerate(heads):
            lb, lk = per_head[h]
            hi, lo = _split_bf16(lk)
            er = _dot(jnp.concatenate([hi, lo], axis=1), w)
            r = r_ref[rows, hs]
            a = jnp.exp(lb + er[:, :LANES] + r)
            r_ref[rows, hs] = r + er[:, LANES:]
            if mask is not None:
                a = jnp.where(mask, a, 0.0)
            a = a.astype(BF16)
            if row0:
                a = jnp.concatenate([jnp.zeros((row0, LANES), BF16), a], axis=0)
            a_parts[h][c] = a

    n_q = seq // tq
    chunks = [(qi * tq, kj * tq) for qi in range(n_q) for kj in range(qi, -1, -1)]
    d = qk(*chunks[0])
    sc = [scores(d, c, True) for c in range(n_sub)]
    for n, (q0, k0) in enumerate(chunks):
        if k0 == q0:
            acc_ref[...] = jnp.zeros_like(acc_ref)
            r_ref[...] = jnp.zeros_like(r_ref)
        nxt = chunks[n + 1] if n + 1 < len(chunks) else None
        if nxt is not None:
            d = qk(*nxt)
        sc_next = [None] * n_sub
        a_parts = [[None] * n_sub for _ in heads]
        for c in reversed(range(n_sub)):
            accumulate(sc[c], a_parts, c)
            if nxt is not None:
                sc_next[c] = scores(d, c, nxt[0] == nxt[1])
        for h, hs in enumerate(heads):
            acc_ref[:, hs] += _dot(jnp.concatenate(a_parts[h], axis=1), vs_ref[k0:k0 + tq, hs])
        if k0 == 0:
            o_ref[q0:q0 + tq, :] = acc_ref[...]
        sc = sc_next


def _sb_attention(proj, to_cast, *, tq, nh):
    b, s, _ = proj.shape
    wd = nh * SB_HEAD_DIM
    col0 = (COL_Q // wd, COL_K // wd, COL_V // wd)
    n_heads_steps = SB_HEADS // nh
    cast_specs = _row_block_specs(to_cast, b * n_heads_steps, lambda i, h: i * n_heads_steps + h)
    outs = pl.pallas_call(
        functools.partial(_sb_kernel, n_cast=len(to_cast), seq=s, tq=tq, nh=nh, scale=SB_HEAD_DIM ** -0.5),
        grid=(b, n_heads_steps),
        in_specs=[
            pl.BlockSpec((None, s, wd), lambda i, h: (i, 0, col0[0] + h)),
            pl.BlockSpec((None, s, wd), lambda i, h: (i, 0, col0[1] + h)),
            pl.BlockSpec((None, s, wd), lambda i, h: (i, 0, col0[2] + h)),
            pl.BlockSpec((2 * LANES, 2 * LANES), lambda i, h: (0, 0)),
            *cast_specs,
        ],
        out_specs=[pl.BlockSpec((None, s, wd), lambda i, h: (i, 0, h)), *cast_specs],
        out_shape=[jax.ShapeDtypeStruct((b, s, SB_WIDTH), F32),
                   *[jax.ShapeDtypeStruct(a.shape, BF16) for a in to_cast]],
        scratch_shapes=[
            pltpu.VMEM((s, wd), BF16),
            pltpu.VMEM((s, wd), BF16),
            pltpu.VMEM((s, wd), BF16),
            pltpu.VMEM((tq, wd), F32),
            pltpu.VMEM((tq, nh * LANES), F32),
        ],
        compiler_params=_params(2, 40),
        name="sb_attn",
    )(proj, proj, proj, _suffix_sum_matrix(), *to_cast)
    return outs[0], outs[1:]


def _ssd_kernel(z_ref, xbc_ref, dt_ref, convw_ref, convb_ref, dtb_ref, alog_ref, dskip_ref, norm_ref,
                tri_ref, ex_ref, o_ref, xpad_ref, state_ref, *, lc, n_chunks):
    @pl.when(pl.program_id(1) == 0)
    def _():
        state_ref[...] = jnp.zeros_like(state_ref)
        xpad_ref[:, 0:SUBLANES, :] = jnp.zeros((CONV_DIM // LANES, SUBLANES, LANES), F32)

    ri = lax.broadcasted_iota(jnp.int32, (lc, lc), 0)
    ci = lax.broadcasted_iota(jnp.int32, (lc, lc), 1)
    tril = ri >= ci
    head_of_lane = lax.broadcasted_iota(jnp.int32, (1, SSM_GROUP_WIDTH), 1) // SSM_HEAD_DIM
    n_x, n_b = SSM_INNER // LANES, SSM_GROUPS * SSM_STATE // LANES

    for sub in range(n_chunks):
        rows = slice(sub * lc, (sub + 1) * lc)
        act = []
        for c in range(CONV_DIM // LANES):
            cols = slice(c * LANES, (c + 1) * LANES)
            xpad_ref[c, SUBLANES:SUBLANES + lc, :] = xbc_ref[rows, cols]
            conv = convb_ref[:, cols]
            for k in range(SSM_CONV):
                shift = SUBLANES - (SSM_CONV - 1) + k
                conv = conv + convw_ref[k:k + 1, cols] * xpad_ref[c, shift:shift + lc, :]
            xpad_ref[c, 0:SUBLANES, :] = xpad_ref[c, lc:lc + SUBLANES, :]
            act.append(_silu(conv))
        xs = jnp.concatenate(act[:n_x], axis=1)
        bm = jnp.concatenate(act[n_x:n_x + n_b], axis=1)
        cm = jnp.concatenate(act[n_x + n_b:], axis=1)

        dt = _softplus(dt_ref[rows, :] + dtb_ref[...])
        da = dt * (-jnp.exp(alog_ref[...]))
        da_hi, da_lo = _split_bf16(da)
        cs2 = _dot(tri_ref[...], jnp.concatenate([da_hi, da_lo], axis=1))
        cs = cs2[:, :LANES] + cs2[:, LANES:]
        cs_t = cs.T

        ex = ex_ref[...]
        dt_x = _dot(jnp.concatenate(_split_bf16(dt), axis=1), ex)
        cs_x = _dot(jnp.concatenate(_split_bf16(cs), axis=1), ex)
        cs_last = cs_x[lc - 1:lc, :]
        xd = xs * dt_x
        xd_bf = xd.astype(BF16)
        xd_end = (xd * jnp.exp(cs_last - cs_x)).astype(BF16)
        in_decay = jnp.exp(cs_x)
        chunk_decay = jnp.exp(cs_last)

        ys = []
        for g in range(SSM_GROUPS):
            gl = slice(g * SSM_GROUP_WIDTH, (g + 1) * SSM_GROUP_WIDTH)
            sl = slice(g * SSM_STATE, (g + 1) * SSM_STATE)
            bm_g = bm[:, sl]
            cm_g = cm[:, sl].astype(BF16)
            cb = _dot_nt(cm_g, bm_g.astype(BF16))
            xd_g = xd_bf[:, gl]
            y_diag = jnp.zeros((lc, SSM_GROUP_WIDTH), F32)
            for r in range(SSM_HEADS_PER_GROUP):
                h = g * SSM_HEADS_PER_GROUP + r
                seg = cs[:, h:h + 1] - cs_t[h:h + 1, :]
                decay = jnp.exp(jnp.where(tril, seg, -jnp.inf))
                x_h = jnp.where(head_of_lane == r, xd_g, jnp.zeros_like(xd_g))
                y_diag = y_diag + _dot((cb * decay).astype(BF16), x_h)
            state = state_ref[g]
            y_off = _dot(cm_g, state.astype(BF16)) * in_decay[:, gl]
            state_ref[g] = state * chunk_decay[:, gl] + _dot(bm_g.T.astype(BF16), xd_end[:, gl])
            ys.append(y_diag + y_off)
        y = jnp.concatenate(ys, axis=1) + dskip_ref[...] * xs
        y = y * _silu(z_ref[rows, :])
        o_ref[rows, :] = _rms(y, norm_ref[...]).astype(o_ref.dtype)


def _ssd(proj, dt_raw, conv_w, conv_b, dt_bias, a_log, d_skip, ssm_norm, *, lc, n_chunks):
    b, s, _ = proj.shape
    ts = lc * n_chunks
    pad_heads = lambda v: jnp.pad(v.astype(F32), (0, LANES - SSM_HEADS)).reshape(1, LANES)
    tri = jnp.tril(jnp.ones((lc, lc), BF16))
    lane_head = jnp.arange(SSM_INNER)[None, :] // SSM_HEAD_DIM
    ex_half = (jnp.arange(LANES)[:, None] == lane_head).astype(BF16)
    ex = jnp.concatenate([ex_half, ex_half], axis=0)
    const = lambda shape: pl.BlockSpec(shape, lambda i, c: (0,) * len(shape))
    return pl.pallas_call(
        functools.partial(_ssd_kernel, lc=lc, n_chunks=n_chunks),
        grid=(b, s // ts),
        in_specs=[
            pl.BlockSpec((None, ts, SSM_INNER), lambda i, c: (i, c, COL_Z // SSM_INNER)),
            pl.BlockSpec((None, ts, CONV_DIM), lambda i, c: (i, c, COL_XBC // CONV_DIM)),
            pl.BlockSpec((None, ts, LANES), lambda i, c: (i, c, 0)),
            const((SSM_CONV, CONV_DIM)),
            const((1, CONV_DIM)),
            const((1, LANES)),
            const((1, LANES)),
            const((1, SSM_INNER)),
            const((1, SSM_INNER)),
            const((lc, lc)),
            const((2 * LANES, SSM_INNER)),
        ],
        out_specs=pl.BlockSpec((None, ts, SSM_INNER), lambda i, c: (i, c, 0)),
        out_shape=jax.ShapeDtypeStruct((b, s, SSM_INNER), BF16),
        scratch_shapes=[
            pltpu.VMEM((CONV_DIM // LANES, lc + SUBLANES, LANES), F32),
            pltpu.VMEM((SSM_GROUPS, SSM_STATE, SSM_GROUP_WIDTH), F32),
        ],
        compiler_params=_params(2, 40),
        name="ssd",
    )(proj, proj, dt_raw, conv_w.astype(F32), conv_b.reshape(1, CONV_DIM).astype(F32), pad_heads(dt_bias),
      pad_heads(a_log), jnp.repeat(d_skip.astype(F32), SSM_HEAD_DIM).reshape(1, SSM_INNER),
      ssm_norm.reshape(1, SSM_INNER).astype(F32), tri, ex)


def _mem_kv_kernel(m_ref, norm_ref, w_ref, *rest, n_cast):
    cast_in, (o_ref, *cast_out) = rest[:n_cast], rest[n_cast:]
    for src, dst in zip(cast_in, cast_out):
        dst[...] = src[...].astype(BF16)
    o_ref[...] = _dot(_rms(m_ref[...], norm_ref[...]).astype(BF16), w_ref[...]).astype(o_ref.dtype)


def _mem_kv(mem, kv_norm, w_mkv, to_cast):
    b, n, d = mem.shape
    cast_specs = _row_block_specs(to_cast, b, lambda i: i)
    outs = pl.pallas_call(
        functools.partial(_mem_kv_kernel, n_cast=len(to_cast)),
        grid=(b,),
        in_specs=[
            pl.BlockSpec((None, n, d), lambda i: (i, 0, 0)),
            pl.BlockSpec((1, d), lambda i: (0, 0)),
            pl.BlockSpec((d, 2 * MEM_WIDTH), lambda i: (0, 0)),
            *cast_specs,
        ],
        out_specs=[pl.BlockSpec((None, n, 2 * MEM_WIDTH), lambda i: (i, 0, 0)), *cast_specs],
        out_shape=[jax.ShapeDtypeStruct((b, n, 2 * MEM_WIDTH), BF16),
                   *[jax.ShapeDtypeStruct(a.shape, BF16) for a in to_cast]],
        compiler_params=_params(1, 40),
        name="mem_kv",
    )(mem, kv_norm, w_mkv, *to_cast)
    return outs[0], outs[1:]


def _mix_mem_kernel(sb_ref, y_ref, h_ref, kv_ref, sbn_ref, woa_ref, wob_ref, mixpost_ref, mempre_ref, wmq_ref,
                    wmo_ref, mempost_ref, o_ref, *, scale, n_chunks, mc):
    def stage_norm_sb(st, rows):
        st["sbn"] = _rms(sb_ref[rows, :], sbn_ref[...]).astype(BF16)

    def stage_mix(st, rows):
        st["mix"] = _dot(st["sbn"], woa_ref[...]) + _dot(y_ref[rows, :], wob_ref[...])

    def stage_norms(st, rows):
        st["h2"] = h_ref[rows, :] + _rms(st["mix"], mixpost_ref[...])
        st["u"] = _rms(st["h2"], mempre_ref[...]).astype(BF16)

    def stage_q(st, rows):
        st["q"] = _dot(st["u"], wmq_ref[...])

    def stage_attention(st, rows):
        heads = []
        for hd in range(MEM_HEADS):
            sl = slice(hd * MEM_HEAD_DIM, (hd + 1) * MEM_HEAD_DIM)
            k_h = kv_ref[:, sl]
            v_h = kv_ref[:, MEM_WIDTH + hd * MEM_HEAD_DIM:MEM_WIDTH + (hd + 1) * MEM_HEAD_DIM]
            sc = _dot_nt(st["q"][:, sl].astype(BF16), k_h) * scale
            e = jnp.exp(sc - jnp.max(sc, axis=-1, keepdims=True))
            p = e / jnp.sum(e, axis=-1, keepdims=True)
            heads.append(_dot(p.astype(BF16), v_h))
        st["att"] = jnp.concatenate(heads, axis=1).astype(BF16)

    def stage_out(st, rows):
        st["mo"] = _dot(st["att"], wmo_ref[...])

    def stage_final(st, rows):
        o_ref[rows, :] = st["h2"] + _rms(st["mo"], mempost_ref[...])

    stages = (stage_norm_sb, stage_mix, stage_norms, stage_q, stage_attention, stage_out, stage_final)
    states = [{} for _ in range(n_chunks)]
    for tick in range(len(stages) + n_chunks - 1):
        for m in range(n_chunks):
            if 0 <= tick - m < len(stages):
                stages[tick - m](states[m], slice(m * mc, (m + 1) * mc))


def _mix_mem(sb, y, h, kv, sb_norm, w_out, mix_post, mem_pre, w_mq, w_mo, mem_post, *, tm, mc, seq):
    t, d = h.shape
    n_mem = kv.shape[1]
    steps_per_batch = seq // tm
    assert SB_WIDTH == SSM_INNER and w_out.shape == (SB_WIDTH + SSM_INNER, d)
    const = lambda shape, idx=None: pl.BlockSpec(shape, lambda i: idx or (0,) * len(shape),
                                                 pipeline_mode=pl.Buffered(1))
    return pl.pallas_call(
        functools.partial(_mix_mem_kernel, scale=MEM_HEAD_DIM ** -0.5, n_chunks=tm // mc, mc=mc),
        grid=(t // tm,),
        in_specs=[
            pl.BlockSpec((tm, SB_WIDTH), lambda i: (i, 0)),
            pl.BlockSpec((tm, SSM_INNER), lambda i: (i, 0)),
            pl.BlockSpec((tm, d), lambda i: (i, 0)),
            pl.BlockSpec((None, n_mem, 2 * MEM_WIDTH), lambda i: (i // steps_per_batch, 0, 0)),
            const((1, SB_WIDTH)),
            const((SB_WIDTH, d), (0, 0)),
            const((SSM_INNER, d), (1, 0)),
            const((1, d)),
            const((1, d)),
            const((d, MEM_WIDTH)),
            const((MEM_WIDTH, d)),
            const((1, d)),
        ],
        out_specs=pl.BlockSpec((tm, d), lambda i: (i, 0)),
        out_shape=jax.ShapeDtypeStruct((t, d), F32),
        compiler_params=_params(1, 52),
        name="mix_mem",
    )(sb, y, h, kv, sb_norm, w_out, w_out, mix_post, mem_pre, w_mq, w_mo, mem_post)


FFN_TM = 1024
FFN_TF = 512
FFN_MC = 2 * MXU_DIM
PROJ_TM = 1024
PROJ_TN = 6 * MXU_DIM
PROJ_MC = MXU_DIM
SB_TQ = 512
SB_NH = 1
SSD_CHUNK = 128
SSD_CHUNKS_PER_STEP = 2
MIX_TM = 512
MIX_MC = MXU_DIM


def _row(v):
    return v.reshape(1, -1).astype(F32)


def _layer(h, mem, p):
    b, s, d = h.shape
    t = b * s

    def ffn(hh, pre, wg, wu, wd, post):
        tm = min(FFN_TM, t)
        return _ffn(hh, _row(pre), wg, wu, wd, _row(post), tm=tm, tf=FFN_TF, mc=min(FFN_MC, tm))

    h1 = ffn(h.reshape(t, d), p["ffn1_pre"], p["ffn1_wg"].astype(BF16), p["ffn1_wu"].astype(BF16),
             p["ffn1_wd"].astype(BF16), p["ffn1_post"])

    kv, (w_in_t,) = _mem_kv(mem, _row(p["mem_kv_norm"]), p["w_mkv"].astype(BF16), (p["w_in"].T,))
    proj, dt_raw = _in_proj(h1, _row(p["mix_pre"]), w_in_t, n=PROJ_WIDTH, tm=min(PROJ_TM, t), tn=PROJ_TN,
                            mc=min(PROJ_MC, t))
    proj = proj.reshape(b, s, PROJ_WIDTH)
    later = (p["ffn2_wg"], p["ffn2_wu"], p["ffn2_wd"], p["w_out"], p["w_mq"], p["w_mo"])
    sb, (wg2, wu2, wd2, w_out, w_mq, w_mo) = _sb_attention(proj, later, tq=min(SB_TQ, s), nh=SB_NH)
    y = _ssd(proj, dt_raw.reshape(b, s, LANES), p["conv_w"], p["conv_b"], p["dt_bias"], p["a_log"], p["d_skip"],
             p["ssm_norm"], lc=min(SSD_CHUNK, s), n_chunks=SSD_CHUNKS_PER_STEP)

    h3 = _mix_mem(sb.reshape(t, SB_WIDTH), y.reshape(t, SSM_INNER), h1, kv, _row(p["sb_norm"]), w_out,
                  _row(p["mix_post"]), _row(p["mem_pre"]), w_mq, w_mo, _row(p["mem_post"]), tm=min(MIX_TM, s),
                  mc=min(MIX_MC, s), seq=s)

    out = ffn(h3, p["ffn2_pre"], wg2, wu2, wd2, p["ffn2_post"])
    return out.reshape(b, s, d)


def kernel(x, mem, ffn1_pre, ffn1_wg, ffn1_wu, ffn1_wd, ffn1_post, mix_pre, w_in, conv_w, conv_b, dt_bias, a_log,
           d_skip, sb_norm, ssm_norm, w_out, mix_post, mem_pre, mem_kv_norm, w_mq, w_mkv, w_mo, mem_post, ffn2_pre,
           ffn2_wg, ffn2_wu, ffn2_wd, ffn2_post):
    params = dict(
        ffn1_pre=ffn1_pre, ffn1_wg=ffn1_wg, ffn1_wu=ffn1_wu, ffn1_wd=ffn1_wd, ffn1_post=ffn1_post,
        mix_pre=mix_pre, w_in=w_in, conv_w=conv_w, conv_b=conv_b, dt_bias=dt_bias, a_log=a_log, d_skip=d_skip,
        sb_norm=sb_norm, ssm_norm=ssm_norm, w_out=w_out, mix_post=mix_post, mem_pre=mem_pre,
        mem_kv_norm=mem_kv_norm, w_mq=w_mq, w_mkv=w_mkv, w_mo=w_mo, mem_post=mem_post,
        ffn2_pre=ffn2_pre, ffn2_wg=ffn2_wg, ffn2_wu=ffn2_wu, ffn2_wd=ffn2_wd, ffn2_post=ffn2_post)
    h = x
    for layer in range(ffn1_pre.shape[0]):
        h = _layer(h, mem, {k: v[layer] for k, v in params.items()})
    return h
```

```python
import functools

import jax
import jax.numpy as jnp
from jax import lax
from jax.experimental import pallas as pl
from jax.experimental.pallas import tpu as pltpu

F32 = jnp.float32
BF16 = jnp.bfloat16

EPS = 1e-6
LOG2E = 1.4426950408889634
LANES = 128
SUBLANES = 8
MXU_DIM = 256
MIB = 1024 * 1024
VMEM_BYTES_V7X = 64 * MIB
VMEM_LIMIT_CAP = VMEM_BYTES_V7X - 4 * MIB
SMALL_CALL_VMEM_MIB = 40
MIX_VMEM_MIB = 52

SB_HEADS = 8
SB_HEAD_DIM = 128
SB_WIDTH = SB_HEADS * SB_HEAD_DIM
SSM_HEADS = 16
SSM_HEAD_DIM = 64
SSM_INNER = SSM_HEADS * SSM_HEAD_DIM
SSM_GROUPS = 4
SSM_HEADS_PER_GROUP = SSM_HEADS // SSM_GROUPS
SSM_STATE = 128
SSM_CONV = 4
SSM_GROUP_WIDTH = SSM_HEADS_PER_GROUP * SSM_HEAD_DIM
CONV_DIM = SSM_INNER + 2 * SSM_GROUPS * SSM_STATE
MEM_HEADS = 4
MEM_HEAD_DIM = 128
MEM_WIDTH = MEM_HEADS * MEM_HEAD_DIM

COL_Q = 0
COL_K = SB_WIDTH
COL_V = 2 * SB_WIDTH
COL_Z = 3 * SB_WIDTH
COL_XBC = COL_Z + SSM_INNER
PROJ_WIDTH = COL_XBC + CONV_DIM


def _params(n_grid_axes, vmem_mib):
    return pltpu.CompilerParams(
        dimension_semantics=("arbitrary",) * n_grid_axes,
        vmem_limit_bytes=min(vmem_mib * MIB, VMEM_LIMIT_CAP),
    )


def _rms(x, w):
    ms = jnp.mean(x * x, axis=-1, keepdims=True)
    return x * lax.rsqrt(ms + EPS) * w


def _silu(x):
    return x * jax.nn.sigmoid(x)


def _softplus(x):
    return jnp.maximum(x, 0.0) + jnp.log1p(jnp.exp(-jnp.abs(x)))


def _split_bf16(x):
    hi = x.astype(BF16)
    lo = (x - hi.astype(F32)).astype(BF16)
    return hi, lo


def _dot(a, b):
    return jnp.dot(a, b, preferred_element_type=F32)


def _dot_nt(a, b):
    return lax.dot_general(a, b, (((1,), (1,)), ((), ())), preferred_element_type=F32)


def _row_block_specs(arrays, n_steps, step_of):
    specs = []
    for a in arrays:
        rows = a.shape[0]
        pack = 2 * SUBLANES
        br = -(-rows // (n_steps * pack)) * pack
        while rows % br:
            br += pack
        last = rows // br - 1
        specs.append(pl.BlockSpec(
            (br, a.shape[1]), lambda *ids, last=last: (jnp.minimum(step_of(*ids), last), 0)))
    return specs


def _ffn_kernel(x_ref, pre_ref, wg_ref, wu_ref, wd_ref, wg_hbm, wu_hbm, wd_hbm, post_ref, o_ref, u_ref, wgt_ref,
                wut_ref, wdt_ref, sem, *, n_main, f_main, ft, n_chunks, mc):
    i = pl.program_id(0)
    j = pl.program_id(1)

    def tail_copies():
        return (pltpu.make_async_copy(wg_hbm.at[:, pl.ds(f_main, ft)], wgt_ref, sem.at[0]),
                pltpu.make_async_copy(wu_hbm.at[:, pl.ds(f_main, ft)], wut_ref, sem.at[1]),
                pltpu.make_async_copy(wd_hbm.at[pl.ds(f_main, ft), :], wdt_ref, sem.at[2]))

    @pl.when(jnp.logical_and(i == 0, j == 0))
    def _():
        for copy in tail_copies():
            copy.start()

    @pl.when(jnp.logical_and(i == 0, j == n_main))
    def _():
        for copy in tail_copies():
            copy.wait()

    def step(first, last):
        wg, wu, wd = (wgt_ref, wut_ref, wdt_ref) if last else (wg_ref, wu_ref, wd_ref)
        chunk = mc if first or last else n_chunks * mc
        for m in range(n_chunks * mc // chunk):
            rows = slice(m * chunk, (m + 1) * chunk)
            if first:
                u = _rms(x_ref[rows, :], pre_ref[...]).astype(BF16)
                u_ref[rows, :] = u
            else:
                u = u_ref[rows, :]
            act = (_silu(_dot(u, wg[...])) * _dot(u, wu[...])).astype(BF16)
            y = _dot(act, wd[...])
            if not first:
                y = o_ref[rows, :] + y
            if last:
                y = x_ref[rows, :] + 0.5 * _rms(y, post_ref[...])
            o_ref[rows, :] = y

    pl.when(j == 0)(functools.partial(step, True, False))
    pl.when(jnp.logical_and(j > 0, j < n_main))(functools.partial(step, False, False))
    pl.when(j == n_main)(functools.partial(step, False, True))


def _ffn(h, pre, wg, wu, wd, post, *, tm, tf, mc):
    t, d = h.shape
    f = wg.shape[1]
    n_main = (f - 1) // tf
    f_main = n_main * tf
    ft = f - f_main
    assert n_main >= 1 and ft % LANES == 0 and tm % mc == 0
    vmem = (4 * tm * d * 4 + tm * d * 2 + 2 * 3 * d * tf * 2 + 3 * d * ft * 2 + 4 * mc * tf * 4
            + 2 * mc * d * 4) // MIB + 4
    main_col = lambda i, j: (0, jnp.minimum(j, n_main - 1))
    main_row = lambda i, j: (jnp.minimum(j, n_main - 1), 0)
    const = lambda shape: pl.BlockSpec(shape, lambda i, j: (0, 0), pipeline_mode=pl.Buffered(1))
    hbm = pl.BlockSpec(memory_space=pl.ANY)
    return pl.pallas_call(
        functools.partial(_ffn_kernel, n_main=n_main, f_main=f_main, ft=ft, n_chunks=tm // mc, mc=mc),
        grid=(t // tm, n_main + 1),
        in_specs=[
            pl.BlockSpec((tm, d), lambda i, j: (i, 0)),
            const((1, d)),
            pl.BlockSpec((d, tf), main_col),
            pl.BlockSpec((d, tf), main_col),
            pl.BlockSpec((tf, d), main_row),
            hbm,
            hbm,
            hbm,
            const((1, d)),
        ],
        out_specs=pl.BlockSpec((tm, d), lambda i, j: (i, 0)),
        out_shape=jax.ShapeDtypeStruct((t, d), F32),
        scratch_shapes=[
            pltpu.VMEM((tm, d), BF16),
            pltpu.VMEM((d, ft), BF16),
            pltpu.VMEM((d, ft), BF16),
            pltpu.VMEM((ft, d), BF16),
            pltpu.SemaphoreType.DMA((3,)),
        ],
        compiler_params=_params(2, vmem),
        name="ffn",
    )(h, pre, wg, wu, wd, wg, wu, wd, post)


def _in_proj_kernel(x_ref, pre_ref, w_ref, wx_ref, o_ref, ox_ref, u_ref, wxp_ref, *, n_extra, n_chunks, mc):
    j = pl.program_id(1)

    @pl.when(j == 0)
    def _():
        wxp_ref[...] = jnp.zeros_like(wxp_ref)
        wxp_ref[0:n_extra, :] = wx_ref[...]
        for m in range(n_chunks):
            rows = slice(m * mc, (m + 1) * mc)
            u = _rms(x_ref[rows, :], pre_ref[...]).astype(BF16)
            u_ref[rows, :] = u
            o_ref[rows, :] = _dot_nt(u, w_ref[...])
            ox_ref[rows, :] = _dot_nt(u, wxp_ref[...])

    @pl.when(j > 0)
    def _():
        o_ref[...] = _dot_nt(u_ref[...], w_ref[...])


def _in_proj(h, pre, w_t, *, n, tm, tn, mc):
    t, d = h.shape
    n_extra = w_t.shape[0] - n
    assert n % tn == 0 and 0 < n_extra <= LANES and n % n_extra == 0 and n_extra % (2 * SUBLANES) == 0
    assert tm % mc == 0
    vmem = (2 * tm * d * 4 + tm * d * 2 + 2 * d * tn * 2 + 3 * tm * tn * 4) // MIB + 8
    return pl.pallas_call(
        functools.partial(_in_proj_kernel, n_extra=n_extra, n_chunks=tm // mc, mc=mc),
        grid=(t // tm, n // tn),
        in_specs=[
            pl.BlockSpec((tm, d), lambda i, j: (i, 0)),
            pl.BlockSpec((1, d), lambda i, j: (0, 0)),
            pl.BlockSpec((tn, d), lambda i, j: (j, 0)),
            pl.BlockSpec((n_extra, d), lambda i, j: (n // n_extra, 0)),
        ],
        out_specs=[
            pl.BlockSpec((tm, tn), lambda i, j: (i, j)),
            pl.BlockSpec((tm, LANES), lambda i, j: (i, 0)),
        ],
        out_shape=[jax.ShapeDtypeStruct((t, n), F32), jax.ShapeDtypeStruct((t, LANES), F32)],
        scratch_shapes=[pltpu.VMEM((tm, d), BF16), pltpu.VMEM((LANES, d), BF16)],
        compiler_params=_params(2, vmem),
        name="in_proj",
    )(h, pre, w_t, w_t)


def _suffix_sum_matrix():
    j = jnp.arange(LANES)[:, None]
    s = jnp.arange(LANES)[None, :]
    half = jnp.concatenate([(j > s), jnp.ones((LANES, LANES), bool)], axis=1)
    return jnp.concatenate([half, half], axis=0).astype(BF16)


def _sb_kernel(q_ref, k_ref, v_ref, w_ref, *rest, n_cast, seq, tq, nh, scale):
    cast_in, (o_ref, *cast_out) = rest[:n_cast], rest[n_cast:2 * n_cast + 1]
    qs_ref, ks_ref, vs_ref, acc_ref, r_ref = rest[2 * n_cast + 1:]
    for src, dst in zip(cast_in, cast_out):
        dst[...] = src[...].astype(BF16)
    qs_ref[...] = q_ref[...].astype(BF16)
    ks_ref[...] = k_ref[...].astype(BF16)
    vs_ref[...] = v_ref[...].astype(BF16)
    w = w_ref[...]
    n_sub = tq // LANES
    heads = [slice(h * SB_HEAD_DIM, (h + 1) * SB_HEAD_DIM) for h in range(nh)]
    def qk(q0, k0):
        return [_dot_nt(qs_ref[q0:q0 + tq, hs], ks_ref[k0:k0 + tq, hs]) for hs in heads]

    def scores(d, c, diag):
        row0 = c * LANES if diag else 0
        mask = None
        if diag:
            mask = (lax.broadcasted_iota(jnp.int32, (tq - row0, LANES), 1)
                    < lax.broadcasted_iota(jnp.int32, (tq - row0, LANES), 0))
        per_head = []
        for dh_full in d:
            dh = dh_full[row0:, c * LANES:(c + 1) * LANES]
            z = dh * scale
            e = jnp.log(1.0 + jnp.exp2(jnp.abs(dh) * (-scale * LOG2E)))
            lb = jnp.minimum(z, 0.0) - e
            lk = lb - z
            per_head.append((lb, lk if mask is None else jnp.where(mask, lk, 0.0)))
        return row0, mask, per_head

    def accumulate(sc, a_parts, c):
        row0, mask, per_head = sc
        rows = slice(row0, tq)
        for h, hs in enumerate(heads):
            lb, lk = per_head[h]
            hi, lo = _split_bf16(lk)
            er = _dot(jnp.concatenate([hi, lo], axis=1), w)
            r = r_ref[rows, hs]
            a = jnp.exp(lb + er[:, :LANES] + r)
            r_ref[rows, hs] = r + er[:, LANES:]
            if mask is not None:
                a = jnp.where(mask, a, 0.0)
            a = a.astype(BF16)
            if row0:
                a = jnp.concatenate([jnp.zeros((row0, LANES), BF16), a], axis=0)
            a_parts[h][c] = a

    n_q = seq // tq
    chunks = [(qi * tq, kj * tq) for qi in range(n_q) for kj in range(qi, -1, -1)]
    d = qk(*chunks[0])
    sc = [scores(d, c, True) for c in range(n_sub)]
    for n, (q0, k0) in enumerate(chunks):
        if k0 == q0:
            acc_ref[...] = jnp.zeros_like(acc_ref)
            r_ref[...] = jnp.zeros_like(r_ref)
        nxt = chunks[n + 1] if n + 1 < len(chunks) else None
        if nxt is not None:
            d = qk(*nxt)
        sc_next = [None] * n_sub
        a_parts = [[None] * n_sub for _ in heads]
        for c in reversed(range(n_sub)):
            accumulate(sc[c], a_parts, c)
            if nxt is not None:
                sc_next[c] = scores(d, c, nxt[0] == nxt[1])
        for h, hs in enumerate(heads):
            acc_ref[:, hs] += _dot(jnp.concatenate(a_parts[h], axis=1), vs_ref[k0:k0 + tq, hs])
        if k0 == 0:
            o_ref[q0:q0 + tq, :] = acc_ref[...]
        sc = sc_next


def _sb_attention(proj, to_cast, *, tq, nh):
    b, s, _ = proj.shape
    wd = nh * SB_HEAD_DIM
    col0 = (COL_Q // wd, COL_K // wd, COL_V // wd)
    n_heads_steps = SB_HEADS // nh
    cast_specs = _row_block_specs(to_cast, b * n_heads_steps, lambda i, h: i * n_heads_steps + h)
    outs = pl.pallas_call(
        functools.partial(_sb_kernel, n_cast=len(to_cast), seq=s, tq=tq, nh=nh, scale=SB_HEAD_DIM ** -0.5),
        grid=(b, n_heads_steps),
        in_specs=[
            pl.BlockSpec((None, s, wd), lambda i, h: (i, 0, col0[0] + h)),
            pl.BlockSpec((None, s, wd), lambda i, h: (i, 0, col0[1] + h)),
            pl.BlockSpec((None, s, wd), lambda i, h: (i, 0, col0[2] + h)),
            pl.BlockSpec((2 * LANES, 2 * LANES), lambda i, h: (0, 0)),
            *cast_specs,
        ],
        out_specs=[pl.BlockSpec((None, s, wd), lambda i, h: (i, 0, h)), *cast_specs],
        out_shape=[jax.ShapeDtypeStruct((b, s, SB_WIDTH), F32),
                   *[jax.ShapeDtypeStruct(a.shape, BF16) for a in to_cast]],
        scratch_shapes=[
            pltpu.VMEM((s, wd), BF16),
            pltpu.VMEM((s, wd), BF16),
            pltpu.VMEM((s, wd), BF16),
            pltpu.VMEM((tq, wd), F32),
            pltpu.VMEM((tq, nh * LANES), F32),
        ],
        compiler_params=_params(2, SMALL_CALL_VMEM_MIB),
        name="sb_attn",
    )(proj, proj, proj, _suffix_sum_matrix(), *to_cast)
    return outs[0], outs[1:]


def _ssd_kernel(z_ref, xbc_ref, dt_ref, convw_ref, convb_ref, dtb_ref, alog_ref, dskip_ref, norm_ref,
                tri_ref, ex_ref, o_ref, xpad_ref, state_ref, *, lc, n_chunks):
    @pl.when(pl.program_id(1) == 0)
    def _():
        state_ref[...] = jnp.zeros_like(state_ref)
        xpad_ref[:, 0:SUBLANES, :] = jnp.zeros((CONV_DIM // LANES, SUBLANES, LANES), F32)

    ri = lax.broadcasted_iota(jnp.int32, (lc, lc), 0)
    ci = lax.broadcasted_iota(jnp.int32, (lc, lc), 1)
    tril = ri >= ci
    head_of_lane = lax.broadcasted_iota(jnp.int32, (1, SSM_GROUP_WIDTH), 1) // SSM_HEAD_DIM
    n_x, n_b = SSM_INNER // LANES, SSM_GROUPS * SSM_STATE // LANES

    for sub in range(n_chunks):
        rows = slice(sub * lc, (sub + 1) * lc)
        act = []
        for c in range(CONV_DIM // LANES):
            cols = slice(c * LANES, (c + 1) * LANES)
            xpad_ref[c, SUBLANES:SUBLANES + lc, :] = xbc_ref[rows, cols]
            conv = convb_ref[:, cols]
            for k in range(SSM_CONV):
                shift = SUBLANES - (SSM_CONV - 1) + k
                conv = conv + convw_ref[k:k + 1, cols] * xpad_ref[c, shift:shift + lc, :]
            xpad_ref[c, 0:SUBLANES, :] = xpad_ref[c, lc:lc + SUBLANES, :]
            act.append(_silu(conv))
        xs = jnp.concatenate(act[:n_x], axis=1)
        bm = jnp.concatenate(act[n_x:n_x + n_b], axis=1)
        cm = jnp.concatenate(act[n_x + n_b:], axis=1)

        dt = _softplus(dt_ref[rows, :] + dtb_ref[...])
        da = dt * (-jnp.exp(alog_ref[...]))
        da_hi, da_lo = _split_bf16(da)
        cs2 = _dot(tri_ref[...], jnp.concatenate([da_hi, da_lo], axis=1))
        cs = cs2[:, :LANES] + cs2[:, LANES:]
        cs_t = cs.T

        ex = ex_ref[...]
        dt_x = _dot(jnp.concatenate(_split_bf16(dt), axis=1), ex)
        cs_x = _dot(jnp.concatenate(_split_bf16(cs), axis=1), ex)
        cs_last = cs_x[lc - 1:lc, :]
        xd = xs * dt_x
        xd_bf = xd.astype(BF16)
        xd_end = (xd * jnp.exp(cs_last - cs_x)).astype(BF16)
        in_decay = jnp.exp(cs_x)
        chunk_decay = jnp.exp(cs_last)

        ys = []
        for g in range(SSM_GROUPS):
            gl = slice(g * SSM_GROUP_WIDTH, (g + 1) * SSM_GROUP_WIDTH)
            sl = slice(g * SSM_STATE, (g + 1) * SSM_STATE)
            bm_g = bm[:, sl]
            cm_g = cm[:, sl].astype(BF16)
            cb = _dot_nt(cm_g, bm_g.astype(BF16))
            xd_g = xd_bf[:, gl]
            y_diag = jnp.zeros((lc, SSM_GROUP_WIDTH), F32)
            for r in range(SSM_HEADS_PER_GROUP):
                h = g * SSM_HEADS_PER_GROUP + r
                seg = cs[:, h:h + 1] - cs_t[h:h + 1, :]
                decay = jnp.exp(jnp.where(tril, seg, -jnp.inf))
                x_h = jnp.where(head_of_lane == r, xd_g, jnp.zeros_like(xd_g))
                y_diag = y_diag + _dot((cb * decay).astype(BF16), x_h)
            state = state_ref[g]
            y_off = _dot(cm_g, state.astype(BF16)) * in_decay[:, gl]
            state_ref[g] = state * chunk_decay[:, gl] + _dot(bm_g.T.astype(BF16), xd_end[:, gl])
            ys.append(y_diag + y_off)
        y = jnp.concatenate(ys, axis=1) + dskip_ref[...] * xs
        y = y * _silu(z_ref[rows, :])
        o_ref[rows, :] = _rms(y, norm_ref[...]).astype(o_ref.dtype)


def _ssd(proj, dt_raw, conv_w, conv_b, dt_bias, a_log, d_skip, ssm_norm, *, lc, n_chunks):
    b, s, _ = proj.shape
    ts = lc * n_chunks
    pad_heads = lambda v: jnp.pad(v.astype(F32), (0, LANES - SSM_HEADS)).reshape(1, LANES)
    tri = jnp.tril(jnp.ones((lc, lc), BF16))
    lane_head = jnp.arange(SSM_INNER)[None, :] // SSM_HEAD_DIM
    ex_half = (jnp.arange(LANES)[:, None] == lane_head).astype(BF16)
    ex = jnp.concatenate([ex_half, ex_half], axis=0)
    const = lambda shape: pl.BlockSpec(shape, lambda i, c: (0,) * len(shape))
    return pl.pallas_call(
        functools.partial(_ssd_kernel, lc=lc, n_chunks=n_chunks),
        grid=(b, s // ts),
        in_specs=[
            pl.BlockSpec((None, ts, SSM_INNER), lambda i, c: (i, c, COL_Z // SSM_INNER)),
            pl.BlockSpec((None, ts, CONV_DIM), lambda i, c: (i, c, COL_XBC // CONV_DIM)),
            pl.BlockSpec((None, ts, LANES), lambda i, c: (i, c, 0)),
            const((SSM_CONV, CONV_DIM)),
            const((1, CONV_DIM)),
            const((1, LANES)),
            const((1, LANES)),
            const((1, SSM_INNER)),
            const((1, SSM_INNER)),
            const((lc, lc)),
            const((2 * LANES, SSM_INNER)),
        ],
        out_specs=pl.BlockSpec((None, ts, SSM_INNER), lambda i, c: (i, c, 0)),
        out_shape=jax.ShapeDtypeStruct((b, s, SSM_INNER), BF16),
        scratch_shapes=[
            pltpu.VMEM((CONV_DIM // LANES, lc + SUBLANES, LANES), F32),
            pltpu.VMEM((SSM_GROUPS, SSM_STATE, SSM_GROUP_WIDTH), F32),
        ],
        compiler_params=_params(2, SMALL_CALL_VMEM_MIB),
        name="ssd",
    )(proj, proj, dt_raw, conv_w.astype(F32), conv_b.reshape(1, CONV_DIM).astype(F32), pad_heads(dt_bias),
      pad_heads(a_log), jnp.repeat(d_skip.astype(F32), SSM_HEAD_DIM).reshape(1, SSM_INNER),
      ssm_norm.reshape(1, SSM_INNER).astype(F32), tri, ex)


def _mem_kv_kernel(m_ref, norm_ref, w_ref, *rest, n_cast):
    cast_in, (o_ref, *cast_out) = rest[:n_cast], rest[n_cast:]
    for src, dst in zip(cast_in, cast_out):
        dst[...] = src[...].astype(BF16)
    o_ref[...] = _dot(_rms(m_ref[...], norm_ref[...]).astype(BF16), w_ref[...]).astype(o_ref.dtype)


def _mem_kv(mem, kv_norm, w_mkv, to_cast):
    b, n, d = mem.shape
    cast_specs = _row_block_specs(to_cast, b, lambda i: i)
    outs = pl.pallas_call(
        functools.partial(_mem_kv_kernel, n_cast=len(to_cast)),
        grid=(b,),
        in_specs=[
            pl.BlockSpec((None, n, d), lambda i: (i, 0, 0)),
            pl.BlockSpec((1, d), lambda i: (0, 0)),
            pl.BlockSpec((d, 2 * MEM_WIDTH), lambda i: (0, 0)),
            *cast_specs,
        ],
        out_specs=[pl.BlockSpec((None, n, 2 * MEM_WIDTH), lambda i: (i, 0, 0)), *cast_specs],
        out_shape=[jax.ShapeDtypeStruct((b, n, 2 * MEM_WIDTH), BF16),
                   *[jax.ShapeDtypeStruct(a.shape, BF16) for a in to_cast]],
        compiler_params=_params(1, SMALL_CALL_VMEM_MIB),
        name="mem_kv",
    )(mem, kv_norm, w_mkv, *to_cast)
    return outs[0], outs[1:]


def _mix_mem_kernel(sb_ref, y_ref, h_ref, kv_ref, sbn_ref, woa_ref, wob_ref, mixpost_ref, mempre_ref, wmq_ref,
                    wmo_ref, mempost_ref, o_ref, *, scale, n_chunks, mc):
    def stage_norm_sb(st, rows):
        st["sbn"] = _rms(sb_ref[rows, :], sbn_ref[...]).astype(BF16)

    def stage_mix(st, rows):
        st["mix"] = _dot(st["sbn"], woa_ref[...]) + _dot(y_ref[rows, :], wob_ref[...])

    def stage_norms(st, rows):
        st["h2"] = h_ref[rows, :] + _rms(st["mix"], mixpost_ref[...])
        st["u"] = _rms(st["h2"], mempre_ref[...]).astype(BF16)

    def stage_q(st, rows):
        st["q"] = _dot(st["u"], wmq_ref[...])

    def stage_attention(st, rows):
        heads = []
        for hd in range(MEM_HEADS):
            sl = slice(hd * MEM_HEAD_DIM, (hd + 1) * MEM_HEAD_DIM)
            k_h = kv_ref[:, sl]
            v_h = kv_ref[:, MEM_WIDTH + hd * MEM_HEAD_DIM:MEM_WIDTH + (hd + 1) * MEM_HEAD_DIM]
            sc = _dot_nt(st["q"][:, sl].astype(BF16), k_h) * scale
            e = jnp.exp(sc - jnp.max(sc, axis=-1, keepdims=True))
            p = e / jnp.sum(e, axis=-1, keepdims=True)
            heads.append(_dot(p.astype(BF16), v_h))
        st["att"] = jnp.concatenate(heads, axis=1).astype(BF16)

    def stage_out(st, rows):
        st["mo"] = _dot(st["att"], wmo_ref[...])

    def stage_final(st, rows):
        o_ref[rows, :] = st["h2"] + _rms(st["mo"], mempost_ref[...])

    stages = (stage_norm_sb, stage_mix, stage_norms, stage_q, stage_attention, stage_out, stage_final)
    states = [{} for _ in range(n_chunks)]
    for tick in range(len(stages) + n_chunks - 1):
        for m in range(n_chunks):
            if 0 <= tick - m < len(stages):
                stages[tick - m](states[m], slice(m * mc, (m + 1) * mc))


def _mix_mem(sb, y, h, kv, sb_norm, w_out, mix_post, mem_pre, w_mq, w_mo, mem_post, *, tm, mc, seq):
    t, d = h.shape
    n_mem = kv.shape[1]
    steps_per_batch = seq // tm
    assert SB_WIDTH == SSM_INNER and w_out.shape == (SB_WIDTH + SSM_INNER, d)
    const = lambda shape, idx=None: pl.BlockSpec(shape, lambda i: idx or (0,) * len(shape),
                                                 pipeline_mode=pl.Buffered(1))
    return pl.pallas_call(
        functools.partial(_mix_mem_kernel, scale=MEM_HEAD_DIM ** -0.5, n_chunks=tm // mc, mc=mc),
        grid=(t // tm,),
        in_specs=[
            pl.BlockSpec((tm, SB_WIDTH), lambda i: (i, 0)),
            pl.BlockSpec((tm, SSM_INNER), lambda i: (i, 0)),
            pl.BlockSpec((tm, d), lambda i: (i, 0)),
            pl.BlockSpec((None, n_mem, 2 * MEM_WIDTH), lambda i: (i // steps_per_batch, 0, 0)),
            const((1, SB_WIDTH)),
            const((SB_WIDTH, d), (0, 0)),
            const((SSM_INNER, d), (1, 0)),
            const((1, d)),
            const((1, d)),
            const((d, MEM_WIDTH)),
            const((MEM_WIDTH, d)),
            const((1, d)),
        ],
        out_specs=pl.BlockSpec((tm, d), lambda i: (i, 0)),
        out_shape=jax.ShapeDtypeStruct((t, d), F32),
        compiler_params=_params(1, MIX_VMEM_MIB),
        name="mix_mem",
    )(sb, y, h, kv, sb_norm, w_out, w_out, mix_post, mem_pre, w_mq, w_mo, mem_post)


FFN_TM = 1024
FFN_TF = 512
FFN_MC = 2 * MXU_DIM
PROJ_TM = 1024
PROJ_TN = 6 * MXU_DIM
PROJ_MC = MXU_DIM
SB_TQ = 512
SB_NH = 1
SSD_CHUNK = 128
SSD_CHUNKS_PER_STEP = 2
MIX_TM = 512
MIX_MC = MXU_DIM


def _row(v):
    return v.reshape(1, -1).astype(F32)


def _layer(h, mem, p):
    b, s, d = h.shape
    t = b * s

    def ffn(hh, pre, wg, wu, wd, post):
        tm = min(FFN_TM, t)
        return _ffn(hh, _row(pre), wg, wu, wd, _row(post), tm=tm, tf=FFN_TF, mc=min(FFN_MC, tm))

    h1 = ffn(h.reshape(t, d), p["ffn1_pre"], p["ffn1_wg"].astype(BF16), p["ffn1_wu"].astype(BF16),
             p["ffn1_wd"].astype(BF16), p["ffn1_post"])

    kv, (w_in_t,) = _mem_kv(mem, _row(p["mem_kv_norm"]), p["w_mkv"].astype(BF16), (p["w_in"].T,))
    proj, dt_raw = _in_proj(h1, _row(p["mix_pre"]), w_in_t, n=PROJ_WIDTH, tm=min(PROJ_TM, t), tn=PROJ_TN,
                            mc=min(PROJ_MC, t))
    proj = proj.reshape(b, s, PROJ_WIDTH)
    later = (p["ffn2_wg"], p["ffn2_wu"], p["ffn2_wd"], p["w_out"], p["w_mq"], p["w_mo"])
    sb, (wg2, wu2, wd2, w_out, w_mq, w_mo) = _sb_attention(proj, later, tq=min(SB_TQ, s), nh=SB_NH)
    y = _ssd(proj, dt_raw.reshape(b, s, LANES), p["conv_w"], p["conv_b"], p["dt_bias"], p["a_log"], p["d_skip"],
             p["ssm_norm"], lc=min(SSD_CHUNK, s), n_chunks=SSD_CHUNKS_PER_STEP)

    h3 = _mix_mem(sb.reshape(t, SB_WIDTH), y.reshape(t, SSM_INNER), h1, kv, _row(p["sb_norm"]), w_out,
                  _row(p["mix_post"]), _row(p["mem_pre"]), w_mq, w_mo, _row(p["mem_post"]), tm=min(MIX_TM, s),
                  mc=min(MIX_MC, s), seq=s)

    out = ffn(h3, p["ffn2_pre"], wg2, wu2, wd2, p["ffn2_post"])
    return out.reshape(b, s, d)


def kernel(x, mem, ffn1_pre, ffn1_wg, ffn1_wu, ffn1_wd, ffn1_post, mix_pre, w_in, conv_w, conv_b, dt_bias, a_log,
           d_skip, sb_norm, ssm_norm, w_out, mix_post, mem_pre, mem_kv_norm, w_mq, w_mkv, w_mo, mem_post, ffn2_pre,
           ffn2_wg, ffn2_wu, ffn2_wd, ffn2_post):
    params = dict(
        ffn1_pre=ffn1_pre, ffn1_wg=ffn1_wg, ffn1_wu=ffn1_wu, ffn1_wd=ffn1_wd, ffn1_post=ffn1_post,
        mix_pre=mix_pre, w_in=w_in, conv_w=conv_w, conv_b=conv_b, dt_bias=dt_bias, a_log=a_log, d_skip=d_skip,
        sb_norm=sb_norm, ssm_norm=ssm_norm, w_out=w_out, mix_post=mix_post, mem_pre=mem_pre,
        mem_kv_norm=mem_kv_norm, w_mq=w_mq, w_mkv=w_mkv, w_mo=w_mo, mem_post=mem_post,
        ffn2_pre=ffn2_pre, ffn2_wg=ffn2_wg, ffn2_wu=ffn2_wu, ffn2_wd=ffn2_wd, ffn2_post=ffn2_post)
    h = x
    for layer in range(ffn1_pre.shape[0]):
        h = _layer(h, mem, {k: v[layer] for k, v in params.items()})
    return h
```

```python
import functools

import jax
import jax.numpy as jnp
from jax import lax
from jax.experimental import pallas as pl
from jax.experimental.pallas import tpu as pltpu

F32 = jnp.float32
BF16 = jnp.bfloat16

EPS = 1e-6
LOG2E = 1.4426950408889634
LANES = 128
SUBLANES = 8
MXU_DIM = 256
MIB = 1024 * 1024
VMEM_BYTES_V7X = 64 * MIB
VMEM_LIMIT_CAP = VMEM_BYTES_V7X - 4 * MIB
SMALL_CALL_VMEM_MIB = 40
MIX_VMEM_MIB = 52

SB_HEADS = 8
SB_HEAD_DIM = 128
SB_WIDTH = SB_HEADS * SB_HEAD_DIM
SSM_HEADS = 16
SSM_HEAD_DIM = 64
SSM_INNER = SSM_HEADS * SSM_HEAD_DIM
SSM_GROUPS = 4
SSM_HEADS_PER_GROUP = SSM_HEADS // SSM_GROUPS
SSM_STATE = 128
SSM_CONV = 4
SSM_GROUP_WIDTH = SSM_HEADS_PER_GROUP * SSM_HEAD_DIM
CONV_DIM = SSM_INNER + 2 * SSM_GROUPS * SSM_STATE
MEM_HEADS = 4
MEM_HEAD_DIM = 128
MEM_WIDTH = MEM_HEADS * MEM_HEAD_DIM

COL_Q = 0
COL_K = SB_WIDTH
COL_V = 2 * SB_WIDTH
COL_Z = 3 * SB_WIDTH
COL_XBC = COL_Z + SSM_INNER
PROJ_WIDTH = COL_XBC + CONV_DIM


def _params(n_grid_axes, vmem_mib):
    return pltpu.CompilerParams(
        dimension_semantics=("arbitrary",) * n_grid_axes,
        vmem_limit_bytes=min(vmem_mib * MIB, VMEM_LIMIT_CAP),
    )


def _rms(x, w):
    ms = jnp.mean(x * x, axis=-1, keepdims=True)
    return x * lax.rsqrt(ms + EPS) * w


def _silu(x):
    return x * jax.nn.sigmoid(x)


def _softplus(x):
    return jnp.maximum(x, 0.0) + jnp.log1p(jnp.exp(-jnp.abs(x)))


def _split_bf16(x):
    hi = x.astype(BF16)
    lo = (x - hi.astype(F32)).astype(BF16)
    return hi, lo


def _dot(a, b):
    return jnp.dot(a, b, preferred_element_type=F32)


def _dot_nt(a, b):
    return lax.dot_general(a, b, (((1,), (1,)), ((), ())), preferred_element_type=F32)


def _row_block_specs(arrays, n_steps, step_of):
    specs = []
    for a in arrays:
        rows = a.shape[0]
        pack = 2 * SUBLANES
        br = -(-rows // (n_steps * pack)) * pack
        while rows % br:
            br += pack
        last = rows // br - 1
        specs.append(pl.BlockSpec(
            (br, a.shape[1]), lambda *ids, last=last: (jnp.minimum(step_of(*ids), last), 0)))
    return specs


def _ffn_kernel(x_ref, pre_ref, wg_ref, wu_ref, wd_ref, wg_hbm, wu_hbm, wd_hbm, post_ref, o_ref, u_ref, wgt_ref,
                wut_ref, wdt_ref, sem, *, n_main, f_main, ft, n_chunks, mc):
    i = pl.program_id(0)
    j = pl.program_id(1)

    def tail_copies():
        return (pltpu.make_async_copy(wg_hbm.at[:, pl.ds(f_main, ft)], wgt_ref, sem.at[0]),
                pltpu.make_async_copy(wu_hbm.at[:, pl.ds(f_main, ft)], wut_ref, sem.at[1]),
                pltpu.make_async_copy(wd_hbm.at[pl.ds(f_main, ft), :], wdt_ref, sem.at[2]))

    @pl.when(jnp.logical_and(i == 0, j == 0))
    def _():
        for copy in tail_copies():
            copy.start()

    @pl.when(jnp.logical_and(i == 0, j == n_main))
    def _():
        for copy in tail_copies():
            copy.wait()

    def step(first, last):
        wg, wu, wd = (wgt_ref, wut_ref, wdt_ref) if last else (wg_ref, wu_ref, wd_ref)
        chunk = mc if first or last else n_chunks * mc
        for m in range(n_chunks * mc // chunk):
            rows = slice(m * chunk, (m + 1) * chunk)
            if first:
                u = _rms(x_ref[rows, :], pre_ref[...]).astype(BF16)
                u_ref[rows, :] = u
            else:
                u = u_ref[rows, :]
            act = (_silu(_dot(u, wg[...])) * _dot(u, wu[...])).astype(BF16)
            y = _dot(act, wd[...])
            if not first:
                y = o_ref[rows, :] + y
            if last:
                y = x_ref[rows, :] + 0.5 * _rms(y, post_ref[...])
            o_ref[rows, :] = y

    pl.when(j == 0)(functools.partial(step, True, False))
    pl.when(jnp.logical_and(j > 0, j < n_main))(functools.partial(step, False, False))
    pl.when(j == n_main)(functools.partial(step, False, True))


def _ffn(h, pre, wg, wu, wd, post, *, tm, tf, mc):
    t, d = h.shape
    f = wg.shape[1]
    n_main = (f - 1) // tf
    f_main = n_main * tf
    ft = f - f_main
    assert n_main >= 1 and ft % LANES == 0 and tm % mc == 0
    vmem = (4 * tm * d * 4 + tm * d * 2 + 2 * 3 * d * tf * 2 + 3 * d * ft * 2 + 4 * mc * tf * 4
            + 2 * mc * d * 4) // MIB + 4
    main_col = lambda i, j: (0, jnp.minimum(j, n_main - 1))
    main_row = lambda i, j: (jnp.minimum(j, n_main - 1), 0)
    const = lambda shape: pl.BlockSpec(shape, lambda i, j: (0, 0), pipeline_mode=pl.Buffered(1))
    hbm = pl.BlockSpec(memory_space=pl.ANY)
    return pl.pallas_call(
        functools.partial(_ffn_kernel, n_main=n_main, f_main=f_main, ft=ft, n_chunks=tm // mc, mc=mc),
        grid=(t // tm, n_main + 1),
        in_specs=[
            pl.BlockSpec((tm, d), lambda i, j: (i, 0)),
            const((1, d)),
            pl.BlockSpec((d, tf), main_col),
            pl.BlockSpec((d, tf), main_col),
            pl.BlockSpec((tf, d), main_row),
            hbm,
            hbm,
            hbm,
            const((1, d)),
        ],
        out_specs=pl.BlockSpec((tm, d), lambda i, j: (i, 0)),
        out_shape=jax.ShapeDtypeStruct((t, d), F32),
        scratch_shapes=[
            pltpu.VMEM((tm, d), BF16),
            pltpu.VMEM((d, ft), BF16),
            pltpu.VMEM((d, ft), BF16),
            pltpu.VMEM((ft, d), BF16),
            pltpu.SemaphoreType.DMA((3,)),
        ],
        compiler_params=_params(2, vmem),
        name="ffn",
    )(h, pre, wg, wu, wd, wg, wu, wd, post)


def _in_proj_kernel(x_ref, pre_ref, w_ref, wx_ref, o_ref, ox_ref, u_ref, wxp_ref, *, n_extra, n_chunks, mc):
    j = pl.program_id(1)

    @pl.when(j == 0)
    def _():
        wxp_ref[...] = jnp.zeros_like(wxp_ref)
        wxp_ref[0:n_extra, :] = wx_ref[...]
        for m in range(n_chunks):
            rows = slice(m * mc, (m + 1) * mc)
            u = _rms(x_ref[rows, :], pre_ref[...]).astype(BF16)
            u_ref[rows, :] = u
            o_ref[rows, :] = _dot_nt(u, w_ref[...])
            ox_ref[rows, :] = _dot_nt(u, wxp_ref[...])

    @pl.when(j > 0)
    def _():
        o_ref[...] = _dot_nt(u_ref[...], w_ref[...])


def _in_proj(h, pre, w_t, *, n, tm, tn, mc):
    t, d = h.shape
    n_extra = w_t.shape[0] - n
    assert n % tn == 0 and 0 < n_extra <= LANES and n % n_extra == 0 and n_extra % (2 * SUBLANES) == 0
    assert tm % mc == 0
    vmem = (2 * tm * d * 4 + tm * d * 2 + 2 * d * tn * 2 + 3 * tm * tn * 4) // MIB + 8
    return pl.pallas_call(
        functools.partial(_in_proj_kernel, n_extra=n_extra, n_chunks=tm // mc, mc=mc),
        grid=(t // tm, n // tn),
        in_specs=[
            pl.BlockSpec((tm, d), lambda i, j: (i, 0)),
            pl.BlockSpec((1, d), lambda i, j: (0, 0)),
            pl.BlockSpec((tn, d), lambda i, j: (j, 0)),
            pl.BlockSpec((n_extra, d), lambda i, j: (n // n_extra, 0)),
        ],
        out_specs=[
            pl.BlockSpec((tm, tn), lambda i, j: (i, j)),
            pl.BlockSpec((tm, LANES), lambda i, j: (i, 0)),
        ],
        out_shape=[jax.ShapeDtypeStruct((t, n), F32), jax.ShapeDtypeStruct((t, LANES), F32)],
        scratch_shapes=[pltpu.VMEM((tm, d), BF16), pltpu.VMEM((LANES, d), BF16)],
        compiler_params=_params(2, vmem),
        name="in_proj",
    )(h, pre, w_t, w_t)


def _suffix_sum_matrix():
    j = jnp.arange(LANES)[:, None]
    s = jnp.arange(LANES)[None, :]
    half = jnp.concatenate([(j > s), jnp.ones((LANES, LANES), bool)], axis=1)
    return jnp.concatenate([half, half], axis=0).astype(BF16)


def _sb_kernel(q_ref, k_ref, v_ref, w_ref, *rest, n_cast, seq, tq, nh, scale):
    cast_in, (o_ref, *cast_out) = rest[:n_cast], rest[n_cast:2 * n_cast + 1]
    qs_ref, ks_ref, vs_ref, acc_ref, r_ref = rest[2 * n_cast + 1:]
    for src, dst in zip(cast_in, cast_out):
        dst[...] = src[...].astype(BF16)
    qs_ref[...] = q_ref[...].astype(BF16)
    ks_ref[...] = k_ref[...].astype(BF16)
    vs_ref[...] = v_ref[...].astype(BF16)
    w = w_ref[...]
    n_sub = tq // LANES
    heads = [slice(h * SB_HEAD_DIM, (h + 1) * SB_HEAD_DIM) for h in range(nh)]
    def qk(q0, k0):
        return [_dot_nt(qs_ref[q0:q0 + tq, hs], ks_ref[k0:k0 + tq, hs]) for hs in heads]

    def scores(d, c, diag):
        row0 = c * LANES if diag else 0
        mask = None
        if diag:
            mask = (lax.broadcasted_iota(jnp.int32, (tq - row0, LANES), 1)
                    < lax.broadcasted_iota(jnp.int32, (tq - row0, LANES), 0))
        per_head = []
        for dh_full in d:
            dh = dh_full[row0:, c * LANES:(c + 1) * LANES]
            z = dh * scale
            e = jnp.log(1.0 + jnp.exp2(jnp.abs(dh) * (-scale * LOG2E)))
            lb = jnp.minimum(z, 0.0) - e
            lk = lb - z
            per_head.append((lb, lk if mask is None else jnp.where(mask, lk, 0.0)))
        return row0, mask, per_head

    def accumulate(sc, a_parts, c):
        row0, mask, per_head = sc
        rows = slice(row0, tq)
        for h, hs in enumerate(heads):
            lb, lk = per_head[h]
            hi, lo = _split_bf16(lk)
            er = _dot(jnp.concatenate([hi, lo], axis=1), w)
            r = r_ref[rows, hs]
            a = jnp.exp(lb + er[:, :LANES] + r)
            r_ref[rows, hs] = r + er[:, LANES:]
            if mask is not None:
                a = jnp.where(mask, a, 0.0)
            a = a.astype(BF16)
            if row0:
                a = jnp.concatenate([jnp.zeros((row0, LANES), BF16), a], axis=0)
            a_parts[h][c] = a

    n_q = seq // tq
    chunks = [(qi * tq, kj * tq) for qi in range(n_q) for kj in range(qi, -1, -1)]
    d = qk(*chunks[0])
    sc = [scores(d, c, True) for c in range(n_sub)]
    for n, (q0, k0) in enumerate(chunks):
        if k0 == q0:
            acc_ref[...] = jnp.zeros_like(acc_ref)
            r_ref[...] = jnp.zeros_like(r_ref)
        nxt = chunks[n + 1] if n + 1 < len(chunks) else None
        if nxt is not None:
            d = qk(*nxt)
        sc_next = [None] * n_sub
        a_parts = [[None] * n_sub for _ in heads]
        for c in reversed(range(n_sub)):
            accumulate(sc[c], a_parts, c)
            if nxt is not None:
                sc_next[c] = scores(d, c, nxt[0] == nxt[1])
        for h, hs in enumerate(heads):
            acc_ref[:, hs] += _dot(jnp.concatenate(a_parts[h], axis=1), vs_ref[k0:k0 + tq, hs])
        if k0 == 0:
            o_ref[q0:q0 + tq, :] = acc_ref[...]
        sc = sc_next


def _sb_attention(proj, to_cast, *, tq, nh):
    b, s, _ = proj.shape
    wd = nh * SB_HEAD_DIM
    col0 = (COL_Q // wd, COL_K // wd, COL_V // wd)
    n_heads_steps = SB_HEADS // nh
    cast_specs = _row_block_specs(to_cast, b * n_heads_steps, lambda i, h: i * n_heads_steps + h)
    outs = pl.pallas_call(
        functools.partial(_sb_kernel, n_cast=len(to_cast), seq=s, tq=tq, nh=nh, scale=SB_HEAD_DIM ** -0.5),
        grid=(b, n_heads_steps),
        in_specs=[
            pl.BlockSpec((None, s, wd), lambda i, h: (i, 0, col0[0] + h)),
            pl.BlockSpec((None, s, wd), lambda i, h: (i, 0, col0[1] + h)),
            pl.BlockSpec((None, s, wd), lambda i, h: (i, 0, col0[2] + h)),
            pl.BlockSpec((2 * LANES, 2 * LANES), lambda i, h: (0, 0)),
            *cast_specs,
        ],
        out_specs=[pl.BlockSpec((None, s, wd), lambda i, h: (i, 0, h)), *cast_specs],
        out_shape=[jax.ShapeDtypeStruct((b, s, SB_WIDTH), F32),
                   *[jax.ShapeDtypeStruct(a.shape, BF16) for a in to_cast]],
        scratch_shapes=[
            pltpu.VMEM((s, wd), BF16),
            pltpu.VMEM((s, wd), BF16),
            pltpu.VMEM((s, wd), BF16),
            pltpu.VMEM((tq, wd), F32),
            pltpu.VMEM((tq, nh * LANES), F32),
        ],
        compiler_params=_params(2, SMALL_CALL_VMEM_MIB),
        name="sb_attn",
    )(proj, proj, proj, _suffix_sum_matrix(), *to_cast)
    return outs[0], outs[1:]


def _ssd_kernel(z_ref, xbc_ref, dt_ref, convw_ref, convb_ref, dtb_ref, alog_ref, dskip_ref, norm_ref,
                tri_ref, ex_ref, o_ref, xpad_ref, state_ref, *, lc, n_chunks):
    @pl.when(pl.program_id(1) == 0)
    def _():
        state_ref[...] = jnp.zeros_like(state_ref)
        xpad_ref[:, 0:SUBLANES, :] = jnp.zeros((CONV_DIM // LANES, SUBLANES, LANES), F32)

    ri = lax.broadcasted_iota(jnp.int32, (lc, lc), 0)
    ci = lax.broadcasted_iota(jnp.int32, (lc, lc), 1)
    tril = ri >= ci
    head_of_lane = lax.broadcasted_iota(jnp.int32, (1, SSM_GROUP_WIDTH), 1) // SSM_HEAD_DIM
    n_x, n_b = SSM_INNER // LANES, SSM_GROUPS * SSM_STATE // LANES

    for sub in range(n_chunks):
        rows = slice(sub * lc, (sub + 1) * lc)
        act = []
        for c in range(CONV_DIM // LANES):
            cols = slice(c * LANES, (c + 1) * LANES)
            xpad_ref[c, SUBLANES:SUBLANES + lc, :] = xbc_ref[rows, cols]
            conv = convb_ref[:, cols]
            for k in range(SSM_CONV):
                shift = SUBLANES - (SSM_CONV - 1) + k
                conv = conv + convw_ref[k:k + 1, cols] * xpad_ref[c, shift:shift + lc, :]
            xpad_ref[c, 0:SUBLANES, :] = xpad_ref[c, lc:lc + SUBLANES, :]
            act.append(_silu(conv))
        xs = jnp.concatenate(act[:n_x], axis=1)
        bm = jnp.concatenate(act[n_x:n_x + n_b], axis=1)
        cm = jnp.concatenate(act[n_x + n_b:], axis=1)

        dt = _softplus(dt_ref[rows, :] + dtb_ref[...])
        da = dt * (-jnp.exp(alog_ref[...]))
        da_hi, da_lo = _split_bf16(da)
        cs2 = _dot(tri_ref[...], jnp.concatenate([da_hi, da_lo], axis=1))
        cs = cs2[:, :LANES] + cs2[:, LANES:]
        cs_t = cs.T

        ex = ex_ref[...]
        dt_x = _dot(jnp.concatenate(_split_bf16(dt), axis=1), ex)
        cs_x = _dot(jnp.concatenate(_split_bf16(cs), axis=1), ex)
        cs_last = cs_x[lc - 1:lc, :]
        xd = xs * dt_x
        xd_bf = xd.astype(BF16)
        xd_end = (xd * jnp.exp(cs_last - cs_x)).astype(BF16)
        in_decay = jnp.exp(cs_x)
        chunk_decay = jnp.exp(cs_last)

        ys = []
        for g in range(SSM_GROUPS):
            gl = slice(g * SSM_GROUP_WIDTH, (g + 1) * SSM_GROUP_WIDTH)
            sl = slice(g * SSM_STATE, (g + 1) * SSM_STATE)
            bm_g = bm[:, sl]
            cm_g = cm[:, sl].astype(BF16)
            cb = _dot_nt(cm_g, bm_g.astype(BF16))
            xd_g = xd_bf[:, gl]
            y_diag = jnp.zeros((lc, SSM_GROUP_WIDTH), F32)
            for r in range(SSM_HEADS_PER_GROUP):
                h = g * SSM_HEADS_PER_GROUP + r
                seg = cs[:, h:h + 1] - cs_t[h:h + 1, :]
                decay = jnp.exp(jnp.where(tril, seg, -jnp.inf))
                x_h = jnp.where(head_of_lane == r, xd_g, jnp.zeros_like(xd_g))
                y_diag = y_diag + _dot((cb * decay).astype(BF16), x_h)
            state = state_ref[g]
            y_off = _dot(cm_g, state.astype(BF16)) * in_decay[:, gl]
            state_ref[g] = state * chunk_decay[:, gl] + _dot(bm_g.T.astype(BF16), xd_end[:, gl])
            ys.append(y_diag + y_off)
        y = jnp.concatenate(ys, axis=1) + dskip_ref[...] * xs
        y = y * _silu(z_ref[rows, :])
        o_ref[rows, :] = _rms(y, norm_ref[...]).astype(o_ref.dtype)


def _ssd(proj, dt_raw, conv_w, conv_b, dt_bias, a_log, d_skip, ssm_norm, *, lc, n_chunks):
    b, s, _ = proj.shape
    ts = lc * n_chunks
    pad_heads = lambda v: jnp.pad(v.astype(F32), (0, LANES - SSM_HEADS)).reshape(1, LANES)
    tri = jnp.tril(jnp.ones((lc, lc), BF16))
    lane_head = jnp.arange(SSM_INNER)[None, :] // SSM_HEAD_DIM
    ex_half = (jnp.arange(LANES)[:, None] == lane_head).astype(BF16)
    ex = jnp.concatenate([ex_half, ex_half], axis=0)
    const = lambda shape: pl.BlockSpec(shape, lambda i, c: (0,) * len(shape))
    return pl.pallas_call(
        functools.partial(_ssd_kernel, lc=lc, n_chunks=n_chunks),
        grid=(b, s // ts),
        in_specs=[
            pl.BlockSpec((None, ts, SSM_INNER), lambda i, c: (i, c, COL_Z // SSM_INNER)),
            pl.BlockSpec((None, ts, CONV_DIM), lambda i, c: (i, c, COL_XBC // CONV_DIM)),
            pl.BlockSpec((None, ts, LANES), lambda i, c: (i, c, 0)),
            const((SSM_CONV, CONV_DIM)),
            const((1, CONV_DIM)),
            const((1, LANES)),
            const((1, LANES)),
            const((1, SSM_INNER)),
            const((1, SSM_INNER)),
            const((lc, lc)),
            const((2 * LANES, SSM_INNER)),
        ],
        out_specs=pl.BlockSpec((None, ts, SSM_INNER), lambda i, c: (i, c, 0)),
        out_shape=jax.ShapeDtypeStruct((b, s, SSM_INNER), BF16),
        scratch_shapes=[
            pltpu.VMEM((CONV_DIM // LANES, lc + SUBLANES, LANES), F32),
            pltpu.VMEM((SSM_GROUPS, SSM_STATE, SSM_GROUP_WIDTH), F32),
        ],
        compiler_params=_params(2, SMALL_CALL_VMEM_MIB),
        name="ssd",
    )(proj, proj, dt_raw, conv_w.astype(F32), conv_b.reshape(1, CONV_DIM).astype(F32), pad_heads(dt_bias),
      pad_heads(a_log), jnp.repeat(d_skip.astype(F32), SSM_HEAD_DIM).reshape(1, SSM_INNER),
      ssm_norm.reshape(1, SSM_INNER).astype(F32), tri, ex)


def _mem_kv_kernel(m_ref, norm_ref, w_ref, *rest, n_cast):
    cast_in, (o_ref, *cast_out) = rest[:n_cast], rest[n_cast:]
    for src, dst in zip(cast_in, cast_out):
        dst[...] = src[...].astype(BF16)
    o_ref[...] = _dot(_rms(m_ref[...], norm_ref[...]).astype(BF16), w_ref[...]).astype(o_ref.dtype)


def _mem_kv(mem, kv_norm, w_mkv, to_cast):
    b, n, d = mem.shape
    cast_specs = _row_block_specs(to_cast, b, lambda i: i)
    outs = pl.pallas_call(
        functools.partial(_mem_kv_kernel, n_cast=len(to_cast)),
        grid=(b,),
        in_specs=[
            pl.BlockSpec((None, n, d), lambda i: (i, 0, 0)),
            pl.BlockSpec((1, d), lambda i: (0, 0)),
            pl.BlockSpec((d, 2 * MEM_WIDTH), lambda i: (0, 0)),
            *cast_specs,
        ],
        out_specs=[pl.BlockSpec((None, n, 2 * MEM_WIDTH), lambda i: (i, 0, 0)), *cast_specs],
        out_shape=[jax.ShapeDtypeStruct((b, n, 2 * MEM_WIDTH), BF16),
                   *[jax.ShapeDtypeStruct(a.shape, BF16) for a in to_cast]],
        compiler_params=_params(1, SMALL_CALL_VMEM_MIB),
        name="mem_kv",
    )(mem, kv_norm, w_mkv, *to_cast)
    return outs[0], outs[1:]


def _mix_mem_kernel(sb_ref, y_ref, h_ref, kv_ref, sbn_ref, woa_ref, wob_ref, mixpost_ref, mempre_ref, wmq_ref,
                    wmo_ref, mempost_ref, o_ref, *, scale, n_chunks, mc):
    def stage_norm_sb(st, rows):
        st["sbn"] = _rms(sb_ref[rows, :], sbn_ref[...]).astype(BF16)

    def stage_mix(st, rows):
        st["mix"] = _dot(st["sbn"], woa_ref[...]) + _dot(y_ref[rows, :], wob_ref[...])

    def stage_norms(st, rows):
        st["h2"] = h_ref[rows, :] + _rms(st["mix"], mixpost_ref[...])
        st["u"] = _rms(st["h2"], mempre_ref[...]).astype(BF16)

    def stage_q(st, rows):
        st["q"] = _dot(st["u"], wmq_ref[...])

    def stage_attention(st, rows):
        heads = []
        for hd in range(MEM_HEADS):
            sl = slice(hd * MEM_HEAD_DIM, (hd + 1) * MEM_HEAD_DIM)
            k_h = kv_ref[:, sl]
            v_h = kv_ref[:, MEM_WIDTH + hd * MEM_HEAD_DIM:MEM_WIDTH + (hd + 1) * MEM_HEAD_DIM]
            sc = _dot_nt(st["q"][:, sl].astype(BF16), k_h) * scale
            e = jnp.exp(sc - jnp.max(sc, axis=-1, keepdims=True))
            p = e / jnp.sum(e, axis=-1, keepdims=True)
            heads.append(_dot(p.astype(BF16), v_h))
        st["att"] = jnp.concatenate(heads, axis=1).astype(BF16)

    def stage_out(st, rows):
        st["mo"] = _dot(st["att"], wmo_ref[...])

    def stage_final(st, rows):
        o_ref[rows, :] = st["h2"] + _rms(st["mo"], mempost_ref[...])

    stages = (stage_norm_sb, stage_mix, stage_norms, stage_q, stage_attention, stage_out, stage_final)
    states = [{} for _ in range(n_chunks)]
    for tick in range(len(stages) + n_chunks - 1):
        for m in range(n_chunks):
            if 0 <= tick - m < len(stages):
                stages[tick - m](states[m], slice(m * mc, (m + 1) * mc))


def _mix_mem(sb, y, h, kv, sb_norm, w_out, mix_post, mem_pre, w_mq, w_mo, mem_post, *, tm, mc, seq):
    t, d = h.shape
    n_mem = kv.shape[1]
    steps_per_batch = seq // tm
    assert SB_WIDTH == SSM_INNER and w_out.shape == (SB_WIDTH + SSM_INNER, d)
    const = lambda shape, idx=None: pl.BlockSpec(shape, lambda i: idx or (0,) * len(shape),
                                                 pipeline_mode=pl.Buffered(1))
    return pl.pallas_call(
        functools.partial(_mix_mem_kernel, scale=MEM_HEAD_DIM ** -0.5, n_chunks=tm // mc, mc=mc),
        grid=(t // tm,),
        in_specs=[
            pl.BlockSpec((tm, SB_WIDTH), lambda i: (i, 0)),
            pl.BlockSpec((tm, SSM_INNER), lambda i: (i, 0)),
            pl.BlockSpec((tm, d), lambda i: (i, 0)),
            pl.BlockSpec((None, n_mem, 2 * MEM_WIDTH), lambda i: (i // steps_per_batch, 0, 0)),
            const((1, SB_WIDTH)),
            const((SB_WIDTH, d), (0, 0)),
            const((SSM_INNER, d), (1, 0)),
            const((1, d)),
            const((1, d)),
            const((d, MEM_WIDTH)),
            const((MEM_WIDTH, d)),
            const((1, d)),
        ],
        out_specs=pl.BlockSpec((tm, d), lambda i: (i, 0)),
        out_shape=jax.ShapeDtypeStruct((t, d), F32),
        compiler_params=_params(1, MIX_VMEM_MIB),
        name="mix_mem",
    )(sb, y, h, kv, sb_norm, w_out, w_out, mix_post, mem_pre, w_mq, w_mo, mem_post)


FFN_TM = 1024
FFN_TF = 512
FFN_MC = 2 * MXU_DIM
PROJ_TM = 1024
PROJ_TN = 6 * MXU_DIM
PROJ_MC = 2 * MXU_DIM
SB_TQ = 512
SB_NH = 1
SSD_CHUNK = 128
SSD_CHUNKS_PER_STEP = 2
MIX_TM = 512
MIX_MC = MXU_DIM


def _row(v):
    return v.reshape(1, -1).astype(F32)


def _layer(h, mem, p):
    b, s, d = h.shape
    t = b * s

    def ffn(hh, pre, wg, wu, wd, post):
        tm = min(FFN_TM, t)
        return _ffn(hh, _row(pre), wg, wu, wd, _row(post), tm=tm, tf=FFN_TF, mc=min(FFN_MC, tm))

    h1 = ffn(h.reshape(t, d), p["ffn1_pre"], p["ffn1_wg"].astype(BF16), p["ffn1_wu"].astype(BF16),
             p["ffn1_wd"].astype(BF16), p["ffn1_post"])

    kv, (w_in_t,) = _mem_kv(mem, _row(p["mem_kv_norm"]), p["w_mkv"].astype(BF16), (p["w_in"].T,))
    proj, dt_raw = _in_proj(h1, _row(p["mix_pre"]), w_in_t, n=PROJ_WIDTH, tm=min(PROJ_TM, t), tn=PROJ_TN,
                            mc=min(PROJ_MC, t))
    proj = proj.reshape(b, s, PROJ_WIDTH)
    later = (p["ffn2_wg"], p["ffn2_wu"], p["ffn2_wd"], p["w_out"], p["w_mq"], p["w_mo"])
    sb, (wg2, wu2, wd2, w_out, w_mq, w_mo) = _sb_attention(proj, later, tq=min(SB_TQ, s), nh=SB_NH)
    y = _ssd(proj, dt_raw.reshape(b, s, LANES), p["conv_w"], p["conv_b"], p["dt_bias"], p["a_log"], p["d_skip"],
             p["ssm_norm"], lc=min(SSD_CHUNK, s), n_chunks=SSD_CHUNKS_PER_STEP)

    h3 = _mix_mem(sb.reshape(t, SB_WIDTH), y.reshape(t, SSM_INNER), h1, kv, _row(p["sb_norm"]), w_out,
                  _row(p["mix_post"]), _row(p["mem_pre"]), w_mq, w_mo, _row(p["mem_post"]), tm=min(MIX_TM, s),
                  mc=min(MIX_MC, s), seq=s)

    out = ffn(h3, p["ffn2_pre"], wg2, wu2, wd2, p["ffn2_post"])
    return out.reshape(b, s, d)


def kernel(x, mem, ffn1_pre, ffn1_wg, ffn1_wu, ffn1_wd, ffn1_post, mix_pre, w_in, conv_w, conv_b, dt_bias, a_log,
           d_skip, sb_norm, ssm_norm, w_out, mix_post, mem_pre, mem_kv_norm, w_mq, w_mkv, w_mo, mem_post, ffn2_pre,
           ffn2_wg, ffn2_wu, ffn2_wd, ffn2_post):
    params = dict(
        ffn1_pre=ffn1_pre, ffn1_wg=ffn1_wg, ffn1_wu=ffn1_wu, ffn1_wd=ffn1_wd, ffn1_post=ffn1_post,
        mix_pre=mix_pre, w_in=w_in, conv_w=conv_w, conv_b=conv_b, dt_bias=dt_bias, a_log=a_log, d_skip=d_skip,
        sb_norm=sb_norm, ssm_norm=ssm_norm, w_out=w_out, mix_post=mix_post, mem_pre=mem_pre,
        mem_kv_norm=mem_kv_norm, w_mq=w_mq, w_mkv=w_mkv, w_mo=w_mo, mem_post=mem_post,
        ffn2_pre=ffn2_pre, ffn2_wg=ffn2_wg, ffn2_wu=ffn2_wu, ffn2_wd=ffn2_wd, ffn2_post=ffn2_post)
    h = x
    for layer in range(ffn1_pre.shape[0]):
        h = _layer(h, mem, {k: v[layer] for k, v in params.items()})
    return h
```

```python
import functools

import jax
import jax.numpy as jnp
from jax import lax
from jax.experimental import pallas as pl
from jax.experimental.pallas import tpu as pltpu

F32 = jnp.float32
BF16 = jnp.bfloat16

EPS = 1e-6
LOG2E = 1.4426950408889634
LANES = 128
SUBLANES = 8
MXU_DIM = 256
MIB = 1024 * 1024
VMEM_BYTES_V7X = 64 * MIB
VMEM_LIMIT_CAP = VMEM_BYTES_V7X - 4 * MIB
SMALL_CALL_VMEM_MIB = 40
MIX_VMEM_MIB = 52

SB_HEADS = 8
SB_HEAD_DIM = 128
SB_WIDTH = SB_HEADS * SB_HEAD_DIM
SSM_HEADS = 16
SSM_HEAD_DIM = 64
SSM_INNER = SSM_HEADS * SSM_HEAD_DIM
SSM_GROUPS = 4
SSM_HEADS_PER_GROUP = SSM_HEADS // SSM_GROUPS
SSM_STATE = 128
SSM_CONV = 4
SSM_GROUP_WIDTH = SSM_HEADS_PER_GROUP * SSM_HEAD_DIM
CONV_DIM = SSM_INNER + 2 * SSM_GROUPS * SSM_STATE
MEM_HEADS = 4
MEM_HEAD_DIM = 128
MEM_WIDTH = MEM_HEADS * MEM_HEAD_DIM

COL_Q = 0
COL_K = SB_WIDTH
COL_V = 2 * SB_WIDTH
COL_Z = 3 * SB_WIDTH
COL_XBC = COL_Z + SSM_INNER
PROJ_WIDTH = COL_XBC + CONV_DIM


def _params(n_grid_axes, vmem_mib):
    return pltpu.CompilerParams(
        dimension_semantics=("arbitrary",) * n_grid_axes,
        vmem_limit_bytes=min(vmem_mib * MIB, VMEM_LIMIT_CAP),
    )


def _rms(x, w):
    ms = jnp.mean(x * x, axis=-1, keepdims=True)
    return x * lax.rsqrt(ms + EPS) * w


def _silu(x):
    return x * jax.nn.sigmoid(x)


def _softplus(x):
    return jnp.maximum(x, 0.0) + jnp.log1p(jnp.exp(-jnp.abs(x)))


def _split_bf16(x):
    hi = x.astype(BF16)
    lo = (x - hi.astype(F32)).astype(BF16)
    return hi, lo


def _dot(a, b):
    return jnp.dot(a, b, preferred_element_type=F32)


def _dot_nt(a, b):
    return lax.dot_general(a, b, (((1,), (1,)), ((), ())), preferred_element_type=F32)


def _row_block_specs(arrays, n_steps, step_of):
    specs = []
    for a in arrays:
        rows = a.shape[0]
        pack = 2 * SUBLANES
        br = -(-rows // (n_steps * pack)) * pack
        while rows % br:
            br += pack
        last = rows // br - 1
        specs.append(pl.BlockSpec(
            (br, a.shape[1]), lambda *ids, last=last: (jnp.minimum(step_of(*ids), last), 0)))
    return specs


def _ffn_kernel(x_ref, pre_ref, wg_ref, wu_ref, wd_ref, wg_hbm, wu_hbm, wd_hbm, post_ref, o_ref, u_ref, wgt_ref,
                wut_ref, wdt_ref, sem, *, n_main, f_main, ft, n_chunks, mc):
    i = pl.program_id(0)
    j = pl.program_id(1)

    def tail_copies():
        return (pltpu.make_async_copy(wg_hbm.at[:, pl.ds(f_main, ft)], wgt_ref, sem.at[0]),
                pltpu.make_async_copy(wu_hbm.at[:, pl.ds(f_main, ft)], wut_ref, sem.at[1]),
                pltpu.make_async_copy(wd_hbm.at[pl.ds(f_main, ft), :], wdt_ref, sem.at[2]))

    @pl.when(jnp.logical_and(i == 0, j == 0))
    def _():
        for copy in tail_copies():
            copy.start()

    @pl.when(jnp.logical_and(i == 0, j == n_main))
    def _():
        for copy in tail_copies():
            copy.wait()

    def step(first, last):
        wg, wu, wd = (wgt_ref, wut_ref, wdt_ref) if last else (wg_ref, wu_ref, wd_ref)
        chunk = mc if first or last else n_chunks * mc
        for m in range(n_chunks * mc // chunk):
            rows = slice(m * chunk, (m + 1) * chunk)
            if first:
                u = _rms(x_ref[rows, :], pre_ref[...]).astype(BF16)
                u_ref[rows, :] = u
            else:
                u = u_ref[rows, :]
            act = (_silu(_dot(u, wg[...])) * _dot(u, wu[...])).astype(BF16)
            y = _dot(act, wd[...])
            if not first:
                y = o_ref[rows, :] + y
            if last:
                y = x_ref[rows, :] + 0.5 * _rms(y, post_ref[...])
            o_ref[rows, :] = y

    pl.when(j == 0)(functools.partial(step, True, False))
    pl.when(jnp.logical_and(j > 0, j < n_main))(functools.partial(step, False, False))
    pl.when(j == n_main)(functools.partial(step, False, True))


def _ffn(h, pre, wg, wu, wd, post, *, tm, tf, mc):
    t, d = h.shape
    f = wg.shape[1]
    n_main = (f - 1) // tf
    f_main = n_main * tf
    ft = f - f_main
    assert n_main >= 1 and ft % LANES == 0 and tm % mc == 0
    vmem = (4 * tm * d * 4 + tm * d * 2 + 2 * 3 * d * tf * 2 + 3 * d * ft * 2 + 4 * mc * tf * 4
            + 2 * mc * d * 4) // MIB + 4
    main_col = lambda i, j: (0, jnp.minimum(j, n_main - 1))
    main_row = lambda i, j: (jnp.minimum(j, n_main - 1), 0)
    const = lambda shape: pl.BlockSpec(shape, lambda i, j: (0, 0), pipeline_mode=pl.Buffered(1))
    hbm = pl.BlockSpec(memory_space=pl.ANY)
    return pl.pallas_call(
        functools.partial(_ffn_kernel, n_main=n_main, f_main=f_main, ft=ft, n_chunks=tm // mc, mc=mc),
        grid=(t // tm, n_main + 1),
        in_specs=[
            pl.BlockSpec((tm, d), lambda i, j: (i, 0)),
            const((1, d)),
            pl.BlockSpec((d, tf), main_col),
            pl.BlockSpec((d, tf), main_col),
            pl.BlockSpec((tf, d), main_row),
            hbm,
            hbm,
            hbm,
            const((1, d)),
        ],
        out_specs=pl.BlockSpec((tm, d), lambda i, j: (i, 0)),
        out_shape=jax.ShapeDtypeStruct((t, d), F32),
        scratch_shapes=[
            pltpu.VMEM((tm, d), BF16),
            pltpu.VMEM((d, ft), BF16),
            pltpu.VMEM((d, ft), BF16),
            pltpu.VMEM((ft, d), BF16),
            pltpu.SemaphoreType.DMA((3,)),
        ],
        compiler_params=_params(2, vmem),
        name="ffn",
    )(h, pre, wg, wu, wd, wg, wu, wd, post)


def _in_proj_kernel(x_ref, pre_ref, w_ref, wx_ref, o_ref, ox_ref, u_ref, wxp_ref, *, n_extra, n_chunks, mc):
    j = pl.program_id(1)

    @pl.when(j == 0)
    def _():
        wxp_ref[...] = jnp.zeros_like(wxp_ref)
        wxp_ref[0:n_extra, :] = wx_ref[...]
        for m in range(n_chunks):
            rows = slice(m * mc, (m + 1) * mc)
            u = _rms(x_ref[rows, :], pre_ref[...]).astype(BF16)
            u_ref[rows, :] = u
            o_ref[rows, :] = _dot_nt(u, w_ref[...])
            ox_ref[rows, :] = _dot_nt(u, wxp_ref[...])

    @pl.when(j > 0)
    def _():
        o_ref[...] = _dot_nt(u_ref[...], w_ref[...])


def _in_proj(h, pre, w_t, *, n, tm, tn, mc):
    t, d = h.shape
    n_extra = w_t.shape[0] - n
    assert n % tn == 0 and 0 < n_extra <= LANES and n % n_extra == 0 and n_extra % (2 * SUBLANES) == 0
    assert tm % mc == 0
    vmem = (2 * tm * d * 4 + tm * d * 2 + 2 * d * tn * 2 + 3 * tm * tn * 4) // MIB + 8
    return pl.pallas_call(
        functools.partial(_in_proj_kernel, n_extra=n_extra, n_chunks=tm // mc, mc=mc),
        grid=(t // tm, n // tn),
        in_specs=[
            pl.BlockSpec((tm, d), lambda i, j: (i, 0)),
            pl.BlockSpec((1, d), lambda i, j: (0, 0)),
            pl.BlockSpec((tn, d), lambda i, j: (j, 0)),
            pl.BlockSpec((n_extra, d), lambda i, j: (n // n_extra, 0)),
        ],
        out_specs=[
            pl.BlockSpec((tm, tn), lambda i, j: (i, j)),
            pl.BlockSpec((tm, LANES), lambda i, j: (i, 0)),
        ],
        out_shape=[jax.ShapeDtypeStruct((t, n), F32), jax.ShapeDtypeStruct((t, LANES), F32)],
        scratch_shapes=[pltpu.VMEM((tm, d), BF16), pltpu.VMEM((LANES, d), BF16)],
        compiler_params=_params(2, vmem),
        name="in_proj",
    )(h, pre, w_t, w_t)


def _suffix_sum_matrix():
    j = jnp.arange(LANES)[:, None]
    s = jnp.arange(LANES)[None, :]
    half = jnp.concatenate([(j > s), jnp.ones((LANES, LANES), bool)], axis=1)
    return jnp.concatenate([half, half], axis=0).astype(BF16)


def _sb_kernel(q_ref, k_ref, v_ref, w_ref, *rest, n_cast, seq, tq, nh, scale):
    cast_in, (o_ref, *cast_out) = rest[:n_cast], rest[n_cast:2 * n_cast + 1]
    qs_ref, ks_ref, vs_ref, acc_ref, r_ref = rest[2 * n_cast + 1:]
    for src, dst in zip(cast_in, cast_out):
        dst[...] = src[...].astype(BF16)
    qs_ref[...] = q_ref[...].astype(BF16)
    ks_ref[...] = k_ref[...].astype(BF16)
    vs_ref[...] = v_ref[...].astype(BF16)
    w = w_ref[...]
    n_sub = tq // LANES
    heads = [slice(h * SB_HEAD_DIM, (h + 1) * SB_HEAD_DIM) for h in range(nh)]
    def qk(q0, k0):
        return [_dot_nt(qs_ref[q0:q0 + tq, hs], ks_ref[k0:k0 + tq, hs]) for hs in heads]

    def scores(d, c, diag):
        row0 = c * LANES if diag else 0
        mask = None
        if diag:
            mask = (lax.broadcasted_iota(jnp.int32, (tq - row0, LANES), 1)
                    < lax.broadcasted_iota(jnp.int32, (tq - row0, LANES), 0))
        per_head = []
        for dh_full in d:
            dh = dh_full[row0:, c * LANES:(c + 1) * LANES]
            z = dh * scale
            e = jnp.log(1.0 + jnp.exp2(jnp.abs(dh) * (-scale * LOG2E)))
            lb = jnp.minimum(z, 0.0) - e
            lk = lb - z
            per_head.append((lb, lk if mask is None else jnp.where(mask, lk, 0.0)))
        return row0, mask, per_head

    def accumulate(sc, a_parts, c):
        row0, mask, per_head = sc
        rows = slice(row0, tq)
        for h, hs in enumerate(heads):
            lb, lk = per_head[h]
            hi, lo = _split_bf16(lk)
            er = _dot(jnp.concatenate([hi, lo], axis=1), w)
            r = r_ref[rows, hs]
            a = jnp.exp(lb + er[:, :LANES] + r)
            r_ref[rows, hs] = r + er[:, LANES:]
            if mask is not None:
                a = jnp.where(mask, a, 0.0)
            a = a.astype(BF16)
            if row0:
                a = jnp.concatenate([jnp.zeros((row0, LANES), BF16), a], axis=0)
            a_parts[h][c] = a

    n_q = seq // tq
    chunks = [(qi * tq, kj * tq) for qi in range(n_q) for kj in range(qi, -1, -1)]
    d = qk(*chunks[0])
    sc = [scores(d, c, True) for c in range(n_sub)]
    for n, (q0, k0) in enumerate(chunks):
        if k0 == q0:
            acc_ref[...] = jnp.zeros_like(acc_ref)
            r_ref[...] = jnp.zeros_like(r_ref)
        nxt = chunks[n + 1] if n + 1 < len(chunks) else None
        if nxt is not None:
            d = qk(*nxt)
        sc_next = [None] * n_sub
        a_parts = [[None] * n_sub for _ in heads]
        for c in reversed(range(n_sub)):
            accumulate(sc[c], a_parts, c)
            if nxt is not None:
                sc_next[c] = scores(d, c, nxt[0] == nxt[1])
        for h, hs in enumerate(heads):
            acc_ref[:, hs] += _dot(jnp.concatenate(a_parts[h], axis=1), vs_ref[k0:k0 + tq, hs])
        if k0 == 0:
            o_ref[q0:q0 + tq, :] = acc_ref[...]
        sc = sc_next


def _sb_attention(proj, to_cast, *, tq, nh):
    b, s, _ = proj.shape
    wd = nh * SB_HEAD_DIM
    col0 = (COL_Q // wd, COL_K // wd, COL_V // wd)
    n_heads_steps = SB_HEADS // nh
    cast_specs = _row_block_specs(to_cast, b * n_heads_steps, lambda i, h: i * n_heads_steps + h)
    outs = pl.pallas_call(
        functools.partial(_sb_kernel, n_cast=len(to_cast), seq=s, tq=tq, nh=nh, scale=SB_HEAD_DIM ** -0.5),
        grid=(b, n_heads_steps),
        in_specs=[
            pl.BlockSpec((None, s, wd), lambda i, h: (i, 0, col0[0] + h)),
            pl.BlockSpec((None, s, wd), lambda i, h: (i, 0, col0[1] + h)),
            pl.BlockSpec((None, s, wd), lambda i, h: (i, 0, col0[2] + h)),
            pl.BlockSpec((2 * LANES, 2 * LANES), lambda i, h: (0, 0)),
            *cast_specs,
        ],
        out_specs=[pl.BlockSpec((None, s, wd), lambda i, h: (i, 0, h)), *cast_specs],
        out_shape=[jax.ShapeDtypeStruct((b, s, SB_WIDTH), F32),
                   *[jax.ShapeDtypeStruct(a.shape, BF16) for a in to_cast]],
        scratch_shapes=[
            pltpu.VMEM((s, wd), BF16),
            pltpu.VMEM((s, wd), BF16),
            pltpu.VMEM((s, wd), BF16),
            pltpu.VMEM((tq, wd), F32),
            pltpu.VMEM((tq, nh * LANES), F32),
        ],
        compiler_params=_params(2, SMALL_CALL_VMEM_MIB),
        name="sb_attn",
    )(proj, proj, proj, _suffix_sum_matrix(), *to_cast)
    return outs[0], outs[1:]


def _ssd_kernel(z_ref, xbc_ref, dt_ref, convw_ref, convb_ref, dtb_ref, alog_ref, dskip_ref, norm_ref,
                tri_ref, ex_ref, o_ref, xpad_ref, state_ref, *, lc, n_chunks):
    @pl.when(pl.program_id(1) == 0)
    def _():
        state_ref[...] = jnp.zeros_like(state_ref)
        xpad_ref[:, 0:SUBLANES, :] = jnp.zeros((CONV_DIM // LANES, SUBLANES, LANES), F32)

    ri = lax.broadcasted_iota(jnp.int32, (lc, lc), 0)
    ci = lax.broadcasted_iota(jnp.int32, (lc, lc), 1)
    tril = ri >= ci
    head_of_lane = lax.broadcasted_iota(jnp.int32, (1, SSM_GROUP_WIDTH), 1) // SSM_HEAD_DIM
    n_x, n_b = SSM_INNER // LANES, SSM_GROUPS * SSM_STATE // LANES

    for sub in range(n_chunks):
        rows = slice(sub * lc, (sub + 1) * lc)
        act = []
        for c in range(CONV_DIM // LANES):
            cols = slice(c * LANES, (c + 1) * LANES)
            xpad_ref[c, SUBLANES:SUBLANES + lc, :] = xbc_ref[rows, cols]
            conv = convb_ref[:, cols]
            for k in range(SSM_CONV):
                shift = SUBLANES - (SSM_CONV - 1) + k
                conv = conv + convw_ref[k:k + 1, cols] * xpad_ref[c, shift:shift + lc, :]
            xpad_ref[c, 0:SUBLANES, :] = xpad_ref[c, lc:lc + SUBLANES, :]
            act.append(_silu(conv))
        xs = jnp.concatenate(act[:n_x], axis=1)
        bm = jnp.concatenate(act[n_x:n_x + n_b], axis=1)
        cm = jnp.concatenate(act[n_x + n_b:], axis=1)

        dt = _softplus(dt_ref[rows, :] + dtb_ref[...])
        da = dt * (-jnp.exp(alog_ref[...]))
        da_hi, da_lo = _split_bf16(da)
        cs2 = _dot(tri_ref[...], jnp.concatenate([da_hi, da_lo], axis=1))
        cs = cs2[:, :LANES] + cs2[:, LANES:]
        cs_t = cs.T

        ex = ex_ref[...]
        dt_x = _dot(jnp.concatenate(_split_bf16(dt), axis=1), ex)
        cs_x = _dot(jnp.concatenate(_split_bf16(cs), axis=1), ex)
        cs_last = cs_x[lc - 1:lc, :]
        xd = xs * dt_x
        xd_bf = xd.astype(BF16)
        xd_end = (xd * jnp.exp(cs_last - cs_x)).astype(BF16)
        in_decay = jnp.exp(cs_x)
        chunk_decay = jnp.exp(cs_last)

        ys = []
        for g in range(SSM_GROUPS):
            gl = slice(g * SSM_GROUP_WIDTH, (g + 1) * SSM_GROUP_WIDTH)
            sl = slice(g * SSM_STATE, (g + 1) * SSM_STATE)
            bm_g = bm[:, sl]
            cm_g = cm[:, sl].astype(BF16)
            cb = _dot_nt(cm_g, bm_g.astype(BF16))
            xd_g = xd_bf[:, gl]
            y_diag = jnp.zeros((lc, SSM_GROUP_WIDTH), F32)
            for r in range(SSM_HEADS_PER_GROUP):
                h = g * SSM_HEADS_PER_GROUP + r
                seg = cs[:, h:h + 1] - cs_t[h:h + 1, :]
                decay = jnp.exp(jnp.where(tril, seg, -jnp.inf))
                x_h = jnp.where(head_of_lane == r, xd_g, jnp.zeros_like(xd_g))
                y_diag = y_diag + _dot((cb * decay).astype(BF16), x_h)
            state = state_ref[g]
            y_off = _dot(cm_g, state.astype(BF16)) * in_decay[:, gl]
            state_ref[g] = state * chunk_decay[:, gl] + _dot(bm_g.T.astype(BF16), xd_end[:, gl])
            ys.append(y_diag + y_off)
        y = jnp.concatenate(ys, axis=1) + dskip_ref[...] * xs
        y = y * _silu(z_ref[rows, :])
        o_ref[rows, :] = _rms(y, norm_ref[...]).astype(o_ref.dtype)


def _ssd(proj, dt_raw, conv_w, conv_b, dt_bias, a_log, d_skip, ssm_norm, *, lc, n_chunks):
    b, s, _ = proj.shape
    ts = lc * n_chunks
    pad_heads = lambda v: jnp.pad(v.astype(F32), (0, LANES - SSM_HEADS)).reshape(1, LANES)
    tri = jnp.tril(jnp.ones((lc, lc), BF16))
    lane_head = jnp.arange(SSM_INNER)[None, :] // SSM_HEAD_DIM
    ex_half = (jnp.arange(LANES)[:, None] == lane_head).astype(BF16)
    ex = jnp.concatenate([ex_half, ex_half], axis=0)
    const = lambda shape: pl.BlockSpec(shape, lambda i, c: (0,) * len(shape))
    return pl.pallas_call(
        functools.partial(_ssd_kernel, lc=lc, n_chunks=n_chunks),
        grid=(b, s // ts),
        in_specs=[
            pl.BlockSpec((None, ts, SSM_INNER), lambda i, c: (i, c, COL_Z // SSM_INNER)),
            pl.BlockSpec((None, ts, CONV_DIM), lambda i, c: (i, c, COL_XBC // CONV_DIM)),
            pl.BlockSpec((None, ts, LANES), lambda i, c: (i, c, 0)),
            const((SSM_CONV, CONV_DIM)),
            const((1, CONV_DIM)),
            const((1, LANES)),
            const((1, LANES)),
            const((1, SSM_INNER)),
            const((1, SSM_INNER)),
            const((lc, lc)),
            const((2 * LANES, SSM_INNER)),
        ],
        out_specs=pl.BlockSpec((None, ts, SSM_INNER), lambda i, c: (i, c, 0)),
        out_shape=jax.ShapeDtypeStruct((b, s, SSM_INNER), BF16),
        scratch_shapes=[
            pltpu.VMEM((CONV_DIM // LANES, lc + SUBLANES, LANES), F32),
            pltpu.VMEM((SSM_GROUPS, SSM_STATE, SSM_GROUP_WIDTH), F32),
        ],
        compiler_params=_params(2, SMALL_CALL_VMEM_MIB),
        name="ssd",
    )(proj, proj, dt_raw, conv_w.astype(F32), conv_b.reshape(1, CONV_DIM).astype(F32), pad_heads(dt_bias),
      pad_heads(a_log), jnp.repeat(d_skip.astype(F32), SSM_HEAD_DIM).reshape(1, SSM_INNER),
      ssm_norm.reshape(1, SSM_INNER).astype(F32), tri, ex)


def _mem_kv_kernel(m_ref, norm_ref, w_ref, *rest, n_cast):
    cast_in, (o_ref, *cast_out) = rest[:n_cast], rest[n_cast:]
    for src, dst in zip(cast_in, cast_out):
        dst[...] = src[...].astype(BF16)
    o_ref[...] = _dot(_rms(m_ref[...], norm_ref[...]).astype(BF16), w_ref[...]).astype(o_ref.dtype)


def _mem_kv(mem, kv_norm, w_mkv, to_cast):
    b, n, d = mem.shape
    cast_specs = _row_block_specs(to_cast, b, lambda i: i)
    outs = pl.pallas_call(
        functools.partial(_mem_kv_kernel, n_cast=len(to_cast)),
        grid=(b,),
        in_specs=[
            pl.BlockSpec((None, n, d), lambda i: (i, 0, 0)),
            pl.BlockSpec((1, d), lambda i: (0, 0)),
            pl.BlockSpec((d, 2 * MEM_WIDTH), lambda i: (0, 0)),
            *cast_specs,
        ],
        out_specs=[pl.BlockSpec((None, n, 2 * MEM_WIDTH), lambda i: (i, 0, 0)), *cast_specs],
        out_shape=[jax.ShapeDtypeStruct((b, n, 2 * MEM_WIDTH), BF16),
                   *[jax.ShapeDtypeStruct(a.shape, BF16) for a in to_cast]],
        compiler_params=_params(1, SMALL_CALL_VMEM_MIB),
        name="mem_kv",
    )(mem, kv_norm, w_mkv, *to_cast)
    return outs[0], outs[1:]


def _mix_mem_kernel(sb_ref, y_ref, h_ref, kv_ref, sbn_ref, woa_ref, wob_ref, mixpost_ref, mempre_ref, wmq_ref,
                    wmo_ref, mempost_ref, o_ref, *, scale, n_chunks, mc):
    def stage_norm_sb(st, rows):
        st["sbn"] = _rms(sb_ref[rows, :], sbn_ref[...]).astype(BF16)

    def stage_mix(st, rows):
        st["mix"] = _dot(st["sbn"], woa_ref[...]) + _dot(y_ref[rows, :], wob_ref[...])

    def stage_norms(st, rows):
        st["h2"] = h_ref[rows, :] + _rms(st["mix"], mixpost_ref[...])
        st["u"] = _rms(st["h2"], mempre_ref[...]).astype(BF16)

    def stage_q(st, rows):
        st["q"] = _dot(st["u"], wmq_ref[...])

    def stage_attention(st, rows):
        heads = []
        for hd in range(MEM_HEADS):
            sl = slice(hd * MEM_HEAD_DIM, (hd + 1) * MEM_HEAD_DIM)
            k_h = kv_ref[:, sl]
            v_h = kv_ref[:, MEM_WIDTH + hd * MEM_HEAD_DIM:MEM_WIDTH + (hd + 1) * MEM_HEAD_DIM]
            sc = _dot_nt(st["q"][:, sl].astype(BF16), k_h) * scale
            e = jnp.exp(sc - jnp.max(sc, axis=-1, keepdims=True))
            p = e / jnp.sum(e, axis=-1, keepdims=True)
            heads.append(_dot(p.astype(BF16), v_h))
        st["att"] = jnp.concatenate(heads, axis=1).astype(BF16)

    def stage_out(st, rows):
        st["mo"] = _dot(st["att"], wmo_ref[...])

    def stage_final(st, rows):
        o_ref[rows, :] = st["h2"] + _rms(st["mo"], mempost_ref[...])

    stages = (stage_norm_sb, stage_mix, stage_norms, stage_q, stage_attention, stage_out, stage_final)
    states = [{} for _ in range(n_chunks)]
    for tick in range(len(stages) + n_chunks - 1):
        for m in range(n_chunks):
            if 0 <= tick - m < len(stages):
                stages[tick - m](states[m], slice(m * mc, (m + 1) * mc))


def _mix_mem(sb, y, h, kv, sb_norm, w_out, mix_post, mem_pre, w_mq, w_mo, mem_post, *, tm, mc, seq):
    t, d = h.shape
    n_mem = kv.shape[1]
    steps_per_batch = seq // tm
    assert SB_WIDTH == SSM_INNER and w_out.shape == (SB_WIDTH + SSM_INNER, d)
    const = lambda shape, idx=None: pl.BlockSpec(shape, lambda i: idx or (0,) * len(shape),
                                                 pipeline_mode=pl.Buffered(1))
    return pl.pallas_call(
        functools.partial(_mix_mem_kernel, scale=MEM_HEAD_DIM ** -0.5, n_chunks=tm // mc, mc=mc),
        grid=(t // tm,),
        in_specs=[
            pl.BlockSpec((tm, SB_WIDTH), lambda i: (i, 0)),
            pl.BlockSpec((tm, SSM_INNER), lambda i: (i, 0)),
            pl.BlockSpec((tm, d), lambda i: (i, 0)),
            pl.BlockSpec((None, n_mem, 2 * MEM_WIDTH), lambda i: (i // steps_per_batch, 0, 0)),
            const((1, SB_WIDTH)),
            const((SB_WIDTH, d), (0, 0)),
            const((SSM_INNER, d), (1, 0)),
            const((1, d)),
            const((1, d)),
            const((d, MEM_WIDTH)),
            const((MEM_WIDTH, d)),
            const((1, d)),
        ],
        out_specs=pl.BlockSpec((tm, d), lambda i: (i, 0)),
        out_shape=jax.ShapeDtypeStruct((t, d), F32),
        compiler_params=_params(1, MIX_VMEM_MIB),
        name="mix_mem",
    )(sb, y, h, kv, sb_norm, w_out, w_out, mix_post, mem_pre, w_mq, w_mo, mem_post)


FFN_TM = 1024
FFN_TF = 512
FFN_MC = 2 * MXU_DIM
PROJ_TM = 1024
PROJ_TN = 6 * MXU_DIM
PROJ_MC = MXU_DIM
SB_TQ = 512
SB_NH = 1
SSD_CHUNK = 128
SSD_CHUNKS_PER_STEP = 4
MIX_TM = 512
MIX_MC = MXU_DIM


def _row(v):
    return v.reshape(1, -1).astype(F32)


def _layer(h, mem, p):
    b, s, d = h.shape
    t = b * s

    def ffn(hh, pre, wg, wu, wd, post):
        tm = min(FFN_TM, t)
        return _ffn(hh, _row(pre), wg, wu, wd, _row(post), tm=tm, tf=FFN_TF, mc=min(FFN_MC, tm))

    h1 = ffn(h.reshape(t, d), p["ffn1_pre"], p["ffn1_wg"].astype(BF16), p["ffn1_wu"].astype(BF16),
             p["ffn1_wd"].astype(BF16), p["ffn1_post"])

    kv, (w_in_t,) = _mem_kv(mem, _row(p["mem_kv_norm"]), p["w_mkv"].astype(BF16), (p["w_in"].T,))
    proj, dt_raw = _in_proj(h1, _row(p["mix_pre"]), w_in_t, n=PROJ_WIDTH, tm=min(PROJ_TM, t), tn=PROJ_TN,
                            mc=min(PROJ_MC, t))
    proj = proj.reshape(b, s, PROJ_WIDTH)
    later = (p["ffn2_wg"], p["ffn2_wu"], p["ffn2_wd"], p["w_out"], p["w_mq"], p["w_mo"])
    sb, (wg2, wu2, wd2, w_out, w_mq, w_mo) = _sb_attention(proj, later, tq=min(SB_TQ, s), nh=SB_NH)
    y = _ssd(proj, dt_raw.reshape(b, s, LANES), p["conv_w"], p["conv_b"], p["dt_bias"], p["a_log"], p["d_skip"],
             p["ssm_norm"], lc=min(SSD_CHUNK, s), n_chunks=SSD_CHUNKS_PER_STEP)

    h3 = _mix_mem(sb.reshape(t, SB_WIDTH), y.reshape(t, SSM_INNER), h1, kv, _row(p["sb_norm"]), w_out,
                  _row(p["mix_post"]), _row(p["mem_pre"]), w_mq, w_mo, _row(p["mem_post"]), tm=min(MIX_TM, s),
                  mc=min(MIX_MC, s), seq=s)

    out = ffn(h3, p["ffn2_pre"], wg2, wu2, wd2, p["ffn2_post"])
    return out.reshape(b, s, d)


def kernel(x, mem, ffn1_pre, ffn1_wg, ffn1_wu, ffn1_wd, ffn1_post, mix_pre, w_in, conv_w, conv_b, dt_bias, a_log,
           d_skip, sb_norm, ssm_norm, w_out, mix_post, mem_pre, mem_kv_norm, w_mq, w_mkv, w_mo, mem_post, ffn2_pre,
           ffn2_wg, ffn2_wu, ffn2_wd, ffn2_post):
    params = dict(
        ffn1_pre=ffn1_pre, ffn1_wg=ffn1_wg, ffn1_wu=ffn1_wu, ffn1_wd=ffn1_wd, ffn1_post=ffn1_post,
        mix_pre=mix_pre, w_in=w_in, conv_w=conv_w, conv_b=conv_b, dt_bias=dt_bias, a_log=a_log, d_skip=d_skip,
        sb_norm=sb_norm, ssm_norm=ssm_norm, w_out=w_out, mix_post=mix_post, mem_pre=mem_pre,
        mem_kv_norm=mem_kv_norm, w_mq=w_mq, w_mkv=w_mkv, w_mo=w_mo, mem_post=mem_post,
        ffn2_pre=ffn2_pre, ffn2_wg=ffn2_wg, ffn2_wu=ffn2_wu, ffn2_wd=ffn2_wd, ffn2_post=ffn2_post)
    h = x
    for layer in range(ffn1_pre.shape[0]):
        h = _layer(h, mem, {k: v[layer] for k, v in params.items()})
    return h
```

```python
import functools

import jax
import jax.numpy as jnp
from jax import lax
from jax.experimental import pallas as pl
from jax.experimental.pallas import tpu as pltpu

F32 = jnp.float32
BF16 = jnp.bfloat16

EPS = 1e-6
LOG2E = 1.4426950408889634
LANES = 128
SUBLANES = 8
MXU_DIM = 256
MIB = 1024 * 1024
VMEM_BYTES_V7X = 64 * MIB
VMEM_LIMIT_CAP = VMEM_BYTES_V7X - 4 * MIB
SMALL_CALL_VMEM_MIB = 40
MIX_VMEM_MIB = 52

SB_HEADS = 8
SB_HEAD_DIM = 128
SB_WIDTH = SB_HEADS * SB_HEAD_DIM
SSM_HEADS = 16
SSM_HEAD_DIM = 64
SSM_INNER = SSM_HEADS * SSM_HEAD_DIM
SSM_GROUPS = 4
SSM_HEADS_PER_GROUP = SSM_HEADS // SSM_GROUPS
SSM_STATE = 128
SSM_CONV = 4
SSM_GROUP_WIDTH = SSM_HEADS_PER_GROUP * SSM_HEAD_DIM
CONV_DIM = SSM_INNER + 2 * SSM_GROUPS * SSM_STATE
MEM_HEADS = 4
MEM_HEAD_DIM = 128
MEM_WIDTH = MEM_HEADS * MEM_HEAD_DIM

COL_Q = 0
COL_K = SB_WIDTH
COL_V = 2 * SB_WIDTH
COL_Z = 3 * SB_WIDTH
COL_XBC = COL_Z + SSM_INNER
PROJ_WIDTH = COL_XBC + CONV_DIM


def _params(n_grid_axes, vmem_mib):
    return pltpu.CompilerParams(
        dimension_semantics=("arbitrary",) * n_grid_axes,
        vmem_limit_bytes=min(vmem_mib * MIB, VMEM_LIMIT_CAP),
    )


def _rms(x, w):
    ms = jnp.mean(x * x, axis=-1, keepdims=True)
    return x * lax.rsqrt(ms + EPS) * w


def _silu(x):
    return x * jax.nn.sigmoid(x)


def _softplus(x):
    return jnp.maximum(x, 0.0) + jnp.log1p(jnp.exp(-jnp.abs(x)))


def _split_bf16(x):
    hi = x.astype(BF16)
    lo = (x - hi.astype(F32)).astype(BF16)
    return hi, lo


def _dot(a, b):
    return jnp.dot(a, b, preferred_element_type=F32)


def _dot_nt(a, b):
    return lax.dot_general(a, b, (((1,), (1,)), ((), ())), preferred_element_type=F32)


def _row_block_specs(arrays, n_steps, step_of):
    specs = []
    for a in arrays:
        rows = a.shape[0]
        pack = 2 * SUBLANES
        br = -(-rows // (n_steps * pack)) * pack
        while rows % br:
            br += pack
        last = rows // br - 1
        specs.append(pl.BlockSpec(
            (br, a.shape[1]), lambda *ids, last=last: (jnp.minimum(step_of(*ids), last), 0)))
    return specs


def _ffn_kernel(x_ref, pre_ref, wg_ref, wu_ref, wd_ref, wg_hbm, wu_hbm, wd_hbm, post_ref, o_ref, u_ref, wgt_ref,
                wut_ref, wdt_ref, sem, *, n_main, f_main, ft, n_chunks, mc):
    i = pl.program_id(0)
    j = pl.program_id(1)

    def tail_copies():
        return (pltpu.make_async_copy(wg_hbm.at[:, pl.ds(f_main, ft)], wgt_ref, sem.at[0]),
                pltpu.make_async_copy(wu_hbm.at[:, pl.ds(f_main, ft)], wut_ref, sem.at[1]),
                pltpu.make_async_copy(wd_hbm.at[pl.ds(f_main, ft), :], wdt_ref, sem.at[2]))

    @pl.when(jnp.logical_and(i == 0, j == 0))
    def _():
        for copy in tail_copies():
            copy.start()

    @pl.when(jnp.logical_and(i == 0, j == n_main))
    def _():
        for copy in tail_copies():
            copy.wait()

    def step(first, last):
        wg, wu, wd = (wgt_ref, wut_ref, wdt_ref) if last else (wg_ref, wu_ref, wd_ref)
        chunk = mc if first or last else n_chunks * mc
        for m in range(n_chunks * mc // chunk):
            rows = slice(m * chunk, (m + 1) * chunk)
            if first:
                u = _rms(x_ref[rows, :], pre_ref[...]).astype(BF16)
                u_ref[rows, :] = u
            else:
                u = u_ref[rows, :]
            act = (_silu(_dot(u, wg[...])) * _dot(u, wu[...])).astype(BF16)
            y = _dot(act, wd[...])
            if not first:
                y = o_ref[rows, :] + y
            if last:
                y = x_ref[rows, :] + 0.5 * _rms(y, post_ref[...])
            o_ref[rows, :] = y

    pl.when(j == 0)(functools.partial(step, True, False))
    pl.when(jnp.logical_and(j > 0, j < n_main))(functools.partial(step, False, False))
    pl.when(j == n_main)(functools.partial(step, False, True))


def _ffn(h, pre, wg, wu, wd, post, *, tm, tf, mc):
    t, d = h.shape
    f = wg.shape[1]
    n_main = (f - 1) // tf
    f_main = n_main * tf
    ft = f - f_main
    assert n_main >= 1 and ft % LANES == 0 and tm % mc == 0
    vmem = (4 * tm * d * 4 + tm * d * 2 + 2 * 3 * d * tf * 2 + 3 * d * ft * 2 + 4 * mc * tf * 4
            + 2 * mc * d * 4) // MIB + 4
    main_col = lambda i, j: (0, jnp.minimum(j, n_main - 1))
    main_row = lambda i, j: (jnp.minimum(j, n_main - 1), 0)
    const = lambda shape: pl.BlockSpec(shape, lambda i, j: (0, 0), pipeline_mode=pl.Buffered(1))
    hbm = pl.BlockSpec(memory_space=pl.ANY)
    return pl.pallas_call(
        functools.partial(_ffn_kernel, n_main=n_main, f_main=f_main, ft=ft, n_chunks=tm // mc, mc=mc),
        grid=(t // tm, n_main + 1),
        in_specs=[
            pl.BlockSpec((tm, d), lambda i, j: (i, 0)),
            const((1, d)),
            pl.BlockSpec((d, tf), main_col),
            pl.BlockSpec((d, tf), main_col),
            pl.BlockSpec((tf, d), main_row),
            hbm,
            hbm,
            hbm,
            const((1, d)),
        ],
        out_specs=pl.BlockSpec((tm, d), lambda i, j: (i, 0)),
        out_shape=jax.ShapeDtypeStruct((t, d), F32),
        scratch_shapes=[
            pltpu.VMEM((tm, d), BF16),
            pltpu.VMEM((d, ft), BF16),
            pltpu.VMEM((d, ft), BF16),
            pltpu.VMEM((ft, d), BF16),
            pltpu.SemaphoreType.DMA((3,)),
        ],
        compiler_params=_params(2, vmem),
        name="ffn",
    )(h, pre, wg, wu, wd, wg, wu, wd, post)


def _in_proj_kernel(x_ref, pre_ref, w_ref, wx_ref, o_ref, ox_ref, u_ref, wxp_ref, *, n_extra, n_chunks, mc):
    j = pl.program_id(1)

    @pl.when(j == 0)
    def _():
        wxp_ref[...] = jnp.zeros_like(wxp_ref)
        wxp_ref[0:n_extra, :] = wx_ref[...]
        for m in range(n_chunks):
            rows = slice(m * mc, (m + 1) * mc)
            u = _rms(x_ref[rows, :], pre_ref[...]).astype(BF16)
            u_ref[rows, :] = u
            o_ref[rows, :] = _dot_nt(u, w_ref[...])
            ox_ref[rows, :] = _dot_nt(u, wxp_ref[...])

    @pl.when(j > 0)
    def _():
        o_ref[...] = _dot_nt(u_ref[...], w_ref[...])


def _in_proj(h, pre, w_t, *, n, tm, tn, mc):
    t, d = h.shape
    n_extra = w_t.shape[0] - n
    assert n % tn == 0 and 0 < n_extra <= LANES and n % n_extra == 0 and n_extra % (2 * SUBLANES) == 0
    assert tm % mc == 0
    vmem = (2 * tm * d * 4 + tm * d * 2 + 2 * d * tn * 2 + 3 * tm * tn * 4) // MIB + 8
    return pl.pallas_call(
        functools.partial(_in_proj_kernel, n_extra=n_extra, n_chunks=tm // mc, mc=mc),
        grid=(t // tm, n // tn),
        in_specs=[
            pl.BlockSpec((tm, d), lambda i, j: (i, 0)),
            pl.BlockSpec((1, d), lambda i, j: (0, 0)),
            pl.BlockSpec((tn, d), lambda i, j: (j, 0)),
            pl.BlockSpec((n_extra, d), lambda i, j: (n // n_extra, 0)),
        ],
        out_specs=[
            pl.BlockSpec((tm, tn), lambda i, j: (i, j)),
            pl.BlockSpec((tm, LANES), lambda i, j: (i, 0)),
        ],
        out_shape=[jax.ShapeDtypeStruct((t, n), F32), jax.ShapeDtypeStruct((t, LANES), F32)],
        scratch_shapes=[pltpu.VMEM((tm, d), BF16), pltpu.VMEM((LANES, d), BF16)],
        compiler_params=_params(2, vmem),
        name="in_proj",
    )(h, pre, w_t, w_t)


def _suffix_sum_matrix():
    j = jnp.arange(LANES)[:, None]
    s = jnp.arange(LANES)[None, :]
    half = jnp.concatenate([(j > s), jnp.ones((LANES, LANES), bool)], axis=1)
    return jnp.concatenate([half, half], axis=0).astype(BF16)


def _sb_kernel(q_ref, k_ref, v_ref, w_ref, *rest, n_cast, seq, tq, nh, scale):
    cast_in, (o_ref, *cast_out) = rest[:n_cast], rest[n_cast:2 * n_cast + 1]
    qs_ref, ks_ref, vs_ref, acc_ref, r_ref = rest[2 * n_cast + 1:]
    for src, dst in zip(cast_in, cast_out):
        dst[...] = src[...].astype(BF16)
    qs_ref[...] = q_ref[...].astype(BF16)
    ks_ref[...] = k_ref[...].astype(BF16)
    vs_ref[...] = v_ref[...].astype(BF16)
    w = w_ref[...]
    n_sub = tq // LANES
    heads = [slice(h * SB_HEAD_DIM, (h + 1) * SB_HEAD_DIM) for h in range(nh)]
    def qk(q0, k0):
        return [_dot_nt(qs_ref[q0:q0 + tq, hs], ks_ref[k0:k0 + tq, hs]) for hs in heads]

    def scores(d, c, diag):
        row0 = c * LANES if diag else 0
        mask = None
        if diag:
            mask = (lax.broadcasted_iota(jnp.int32, (tq - row0, LANES), 1)
                    < lax.broadcasted_iota(jnp.int32, (tq - row0, LANES), 0))
        per_head = []
        for dh_full in d:
            dh = dh_full[row0:, c * LANES:(c + 1) * LANES]
            z = dh * scale
            e = jnp.log(1.0 + jnp.exp2(jnp.abs(dh) * (-scale * LOG2E)))
            lb = jnp.minimum(z, 0.0) - e
            lk = lb - z
            per_head.append((lb, lk if mask is None else jnp.where(mask, lk, 0.0)))
        return row0, mask, per_head

    def accumulate(sc, a_parts, c):
        row0, mask, per_head = sc
        rows = slice(row0, tq)
        for h, hs in enumerate(heads):
            lb, lk = per_head[h]
            hi, lo = _split_bf16(lk)
            er = _dot(jnp.concatenate([hi, lo], axis=1), w)
            r = r_ref[rows, hs]
            a = jnp.exp(lb + er[:, :LANES] + r)
            r_ref[rows, hs] = r + er[:, LANES:]
            if mask is not None:
                a = jnp.where(mask, a, 0.0)
            a = a.astype(BF16)
            if row0:
                a = jnp.concatenate([jnp.zeros((row0, LANES), BF16), a], axis=0)
            a_parts[h][c] = a

    n_q = seq // tq
    chunks = [(qi * tq, kj * tq) for qi in range(n_q) for kj in range(qi, -1, -1)]
    d = qk(*chunks[0])
    sc = [scores(d, c, True) for c in range(n_sub)]
    for n, (q0, k0) in enumerate(chunks):
        if k0 == q0:
            acc_ref[...] = jnp.zeros_like(acc_ref)
            r_ref[...] = jnp.zeros_like(r_ref)
        nxt = chunks[n + 1] if n + 1 < len(chunks) else None
        if nxt is not None:
            d = qk(*nxt)
        sc_next = [None] * n_sub
        a_parts = [[None] * n_sub for _ in heads]
        for c in reversed(range(n_sub)):
            accumulate(sc[c], a_parts, c)
            if nxt is not None:
                sc_next[c] = scores(d, c, nxt[0] == nxt[1])
        for h, hs in enumerate(heads):
            acc_ref[:, hs] += _dot(jnp.concatenate(a_parts[h], axis=1), vs_ref[k0:k0 + tq, hs])
        if k0 == 0:
            o_ref[q0:q0 + tq, :] = acc_ref[...]
        sc = sc_next


def _sb_attention(proj, to_cast, *, tq, nh):
    b, s, _ = proj.shape
    wd = nh * SB_HEAD_DIM
    col0 = (COL_Q // wd, COL_K // wd, COL_V // wd)
    n_heads_steps = SB_HEADS // nh
    cast_specs = _row_block_specs(to_cast, b * n_heads_steps, lambda i, h: i * n_heads_steps + h)
    outs = pl.pallas_call(
        functools.partial(_sb_kernel, n_cast=len(to_cast), seq=s, tq=tq, nh=nh, scale=SB_HEAD_DIM ** -0.5),
        grid=(b, n_heads_steps),
        in_specs=[
            pl.BlockSpec((None, s, wd), lambda i, h: (i, 0, col0[0] + h)),
            pl.BlockSpec((None, s, wd), lambda i, h: (i, 0, col0[1] + h)),
            pl.BlockSpec((None, s, wd), lambda i, h: (i, 0, col0[2] + h)),
            pl.BlockSpec((2 * LANES, 2 * LANES), lambda i, h: (0, 0)),
            *cast_specs,
        ],
        out_specs=[pl.BlockSpec((None, s, wd), lambda i, h: (i, 0, h)), *cast_specs],
        out_shape=[jax.ShapeDtypeStruct((b, s, SB_WIDTH), F32),
                   *[jax.ShapeDtypeStruct(a.shape, BF16) for a in to_cast]],
        scratch_shapes=[
            pltpu.VMEM((s, wd), BF16),
            pltpu.VMEM((s, wd), BF16),
            pltpu.VMEM((s, wd), BF16),
            pltpu.VMEM((tq, wd), F32),
            pltpu.VMEM((tq, nh * LANES), F32),
        ],
        compiler_params=_params(2, SMALL_CALL_VMEM_MIB),
        name="sb_attn",
    )(proj, proj, proj, _suffix_sum_matrix(), *to_cast)
    return outs[0], outs[1:]


def _ssd_kernel(z_ref, xbc_ref, dt_ref, convw_ref, convb_ref, dtb_ref, alog_ref, dskip_ref, norm_ref,
                tri_ref, ex_ref, o_ref, xpad_ref, state_ref, *, lc, n_chunks):
    @pl.when(pl.program_id(1) == 0)
    def _():
        state_ref[...] = jnp.zeros_like(state_ref)
        xpad_ref[:, 0:SUBLANES, :] = jnp.zeros((CONV_DIM // LANES, SUBLANES, LANES), F32)

    ri = lax.broadcasted_iota(jnp.int32, (lc, lc), 0)
    ci = lax.broadcasted_iota(jnp.int32, (lc, lc), 1)
    tril = ri >= ci
    head_of_lane = lax.broadcasted_iota(jnp.int32, (1, SSM_GROUP_WIDTH), 1) // SSM_HEAD_DIM
    n_x, n_b = SSM_INNER // LANES, SSM_GROUPS * SSM_STATE // LANES

    for sub in range(n_chunks):
        rows = slice(sub * lc, (sub + 1) * lc)
        act = []
        for c in range(CONV_DIM // LANES):
            cols = slice(c * LANES, (c + 1) * LANES)
            xpad_ref[c, SUBLANES:SUBLANES + lc, :] = xbc_ref[rows, cols]
            conv = convb_ref[:, cols]
            for k in range(SSM_CONV):
                shift = SUBLANES - (SSM_CONV - 1) + k
                conv = conv + convw_ref[k:k + 1, cols] * xpad_ref[c, shift:shift + lc, :]
            xpad_ref[c, 0:SUBLANES, :] = xpad_ref[c, lc:lc + SUBLANES, :]
            act.append(_silu(conv))
        xs = jnp.concatenate(act[:n_x], axis=1)
        bm = jnp.concatenate(act[n_x:n_x + n_b], axis=1)
        cm = jnp.concatenate(act[n_x + n_b:], axis=1)

        dt = _softplus(dt_ref[rows, :] + dtb_ref[...])
        da = dt * (-jnp.exp(alog_ref[...]))
        da_hi, da_lo = _split_bf16(da)
        cs2 = _dot(tri_ref[...], jnp.concatenate([da_hi, da_lo], axis=1))
        cs = cs2[:, :LANES] + cs2[:, LANES:]
        cs_t = cs.T

        ex = ex_ref[...]
        dt_x = _dot(jnp.concatenate(_split_bf16(dt), axis=1), ex)
        cs_x = _dot(jnp.concatenate(_split_bf16(cs), axis=1), ex)
        cs_last = cs_x[lc - 1:lc, :]
        xd = xs * dt_x
        xd_bf = xd.astype(BF16)
        xd_end = (xd * jnp.exp(cs_last - cs_x)).astype(BF16)
        in_decay = jnp.exp(cs_x)
        chunk_decay = jnp.exp(cs_last)

        ys = []
        for g in range(SSM_GROUPS):
            gl = slice(g * SSM_GROUP_WIDTH, (g + 1) * SSM_GROUP_WIDTH)
            sl = slice(g * SSM_STATE, (g + 1) * SSM_STATE)
            bm_g = bm[:, sl]
            cm_g = cm[:, sl].astype(BF16)
            cb = _dot_nt(cm_g, bm_g.astype(BF16))
            xd_g = xd_bf[:, gl]
            y_diag = jnp.zeros((lc, SSM_GROUP_WIDTH), F32)
            for r in range(SSM_HEADS_PER_GROUP):
                h = g * SSM_HEADS_PER_GROUP + r
                seg = cs[:, h:h + 1] - cs_t[h:h + 1, :]
                decay = jnp.exp(jnp.where(tril, seg, -jnp.inf))
                x_h = jnp.where(head_of_lane == r, xd_g, jnp.zeros_like(xd_g))
                y_diag = y_diag + _dot((cb * decay).astype(BF16), x_h)
            state = state_ref[g]
            y_off = _dot(cm_g, state.astype(BF16)) * in_decay[:, gl]
            state_ref[g] = state * chunk_decay[:, gl] + _dot(bm_g.T.astype(BF16), xd_end[:, gl])
            ys.append(y_diag + y_off)
        y = jnp.concatenate(ys, axis=1) + dskip_ref[...] * xs
        y = y * _silu(z_ref[rows, :])
        o_ref[rows, :] = _rms(y, norm_ref[...]).astype(o_ref.dtype)


def _ssd(proj, dt_raw, conv_w, conv_b, dt_bias, a_log, d_skip, ssm_norm, *, lc, n_chunks):
    b, s, _ = proj.shape
    ts = lc * n_chunks
    pad_heads = lambda v: jnp.pad(v.astype(F32), (0, LANES - SSM_HEADS)).reshape(1, LANES)
    tri = jnp.tril(jnp.ones((lc, lc), BF16))
    lane_head = jnp.arange(SSM_INNER)[None, :] // SSM_HEAD_DIM
    ex_half = (jnp.arange(LANES)[:, None] == lane_head).astype(BF16)
    ex = jnp.concatenate([ex_half, ex_half], axis=0)
    const = lambda shape: pl.BlockSpec(shape, lambda i, c: (0,) * len(shape))
    return pl.pallas_call(
        functools.partial(_ssd_kernel, lc=lc, n_chunks=n_chunks),
        grid=(b, s // ts),
        in_specs=[
            pl.BlockSpec((None, ts, SSM_INNER), lambda i, c: (i, c, COL_Z // SSM_INNER)),
            pl.BlockSpec((None, ts, CONV_DIM), lambda i, c: (i, c, COL_XBC // CONV_DIM)),
            pl.BlockSpec((None, ts, LANES), lambda i, c: (i, c, 0)),
            const((SSM_CONV, CONV_DIM)),
            const((1, CONV_DIM)),
            const((1, LANES)),
            const((1, LANES)),
            const((1, SSM_INNER)),
            const((1, SSM_INNER)),
            const((lc, lc)),
            const((2 * LANES, SSM_INNER)),
        ],
        out_specs=pl.BlockSpec((None, ts, SSM_INNER), lambda i, c: (i, c, 0)),
        out_shape=jax.ShapeDtypeStruct((b, s, SSM_INNER), BF16),
        scratch_shapes=[
            pltpu.VMEM((CONV_DIM // LANES, lc + SUBLANES, LANES), F32),
            pltpu.VMEM((SSM_GROUPS, SSM_STATE, SSM_GROUP_WIDTH), F32),
        ],
        compiler_params=_params(2, SMALL_CALL_VMEM_MIB),
        name="ssd",
    )(proj, proj, dt_raw, conv_w.astype(F32), conv_b.reshape(1, CONV_DIM).astype(F32), pad_heads(dt_bias),
      pad_heads(a_log), jnp.repeat(d_skip.astype(F32), SSM_HEAD_DIM).reshape(1, SSM_INNER),
      ssm_norm.reshape(1, SSM_INNER).astype(F32), tri, ex)


def _mem_kv_kernel(m_ref, norm_ref, w_ref, *rest, n_cast):
    cast_in, (o_ref, *cast_out) = rest[:n_cast], rest[n_cast:]
    for src, dst in zip(cast_in, cast_out):
        dst[...] = src[...].astype(BF16)
    o_ref[...] = _dot(_rms(m_ref[...], norm_ref[...]).astype(BF16), w_ref[...]).astype(o_ref.dtype)


def _mem_kv(mem, kv_norm, w_mkv, to_cast):
    b, n, d = mem.shape
    cast_specs = _row_block_specs(to_cast, b, lambda i: i)
    outs = pl.pallas_call(
        functools.partial(_mem_kv_kernel, n_cast=len(to_cast)),
        grid=(b,),
        in_specs=[
            pl.BlockSpec((None, n, d), lambda i: (i, 0, 0)),
            pl.BlockSpec((1, d), lambda i: (0, 0)),
            pl.BlockSpec((d, 2 * MEM_WIDTH), lambda i: (0, 0)),
            *cast_specs,
        ],
        out_specs=[pl.BlockSpec((None, n, 2 * MEM_WIDTH), lambda i: (i, 0, 0)), *cast_specs],
        out_shape=[jax.ShapeDtypeStruct((b, n, 2 * MEM_WIDTH), BF16),
                   *[jax.ShapeDtypeStruct(a.shape, BF16) for a in to_cast]],
        compiler_params=_params(1, SMALL_CALL_VMEM_MIB),
        name="mem_kv",
    )(mem, kv_norm, w_mkv, *to_cast)
    return outs[0], outs[1:]


def _mix_mem_kernel(sb_ref, y_ref, h_ref, kv_ref, sbn_ref, woa_ref, wob_ref, mixpost_ref, mempre_ref, wmq_ref,
                    wmo_ref, mempost_ref, o_ref, *, scale, n_chunks, mc):
    def stage_norm_sb(st, rows):
        st["sbn"] = _rms(sb_ref[rows, :], sbn_ref[...]).astype(BF16)

    def stage_mix(st, rows):
        st["mix"] = _dot(st["sbn"], woa_ref[...]) + _dot(y_ref[rows, :], wob_ref[...])

    def stage_norms(st, rows):
        st["h2"] = h_ref[rows, :] + _rms(st["mix"], mixpost_ref[...])
        st["u"] = _rms(st["h2"], mempre_ref[...]).astype(BF16)

    def stage_q(st, rows):
        st["q"] = _dot(st["u"], wmq_ref[...])

    def stage_attention(st, rows):
        heads = []
        for hd in range(MEM_HEADS):
            sl = slice(hd * MEM_HEAD_DIM, (hd + 1) * MEM_HEAD_DIM)
            k_h = kv_ref[:, sl]
            v_h = kv_ref[:, MEM_WIDTH + hd * MEM_HEAD_DIM:MEM_WIDTH + (hd + 1) * MEM_HEAD_DIM]
            sc = _dot_nt(st["q"][:, sl].astype(BF16), k_h) * scale
            e = jnp.exp(sc - jnp.max(sc, axis=-1, keepdims=True))
            p = e / jnp.sum(e, axis=-1, keepdims=True)
            heads.append(_dot(p.astype(BF16), v_h))
        st["att"] = jnp.concatenate(heads, axis=1).astype(BF16)

    def stage_out(st, rows):
        st["mo"] = _dot(st["att"], wmo_ref[...])

    def stage_final(st, rows):
        o_ref[rows, :] = st["h2"] + _rms(st["mo"], mempost_ref[...])

    stages = (stage_norm_sb, stage_mix, stage_norms, stage_q, stage_attention, stage_out, stage_final)
    states = [{} for _ in range(n_chunks)]
    for tick in range(len(stages) + n_chunks - 1):
        for m in range(n_chunks):
            if 0 <= tick - m < len(stages):
                stages[tick - m](states[m], slice(m * mc, (m + 1) * mc))


def _mix_mem(sb, y, h, kv, sb_norm, w_out, mix_post, mem_pre, w_mq, w_mo, mem_post, *, tm, mc, seq):
    t, d = h.shape
    n_mem = kv.shape[1]
    steps_per_batch = seq // tm
    assert SB_WIDTH == SSM_INNER and w_out.shape == (SB_WIDTH + SSM_INNER, d)
    const = lambda shape, idx=None: pl.BlockSpec(shape, lambda i: idx or (0,) * len(shape),
                                                 pipeline_mode=pl.Buffered(1))
    return pl.pallas_call(
        functools.partial(_mix_mem_kernel, scale=MEM_HEAD_DIM ** -0.5, n_chunks=tm // mc, mc=mc),
        grid=(t // tm,),
        in_specs=[
            pl.BlockSpec((tm, SB_WIDTH), lambda i: (i, 0)),
            pl.BlockSpec((tm, SSM_INNER), lambda i: (i, 0)),
            pl.BlockSpec((tm, d), lambda i: (i, 0)),
            pl.BlockSpec((None, n_mem, 2 * MEM_WIDTH), lambda i: (i // steps_per_batch, 0, 0)),
            const((1, SB_WIDTH)),
            const((SB_WIDTH, d), (0, 0)),
            const((SSM_INNER, d), (1, 0)),
            const((1, d)),
            const((1, d)),
            const((d, MEM_WIDTH)),
            const((MEM_WIDTH, d)),
            const((1, d)),
        ],
        out_specs=pl.BlockSpec((tm, d), lambda i: (i, 0)),
        out_shape=jax.ShapeDtypeStruct((t, d), F32),
        compiler_params=_params(1, MIX_VMEM_MIB),
        name="mix_mem",
    )(sb, y, h, kv, sb_norm, w_out, w_out, mix_post, mem_pre, w_mq, w_mo, mem_post)


FFN_TM = 1024
FFN_TF = 512
FFN_MC = 2 * MXU_DIM
PROJ_TM = 1024
PROJ_TN = 6 * MXU_DIM
PROJ_MC = MXU_DIM
SB_TQ = 512
SB_NH = 1
SSD_CHUNK = 128
SSD_CHUNKS_PER_STEP = 8
MIX_TM = 512
MIX_MC = MXU_DIM


def _row(v):
    return v.reshape(1, -1).astype(F32)


def _layer(h, mem, p):
    b, s, d = h.shape
    t = b * s

    def ffn(hh, pre, wg, wu, wd, post):
        tm = min(FFN_TM, t)
        return _ffn(hh, _row(pre), wg, wu, wd, _row(post), tm=tm, tf=FFN_TF, mc=min(FFN_MC, tm))

    h1 = ffn(h.reshape(t, d), p["ffn1_pre"], p["ffn1_wg"].astype(BF16), p["ffn1_wu"].astype(BF16),
             p["ffn1_wd"].astype(BF16), p["ffn1_post"])

    kv, (w_in_t,) = _mem_kv(mem, _row(p["mem_kv_norm"]), p["w_mkv"].astype(BF16), (p["w_in"].T,))
    proj, dt_raw = _in_proj(h1, _row(p["mix_pre"]), w_in_t, n=PROJ_WIDTH, tm=min(PROJ_TM, t), tn=PROJ_TN,
                            mc=min(PROJ_MC, t))
    proj = proj.reshape(b, s, PROJ_WIDTH)
    later = (p["ffn2_wg"], p["ffn2_wu"], p["ffn2_wd"], p["w_out"], p["w_mq"], p["w_mo"])
    sb, (wg2, wu2, wd2, w_out, w_mq, w_mo) = _sb_attention(proj, later, tq=min(SB_TQ, s), nh=SB_NH)
    y = _ssd(proj, dt_raw.reshape(b, s, LANES), p["conv_w"], p["conv_b"], p["dt_bias"], p["a_log"], p["d_skip"],
             p["ssm_norm"], lc=min(SSD_CHUNK, s), n_chunks=SSD_CHUNKS_PER_STEP)

    h3 = _mix_mem(sb.reshape(t, SB_WIDTH), y.reshape(t, SSM_INNER), h1, kv, _row(p["sb_norm"]), w_out,
                  _row(p["mix_post"]), _row(p["mem_pre"]), w_mq, w_mo, _row(p["mem_post"]), tm=min(MIX_TM, s),
                  mc=min(MIX_MC, s), seq=s)

    out = ffn(h3, p["ffn2_pre"], wg2, wu2, wd2, p["ffn2_post"])
    return out.reshape(b, s, d)


def kernel(x, mem, ffn1_pre, ffn1_wg, ffn1_wu, ffn1_wd, ffn1_post, mix_pre, w_in, conv_w, conv_b, dt_bias, a_log,
           d_skip, sb_norm, ssm_norm, w_out, mix_post, mem_pre, mem_kv_norm, w_mq, w_mkv, w_mo, mem_post, ffn2_pre,
           ffn2_wg, ffn2_wu, ffn2_wd, ffn2_post):
    params = dict(
        ffn1_pre=ffn1_pre, ffn1_wg=ffn1_wg, ffn1_wu=ffn1_wu, ffn1_wd=ffn1_wd, ffn1_post=ffn1_post,
        mix_pre=mix_pre, w_in=w_in, conv_w=conv_w, conv_b=conv_b, dt_bias=dt_bias, a_log=a_log, d_skip=d_skip,
        sb_norm=sb_norm, ssm_norm=ssm_norm, w_out=w_out, mix_post=mix_post, mem_pre=mem_pre,
        mem_kv_norm=mem_kv_norm, w_mq=w_mq, w_mkv=w_mkv, w_mo=w_mo, mem_post=mem_post,
        ffn2_pre=ffn2_pre, ffn2_wg=ffn2_wg, ffn2_wu=ffn2_wu, ffn2_wd=ffn2_wd, ffn2_post=ffn2_post)
    h = x
    for layer in range(ffn1_pre.shape[0]):
        h = _layer(h, mem, {k: v[layer] for k, v in params.items()})
    return h
```
